```python
import math
import jax, jax.numpy as jnp
from jax import lax
import numpy as np

D_MODEL = 1024
BATCH = 8
SEQ = 2048
DEPTH = 2

N_A_LAYERS = DEPTH // 2
N_B_LAYERS = DEPTH - N_A_LAYERS
M_HEADS = 4
M_V_DIM = D_MODEL // M_HEADS
M_QK_DIM = M_V_DIM // 2
M_CHUNK = 128
GATE_CAP = 15.0
M_IN_DIM = 2 * M_HEADS * M_QK_DIM + 2 * D_MODEL + 2 * M_HEADS
A_HEADS = 16
A_KV_HEADS = 4
A_GROUP = A_HEADS // A_KV_HEADS
A_HEAD_DIM = 64
WINDOW = 128
ROPE_DIM = A_HEAD_DIM // 4
ROPE_THETA = 500000.0
D_FF = 4 * D_MODEL
DEEPNORM_ALPHA = (2 * DEPTH) ** 0.25
DEEPNORM_BETA = (8 * DEPTH) ** -0.25
LN_EPS = 1e-5
RMS_EPS = 1e-6
N_MOD = 6

kernel_name = "yoco_mlstm_swa_sink_hybrid"

F32 = jnp.float32


def layer_norm(x, g, b):
    xf = x.astype(F32)
    mu = jnp.mean(xf, axis=-1, keepdims=True)
    var = jnp.mean(jnp.square(xf - mu), axis=-1, keepdims=True)
    y = (xf - mu) * lax.rsqrt(var + LN_EPS)
    return (y * g.astype(F32) + b.astype(F32)).astype(x.dtype)


def modulate(x, shift, scale):
    return x * (1 + scale) + shift


def rope_partial(x, pos):
    half = ROPE_DIM // 2
    inv_freq = ROPE_THETA ** (-jnp.arange(half, dtype=F32) / half)
    ang = pos.astype(F32)[..., None] * inv_freq
    cos = jnp.cos(ang)[:, :, None, :]
    sin = jnp.sin(ang)[:, :, None, :]
    xr = x[..., :ROPE_DIM].astype(F32)
    x1, x2 = xr[..., :half], xr[..., half:]
    rot = jnp.concatenate([x1 * cos - x2 * sin, x2 * cos + x1 * sin], axis=-1).astype(x.dtype)
    return jnp.concatenate([rot, x[..., ROPE_DIM:]], axis=-1)


def mlstm_mixer(u, w_in, b_gates, norm_w, w_out):
    B, S, _ = u.shape
    H, L, dk, dv = M_HEADS, M_CHUNK, M_QK_DIM, M_V_DIM
    nq = H * dk
    proj = u @ w_in
    q, k, v, o, gi, gf = jnp.split(
        proj, [nq, 2 * nq, 2 * nq + D_MODEL, 2 * nq + 2 * D_MODEL, 2 * nq + 2 * D_MODEL + H], axis=-1)
    gates = (jnp.concatenate([gi, gf], axis=-1) + b_gates).astype(F32)
    gates = GATE_CAP * jnp.tanh(gates / GATE_CAP)
    log_i = gates[..., :H]
    log_f = jax.nn.log_sigmoid(gates[..., H:])
    nc = S // L

    def to_chunks(t, d):
        return t.astype(F32).reshape(B, nc, L, H, d).transpose(1, 0, 3, 2, 4)

    def gate_chunks(t):
        return t.reshape(B, nc, L, H).transpose(1, 0, 3, 2)

    qc = to_chunks(q, dk)
    kc = to_chunks(k, dk) / math.sqrt(dk)
    vc = to_chunks(v, dv)
    lic, lfc = gate_chunks(log_i), gate_chunks(log_f)
    causal = jnp.tril(jnp.ones((L, L), dtype=bool))

    def step(carry, inp):
        C, n, m = carry
        qb, kb, vb, li, lf = inp
        bcum = jnp.cumsum(lf, axis=-1)
        dmat = bcum[..., :, None] - bcum[..., None, :] + li[..., None, :]
        dmat = jnp.where(causal, dmat, -jnp.inf)
        inter = bcum + m[..., None]
        m_t = jnp.maximum(inter, jnp.max(dmat, axis=-1))
        w_intra = jnp.exp(dmat - m_t[..., None])
        w_inter = jnp.exp(inter - m_t)
        s_qk = jnp.einsum('bhtd,bhsd->bhts', qb, kb) * w_intra
        num = (w_inter[..., None] * jnp.einsum('bhvd,bhtd->bhtv', C, qb)
               + jnp.einsum('bhts,bhsv->bhtv', s_qk, vb))
        den = w_inter * jnp.einsum('bhd,bhtd->bht', n, qb) + jnp.sum(s_qk, axis=-1)
        h = num / jnp.maximum(jnp.abs(den), jnp.exp(-m_t))[..., None]
        b_last = bcum[..., -1]
        d_end = b_last[..., None] - bcum + li
        m_new = jnp.maximum(b_last + m, jnp.max(d_end, axis=-1))
        w_state = jnp.exp(d_end - m_new[..., None])
        decay = jnp.exp(b_last + m - m_new)
        C_new = decay[..., None, None] * C + jnp.einsum('bhs,bhsv,bhsd->bhvd', w_state, vb, kb)
        n_new = decay[..., None] * n + jnp.einsum('bhs,bhsd->bhd', w_state, kb)
        return (C_new, n_new, m_new), h

    init = (jnp.zeros((B, H, dv, dk), F32), jnp.zeros((B, H, dk), F32), jnp.zeros((B, H), F32))
    _, hs = lax.scan(step, init, (qc, kc, vc, lic, lfc))
    hs = hs * lax.rsqrt(jnp.mean(jnp.square(hs), axis=-1, keepdims=True) + RMS_EPS)
    hs = hs.transpose(1, 0, 3, 2, 4).reshape(B, S, H * dv)
    hs = hs * norm_w.astype(F32) * jax.nn.sigmoid(o.astype(F32))
    return hs.astype(u.dtype) @ w_out


def swa_sink_attention(q, k, v, sinks):
    B, S = q.shape[:2]
    L = WINDOW
    nb = S // L
    qb = q.reshape(B, nb, L, A_KV_HEADS, A_GROUP, A_HEAD_DIM)

    def band(t):
        tb = t.reshape(B, nb, L, A_KV_HEADS, A_HEAD_DIM)
        prev = jnp.pad(tb[:, :-1], ((0, 0), (1, 0), (0, 0), (0, 0), (0, 0)))
        return jnp.concatenate([prev, tb], axis=2)

    kb, vb = band(k), band(v)
    scores = jnp.einsum('bnqkgd,bnskd->bnkgqs', qb, kb).astype(F32) / math.sqrt(A_HEAD_DIM)
    qi = jnp.arange(L)[:, None]
    sj = jnp.arange(2 * L)[None, :]
    in_window = (sj > qi) & (sj <= qi + L)
    blk = jnp.arange(nb)[:, None, None]
    mask = in_window[None] & ((blk > 0) | (sj[None] >= L))
    scores = jnp.where(mask[None, :, None, None], scores, -jnp.inf)
    sink = sinks.astype(F32).reshape(A_KV_HEADS, A_GROUP)[None, None, :, :, None, None]
    mx = jnp.maximum(jnp.max(scores, axis=-1, keepdims=True), sink)
    e = jnp.exp(scores - mx)
    probs = e / (jnp.sum(e, axis=-1, keepdims=True) + jnp.exp(sink - mx))
    out = jnp.einsum('bnkgqs,bnskd->bnqkgd', probs.astype(v.dtype), vb)
    return out.reshape(B, S, A_HEADS * A_HEAD_DIM)


def setup_inputs(seed: int = 0) -> dict:
    key = jax.random.key(seed)
    ks = jax.random.split(key, 24)
    nrm = jax.random.normal
    D = D_MODEL
    kv_dim = A_KV_HEADS * A_HEAD_DIM
    x = nrm(ks[0], (BATCH, SEQ, D), F32)
    c = nrm(ks[1], (BATCH, D), F32)
    offsets = jax.random.randint(ks[2], (BATCH, 1), 0, 1024, dtype=jnp.int32)
    positions = offsets + jnp.arange(SEQ, dtype=jnp.int32)[None, :]
    ada_w = nrm(ks[3], (DEPTH, D, N_MOD * D), F32) * (0.1 * D ** -0.5)
    ada_b = nrm(ks[4], (DEPTH, N_MOD * D), F32) * 0.01
    kv_ada_w = nrm(ks[5], (D, 2 * D), F32) * (0.1 * D ** -0.5)
    kv_ada_b = nrm(ks[6], (2 * D,), F32) * 0.01
    a_w_in = nrm(ks[7], (N_A_LAYERS, D, M_IN_DIM), F32) * D ** -0.5
    b_i = nrm(ks[8], (N_A_LAYERS, M_HEADS), F32) * 0.01
    b_f = jnp.linspace(3.0, 6.0, M_HEADS, dtype=F32)[None, :] + nrm(ks[9], (N_A_LAYERS, M_HEADS), F32) * 0.01
    a_b_gates = jnp.concatenate([b_i, b_f], axis=-1)
    a_norm_w = 1.0 + 0.02 * nrm(ks[10], (N_A_LAYERS, D), F32)
    a_w_out = nrm(ks[11], (N_A_LAYERS, D, D), F32) * (D ** -0.5 * DEEPNORM_BETA)
    w_k = nrm(ks[12], (D, kv_dim), F32) * D ** -0.5
    w_v = nrm(ks[13], (D, kv_dim), F32) * (D ** -0.5 * DEEPNORM_BETA)
    w_kv = jnp.concatenate([w_k, w_v], axis=-1)
    b_w_q = nrm(ks[14], (N_B_LAYERS, D, A_HEADS * A_HEAD_DIM), F32) * D ** -0.5
    b_sinks = nrm(ks[15], (N_B_LAYERS, A_HEADS), F32) * 0.5
    b_w_o = nrm(ks[16], (N_B_LAYERS, A_HEADS * A_HEAD_DIM, D), F32) * ((A_HEADS * A_HEAD_DIM) ** -0.5 * DEEPNORM_BETA)
    mlp_w_up = nrm(ks[17], (DEPTH, D, D_FF), F32) * D ** -0.5
    mlp_w_down = nrm(ks[18], (DEPTH, D_FF, D), F32) * (D_FF ** -0.5 * DEEPNORM_BETA)
    ln_g = 1.0 + 0.02 * nrm(ks[19], (2 * DEPTH, D), F32)
    ln_b = 0.02 * nrm(ks[20], (2 * DEPTH, D), F32)
    return {"x": x, "c": c, "positions": positions, "ada_w": ada_w, "ada_b": ada_b,
            "kv_ada_w": kv_ada_w, "kv_ada_b": kv_ada_b, "a_w_in": a_w_in, "a_b_gates": a_b_gates,
            "a_norm_w": a_norm_w, "a_w_out": a_w_out, "w_kv": w_kv, "b_w_q": b_w_q,
            "b_sinks": b_sinks, "b_w_o": b_w_o, "mlp_w_up": mlp_w_up, "mlp_w_down": mlp_w_down,
            "ln_g": ln_g, "ln_b": ln_b}


def reference(x, c, positions, ada_w, ada_b, kv_ada_w, kv_ada_b, a_w_in, a_b_gates, a_norm_w,
              a_w_out, w_kv, b_w_q, b_sinks, b_w_o, mlp_w_up, mlp_w_down, ln_g, ln_b):
    B, S, _ = x.shape
    kv_dim = A_KV_HEADS * A_HEAD_DIM
    cs = jax.nn.silu(c)
    h = x
    k_sh = v_sh = None
    for layer in range(DEPTH):
        if layer == N_A_LAYERS:
            kv_mod = (cs @ kv_ada_w + kv_ada_b)[:, None, :]
            kv_shift, kv_scale = jnp.split(kv_mod, 2, axis=-1)
            kv = modulate(h, kv_shift, kv_scale) @ w_kv
            k_sh = rope_partial(kv[..., :kv_dim].reshape(B, S, A_KV_HEADS, A_HEAD_DIM), positions)
            v_sh = kv[..., kv_dim:].reshape(B, S, A_KV_HEADS, A_HEAD_DIM)
        mod = (cs @ ada_w[layer] + ada_b[layer])[:, None, :]
        sh_m, sc_m, g_m, sh_f, sc_f, g_f = jnp.split(mod, N_MOD, axis=-1)
        u = modulate(h, sh_m, sc_m)
        if layer < N_A_LAYERS:
            y = mlstm_mixer(u, a_w_in[layer], a_b_gates[layer], a_norm_w[layer], a_w_out[layer])
        else:
            j = layer - N_A_LAYERS
            q = rope_partial((u @ b_w_q[j]).reshape(B, S, A_HEADS, A_HEAD_DIM), positions)
            y = swa_sink_attention(q, k_sh, v_sh, b_sinks[j]) @ b_w_o[j]
        h = layer_norm(DEEPNORM_ALPHA * h + (1 + g_m) * y, ln_g[2 * layer], ln_b[2 * layer])
        u = modulate(h, sh_f, sc_f)
        y = jnp.square(jax.nn.relu(u @ mlp_w_up[layer])) @ mlp_w_down[layer]
        h = layer_norm(DEEPNORM_ALPHA * h + (1 + g_f) * y, ln_g[2 * layer + 1], ln_b[2 * layer + 1])
    return h
```

```python
import functools
import math

import numpy as np
import jax
import jax.numpy as jnp
from jax import lax
from jax.experimental import pallas as pl
from jax.experimental.pallas import tpu as pltpu

F32 = jnp.float32
BF16 = jnp.bfloat16

D_MODEL = 1024
DEPTH = 2
M_HEADS = 4
M_V_DIM = D_MODEL // M_HEADS
M_QK_DIM = M_V_DIM // 2
M_CHUNK = 128
GATE_CAP = 15.0
A_HEADS = 16
A_KV_HEADS = 4
A_GROUP = A_HEADS // A_KV_HEADS
A_HEAD_DIM = 64
WINDOW = 128
ROPE_DIM = A_HEAD_DIM // 4
ROPE_HALF = ROPE_DIM // 2
ROPE_THETA = 500000.0
D_FF = 4 * D_MODEL
DEEPNORM_ALPHA = (2 * DEPTH) ** 0.25
LN_EPS = 1e-5
RMS_EPS = 1e-6
N_MOD = 6
MOD_KV_BASE = DEPTH * N_MOD * D_MODEL
MOD_WIDTH = MOD_KV_BASE + 2 * D_MODEL

LANES_V7X = 128
VMEM_LIMIT_BYTES_V7X = 56 * 1024 * 1024

ROW_TILE = 512
ADALN_COL_TILE = 1024


def _compiler_params(n_axes):
    return pltpu.CompilerParams(
        dimension_semantics=("arbitrary",) * n_axes,
        vmem_limit_bytes=VMEM_LIMIT_BYTES_V7X,
    )


def _const_spec(shape):
    return pl.BlockSpec(shape, lambda *_: (0,) * len(shape))


def _layer_norm(r, g, b):
    mu = jnp.mean(r, axis=-1, keepdims=True)
    d = r - mu
    var = jnp.mean(d * d, axis=-1, keepdims=True)
    return d * lax.rsqrt(var + LN_EPS) * g + b


def _mod_slice(mod_ref, idx):
    return mod_ref[0, :, idx * D_MODEL:(idx + 1) * D_MODEL]


def _dot(a, b):
    return jnp.dot(a, b, preferred_element_type=F32)


N_ADA_TILES = DEPTH * N_MOD * D_MODEL // ADALN_COL_TILE
N_KV_TILES = 2 * D_MODEL // ADALN_COL_TILE


def _adaln_kernel(c_ref, wa_ref, wk_ref, ba_ref, bk_ref, o_ref):
    i = pl.program_id(0)
    c = c_ref[...]
    cs = (c * jax.nn.sigmoid(c)).astype(BF16)

    @pl.when(i < N_ADA_TILES)
    def _():
        o_ref[...] = _dot(cs, wa_ref[0].astype(BF16)) + ba_ref[0]

    @pl.when(i >= N_ADA_TILES)
    def _():
        o_ref[...] = _dot(cs, wk_ref[...].astype(BF16)) + bk_ref[0]


def _adaln(c, ada_w, ada_b, kv_ada_w, kv_ada_b):
    batch = c.shape[0]
    tiles_per_layer = N_MOD * D_MODEL // ADALN_COL_TILE
    ba = ada_b.reshape(N_ADA_TILES, 1, ADALN_COL_TILE)
    bk = kv_ada_b.reshape(N_KV_TILES, 1, ADALN_COL_TILE)

    def ada_idx(i):
        return jnp.minimum(i, N_ADA_TILES - 1)

    def kv_idx(i):
        return jnp.maximum(i - N_ADA_TILES, 0)

    return pl.pallas_call(
        _adaln_kernel,
        grid=(N_ADA_TILES + N_KV_TILES,),
        in_specs=[
            _const_spec((batch, D_MODEL)),
            pl.BlockSpec((1, D_MODEL, ADALN_COL_TILE),
                         lambda i: (ada_idx(i) // tiles_per_layer, 0, ada_idx(i) % tiles_per_layer)),
            pl.BlockSpec((D_MODEL, ADALN_COL_TILE), lambda i: (0, kv_idx(i))),
            pl.BlockSpec((1, 1, ADALN_COL_TILE), lambda i: (ada_idx(i), 0, 0)),
            pl.BlockSpec((1, 1, ADALN_COL_TILE), lambda i: (kv_idx(i), 0, 0)),
        ],
        out_specs=pl.BlockSpec((batch, ADALN_COL_TILE), lambda i: (0, i)),
        out_shape=jax.ShapeDtypeStruct((batch, MOD_WIDTH), F32),
        compiler_params=_compiler_params(1),
        name="adaln",
    )(c, ada_w, kv_ada_w, ba, bk)


NQ = M_HEADS * M_QK_DIM
GATE_PAD = LANES_V7X


def _lane_cumsum(x):
    lane = lax.broadcasted_iota(jnp.int32, x.shape, 1)
    shift = 1
    while shift < x.shape[1]:
        x = x + jnp.where(lane >= shift, pltpu.roll(x, shift, axis=1), 0.0)
        shift *= 2
    return x


def _mlstm_chunk(r0, qk_s, v_s, g_s, h_s, c_s, n_s, m_s, bg_ref):
    L = M_CHUNK
    rows = pl.ds(r0, L)
    gates_t = g_s[rows, :].T[0:2 * M_HEADS, :] + bg_ref[...]
    capped = GATE_CAP * jnp.tanh(gates_t / GATE_CAP)
    log_f = jnp.minimum(capped, 0.0) - jnp.log1p(jnp.exp(-jnp.abs(capped)))
    bcum = _lane_cumsum(log_f)
    bcum_t = jnp.concatenate([bcum, jnp.zeros((L - 2 * M_HEADS, L), F32)], axis=0).T

    t_idx = lax.broadcasted_iota(jnp.int32, (L, L), 0)
    s_idx = lax.broadcasted_iota(jnp.int32, (L, L), 1)
    causal = s_idx <= t_idx

    for h in range(M_HEADS):
        b_col = bcum_t[:, M_HEADS + h:M_HEADS + h + 1]
        b_row = bcum[M_HEADS + h:M_HEADS + h + 1, :]
        a_row = capped[h:h + 1, :] - b_row
        m_old = m_s[h:h + 1, 0:1]

        dmat = jnp.where(causal, b_col + a_row, -jnp.inf)
        inter = b_col + m_old
        m_t = jnp.maximum(inter, jnp.max(dmat, axis=1, keepdims=True))
        w_intra = jnp.exp(dmat - m_t)
        w_inter = jnp.exp(inter - m_t)

        q_h = qk_s[rows, h * M_QK_DIM:(h + 1) * M_QK_DIM]
        k_t = qk_s[rows, NQ + h * M_QK_DIM:NQ + (h + 1) * M_QK_DIM].astype(F32).T
        v_h = v_s[rows, h * M_V_DIM:(h + 1) * M_V_DIM]
        c_old = c_s[h]
        n_old = n_s[h]

        s_ext = _dot(q_h, jnp.concatenate([k_t.astype(BF16), n_old.astype(BF16)], axis=1))
        s_qk = s_ext[:, 0:L] * w_intra
        qn = s_ext[:, L:L + 1]
        den = w_inter * qn + jnp.sum(s_qk, axis=1, keepdims=True)
        lhs = jnp.concatenate([(q_h.astype(F32) * w_inter).astype(BF16), s_qk.astype(BF16)], axis=1)
        rhs = jnp.concatenate([c_old.astype(BF16), v_h], axis=0)
        num = _dot(lhs, rhs)
        h_s[rows, h * M_V_DIM:(h + 1) * M_V_DIM] = num / jnp.maximum(jnp.abs(den), jnp.exp(-m_t))

        b_last = b_row[:, L - 1:L]
        d_end = b_last + a_row
        m_new = jnp.maximum(b_last + m_old, jnp.max(d_end, axis=1, keepdims=True))
        w_state = jnp.exp(d_end - m_new)
        decay = jnp.exp(b_last + m_old - m_new)
        kw = k_t * w_state
        c_s[h] = decay * c_old + _dot(kw.astype(BF16), v_h)
        n_s[h] = decay * n_old + jnp.sum(kw, axis=1, keepdims=True)
        m_s[h:h + 1, :] = jnp.broadcast_to(m_new, (1, LANES_V7X))


def _mix0_kernel(x_ref, mod_ref, win_ref, wg_ref, bg_ref, nw_ref, wout_ref, lng_ref, lnb_ref,
                 o_ref, qk_s, v_s, og_s, g_s, h_s, c_s, n_s, m_s):
    tm = x_ref.shape[1]

    @pl.when(pl.program_id(1) == 0)
    def _():
        c_s[...] = jnp.zeros_like(c_s)
        n_s[...] = jnp.zeros_like(n_s)
        m_s[...] = jnp.zeros_like(m_s)

    x = x_ref[0]
    u = (x * (1.0 + _mod_slice(mod_ref, 1)) + _mod_slice(mod_ref, 0)).astype(BF16)
    qk_s[:, 0:NQ] = _dot(u, win_ref[:, 0:NQ]).astype(BF16)
    qk_s[:, NQ:2 * NQ] = (_dot(u, win_ref[:, NQ:2 * NQ]) * (1.0 / math.sqrt(M_QK_DIM))).astype(BF16)
    v_s[...] = _dot(u, win_ref[:, 2 * NQ:2 * NQ + D_MODEL]).astype(BF16)
    og_s[...] = _dot(u, win_ref[:, 2 * NQ + D_MODEL:2 * NQ + 2 * D_MODEL])
    g_s[...] = _dot(u, wg_ref[...])

    def chunk_body(c, carry):
        _mlstm_chunk(pl.multiple_of(c * M_CHUNK, M_CHUNK), qk_s, v_s, g_s, h_s, c_s, n_s, m_s, bg_ref)
        return carry

    lax.fori_loop(0, tm // M_CHUNK, chunk_body, 0)

    normed = []
    for h in range(M_HEADS):
        hh = h_s[:, h * M_V_DIM:(h + 1) * M_V_DIM]
        normed.append(hh * lax.rsqrt(jnp.mean(hh * hh, axis=-1, keepdims=True) + RMS_EPS))
    hn = jnp.concatenate(normed, axis=1)
    gated = hn * nw_ref[...] * jax.nn.sigmoid(og_s[...])
    y = _dot(gated.astype(BF16), wout_ref[...])
    r = DEEPNORM_ALPHA * x + (1.0 + _mod_slice(mod_ref, 2)) * y
    o_ref[0] = _layer_norm(r, lng_ref[...], lnb_ref[...])


def _mix0(x, mod, w_in, w_gate, b_gates, norm_w, w_out, ln_g, ln_b):
    B, S, D = x.shape
    tm = ROW_TILE
    n_main = 2 * NQ + 2 * D_MODEL
    return pl.pallas_call(
        _mix0_kernel,
        grid=(B, S // tm),
        in_specs=[
            pl.BlockSpec((1, tm, D), lambda b, t: (b, t, 0)),
            pl.BlockSpec((1, 1, MOD_WIDTH), lambda b, t: (b, 0, 0)),
            _const_spec((D, n_main)),
            _const_spec((D, GATE_PAD)),
            _const_spec((2 * M_HEADS, 1)),
            _const_spec((1, D)),
            _const_spec((D, D)),
            _const_spec((1, D)),
            _const_spec((1, D)),
        ],
        out_specs=pl.BlockSpec((1, tm, D), lambda b, t: (b, t, 0)),
        out_shape=jax.ShapeDtypeStruct((B, S, D), F32),
        scratch_shapes=[
            pltpu.VMEM((tm, 2 * NQ), BF16),
            pltpu.VMEM((tm, D_MODEL), BF16),
            pltpu.VMEM((tm, D_MODEL), F32),
            pltpu.VMEM((tm, GATE_PAD), F32),
            pltpu.VMEM((tm, D_MODEL), F32),
            pltpu.VMEM((M_HEADS, M_QK_DIM, M_V_DIM), F32),
            pltpu.VMEM((M_HEADS, M_QK_DIM, LANES_V7X), F32),
            pltpu.VMEM((2 * M_HEADS, LANES_V7X), F32),
        ],
        compiler_params=_compiler_params(2),
        name="mix0",
    )(x, mod, w_in, w_gate, b_gates, norm_w, w_out, ln_g, ln_b)


def _mlp_core(h, mod_ref, layer, wup_ref, wdn_ref, lng_ref, lnb_ref):
    base = layer * N_MOD
    u = (h * (1.0 + _mod_slice(mod_ref, base + 4)) + _mod_slice(mod_ref, base + 3)).astype(BF16)
    a = jnp.maximum(_dot(u, wup_ref[...]), 0.0)
    y = _dot((a * a).astype(BF16), wdn_ref[...])
    r = DEEPNORM_ALPHA * h + (1.0 + _mod_slice(mod_ref, base + 5)) * y
    return _layer_norm(r, lng_ref[...], lnb_ref[...])


def _split3_bf16(x):
    hi = x.astype(BF16)
    r1 = x - hi.astype(F32)
    mid = r1.astype(BF16)
    lo = (r1 - mid.astype(F32)).astype(BF16)
    return hi, mid, lo


def _rope_tables(pos_row, invf_ref, expand_ref):
    ang = pos_row.astype(F32) * invf_ref[...]
    trig = jnp.concatenate([jnp.cos(ang), jnp.sin(ang)], axis=0)
    parts = jnp.concatenate(_split3_bf16(trig), axis=0)
    tab = lax.dot_general(parts, expand_ref[...], (((0,), (0,)), ((), ())),
                          preferred_element_type=F32)
    lane = lax.broadcasted_iota(jnp.int32, (1, LANES_V7X), 1)
    cos_t = tab[:, 0:LANES_V7X] + jnp.where(lane % A_HEAD_DIM >= ROPE_DIM, 1.0, 0.0)
    return cos_t, tab[:, LANES_V7X:2 * LANES_V7X], tab[:, 2 * LANES_V7X:3 * LANES_V7X]


def _rope(x, tables):
    cos_t, sin_up, sin_dn = tables
    out = []
    for g in range(x.shape[1] // LANES_V7X):
        xg = x[:, g * LANES_V7X:(g + 1) * LANES_V7X]
        from_lo = pltpu.roll(xg, ROPE_HALF, axis=1)
        from_hi = pltpu.roll(xg, LANES_V7X - ROPE_HALF, axis=1)
        out.append(xg * cos_t + from_lo * sin_up + from_hi * sin_dn)
    return jnp.concatenate(out, axis=1)


def _mlp0_kernel(h_ref, mod_ref, pos_ref, wup_ref, wdn_ref, lng_ref, lnb_ref, wkv_ref, wq_ref,
                 invf_ref, expand_ref, o_ref, q_ref, k_ref, v_ref):
    h1 = _mlp_core(h_ref[0], mod_ref, 0, wup_ref, wdn_ref, lng_ref, lnb_ref)
    o_ref[0] = h1
    kv_dim = A_KV_HEADS * A_HEAD_DIM
    kv_shift = mod_ref[0, :, MOD_KV_BASE:MOD_KV_BASE + D_MODEL]
    kv_scale = mod_ref[0, :, MOD_KV_BASE + D_MODEL:MOD_KV_BASE + 2 * D_MODEL]
    kv = _dot((h1 * (1.0 + kv_scale) + kv_shift).astype(BF16), wkv_ref[...])
    uq = (h1 * (1.0 + _mod_slice(mod_ref, N_MOD + 1)) + _mod_slice(mod_ref, N_MOD)).astype(BF16)
    q = _dot(uq, wq_ref[...])
    tables = _rope_tables(pos_ref[0], invf_ref, expand_ref)
    q_ref[0] = (_rope(q, tables) * (1.0 / math.sqrt(A_HEAD_DIM))).astype(BF16)
    k_ref[0] = _rope(kv[:, 0:kv_dim], tables).astype(BF16)
    v_ref[0] = kv[:, kv_dim:2 * kv_dim].astype(BF16)


def _mlp1_kernel(h_ref, mod_ref, wup_ref, wdn_ref, lng_ref, lnb_ref, o_ref):
    o_ref[0] = _mlp_core(h_ref[0], mod_ref, 1, wup_ref, wdn_ref, lng_ref, lnb_ref)


def _row_spec(tm, width):
    return pl.BlockSpec((1, tm, width), lambda b, t: (b, t, 0))


def _mlp0(h, mod, pos, w_up, w_dn, ln_g, ln_b, w_kv, w_q, inv_freq, expand):
    B, S, D = h.shape
    tm = ROW_TILE
    kv_dim = A_KV_HEADS * A_HEAD_DIM
    return pl.pallas_call(
        _mlp0_kernel,
        grid=(B, S // tm),
        in_specs=[
            _row_spec(tm, D),
            pl.BlockSpec((1, 1, MOD_WIDTH), lambda b, t: (b, 0, 0)),
            pl.BlockSpec((1, 1, tm), lambda b, t: (b, 0, t)),
            _const_spec((D, D_FF)),
            _const_spec((D_FF, D)),
            _const_spec((1, D)),
            _const_spec((1, D)),
            _const_spec((D, 2 * kv_dim)),
            _const_spec((D, A_HEADS * A_HEAD_DIM)),
            _const_spec((ROPE_HALF, 1)),
            _const_spec(expand.shape),
        ],
        out_specs=[_row_spec(tm, D), _row_spec(tm, A_HEADS * A_HEAD_DIM),
                   _row_spec(tm, kv_dim), _row_spec(tm, kv_dim)],
        out_shape=[
            jax.ShapeDtypeStruct((B, S, D), F32),
            jax.ShapeDtypeStruct((B, S, A_HEADS * A_HEAD_DIM), BF16),
            jax.ShapeDtypeStruct((B, S, kv_dim), BF16),
            jax.ShapeDtypeStruct((B, S, kv_dim), BF16),
        ],
        compiler_params=_compiler_params(2),
        name="mlp0",
    )(h, mod, pos, w_up, w_dn, ln_g, ln_b, w_kv, w_q, inv_freq, expand)


def _mlp1(h, mod, w_up, w_dn, ln_g, ln_b):
    B, S, D = h.shape
    tm = ROW_TILE
    return pl.pallas_call(
        _mlp1_kernel,
        grid=(B, S // tm),
        in_specs=[
            _row_spec(tm, D),
            pl.BlockSpec((1, 1, MOD_WIDTH), lambda b, t: (b, 0, 0)),
            _const_spec((D, D_FF)),
            _const_spec((D_FF, D)),
            _const_spec((1, D)),
            _const_spec((1, D)),
        ],
        out_specs=_row_spec(tm, D),
        out_shape=jax.ShapeDtypeStruct((B, S, D), F32),
        compiler_params=_compiler_params(2),
        name="mlp1",
    )(h, mod, w_up, w_dn, ln_g, ln_b)


HEAD_ORDER = tuple(8 * (G // 4) + 4 * half + (G % 4) for G in range(A_HEADS // 2) for half in range(2))
KV_PAIRS = A_KV_HEADS // 2


def _attn_block(q_blk, k_band, v_band, mask, sink_ref):
    L = WINDOW
    lane = lax.broadcasted_iota(jnp.int32, (2 * L, LANES_V7X), 1)
    lo_half = lane < A_HEAD_DIM
    zero = jnp.zeros((), BF16)
    outs = [None] * (A_HEADS // 2)
    for p in range(KV_PAIRS):
        kp = k_band[:, p * LANES_V7X:(p + 1) * LANES_V7X]
        vp = v_band[:, p * LANES_V7X:(p + 1) * LANES_V7X]
        k_bd = jnp.concatenate([jnp.where(lo_half, kp, zero), jnp.where(lo_half, zero, kp)], axis=0)
        v_bd = jnp.concatenate([jnp.where(lo_half, vp, zero), jnp.where(lo_half, zero, vp)], axis=0)
        lhs = jnp.concatenate(
            [q_blk[:, (4 * p + j) * LANES_V7X:(4 * p + j + 1) * LANES_V7X] for j in range(A_GROUP)], axis=0)
        s_all = lax.dot_general(lhs, k_bd, (((1,), (1,)), ((), ())), preferred_element_type=F32)
        prob_rows = []
        for j in range(A_GROUP):
            halves = []
            for half in range(2):
                s = s_all[j * L:(j + 1) * L, half * 2 * L:(half + 1) * 2 * L]
                s = jnp.where(mask, s, -jnp.inf)
                sink = sink_ref[2 * (4 * p + j) + half]
                mx = jnp.maximum(jnp.max(s, axis=1, keepdims=True), sink)
                e = jnp.exp(s - mx)
                den = jnp.sum(e, axis=1, keepdims=True) + jnp.exp(sink - mx)
                halves.append((e / den).astype(BF16))
            prob_rows.append(jnp.concatenate(halves, axis=1))
        probs = jnp.concatenate(prob_rows, axis=0)
        o_all = _dot(probs, v_bd)
        for j in range(A_GROUP):
            outs[4 * p + j] = o_all[j * L:(j + 1) * L, :].astype(BF16)
    return jnp.concatenate(outs, axis=1)


def _attn_kernel(sink_ref, q_ref, k_ref, kprev_ref, v_ref, vprev_ref, h_ref, mod_ref, wo_ref,
                 lng_ref, lnb_ref, o_ref, att_s):
    L = WINDOW
    tq = q_ref.shape[1]
    qi = lax.broadcasted_iota(jnp.int32, (L, 2 * L), 0)
    sj = lax.broadcasted_iota(jnp.int32, (L, 2 * L), 1)
    in_window = (sj > qi) & (sj <= qi + L)
    first_mask = in_window & ((sj >= L) | (pl.program_id(1) > 0))
    for blk in range(tq // L):
        r0 = blk * L
        if blk == 0:
            k_prev, v_prev, mask = kprev_ref[0], vprev_ref[0], first_mask
        else:
            k_prev, v_prev, mask = k_ref[0, r0 - L:r0, :], v_ref[0, r0 - L:r0, :], in_window
        k_band = jnp.concatenate([k_prev, k_ref[0, r0:r0 + L, :]], axis=0)
        v_band = jnp.concatenate([v_prev, v_ref[0, r0:r0 + L, :]], axis=0)
        att_s[r0:r0 + L, :] = _attn_block(q_ref[0, r0:r0 + L, :], k_band, v_band, mask, sink_ref)
    y = _dot(att_s[...], wo_ref[...])
    r = DEEPNORM_ALPHA * h_ref[0] + (1.0 + _mod_slice(mod_ref, N_MOD + 2)) * y
    o_ref[0] = _layer_norm(r, lng_ref[...], lnb_ref[...])


def _attn(sinks, q, k, v, h, mod, w_o, ln_g, ln_b):
    B, S, D = h.shape
    tq = ROW_TILE
    kv_dim = A_KV_HEADS * A_HEAD_DIM
    blocks_per_tile = tq // WINDOW

    def prev_map(b, t):
        return (b, jnp.maximum(t * blocks_per_tile - 1, 0), 0)

    return pl.pallas_call(
        _attn_kernel,
        grid=(B, S // tq),
        in_specs=[
            pl.BlockSpec(memory_space=pltpu.SMEM),
            _row_spec(tq, A_HEADS * A_HEAD_DIM),
            _row_spec(tq, kv_dim),
            pl.BlockSpec((1, WINDOW, kv_dim), prev_map),
            _row_spec(tq, kv_dim),
            pl.BlockSpec((1, WINDOW, kv_dim), prev_map),
            _row_spec(tq, D),
            pl.BlockSpec((1, 1, MOD_WIDTH), lambda b, t: (b, 0, 0)),
            _const_spec((A_HEADS * A_HEAD_DIM, D)),
            _const_spec((1, D)),
            _const_spec((1, D)),
        ],
        out_specs=_row_spec(tq, D),
        out_shape=jax.ShapeDtypeStruct((B, S, D), F32),
        scratch_shapes=[pltpu.VMEM((tq, A_HEADS * A_HEAD_DIM), BF16)],
        compiler_params=_compiler_params(2),
        name="attn",
    )(sinks, q, k, k, v, v, h, mod, w_o, ln_g, ln_b)


def _rope_expand_matrix():
    e = np.zeros((2 * ROPE_HALF, 3 * LANES_V7X), np.float32)
    for lane in range(LANES_V7X):
        d = lane % A_HEAD_DIM
        if d < ROPE_HALF:
            e[d, lane] = 1.0
            e[ROPE_HALF + d, 2 * LANES_V7X + lane] = -1.0
        elif d < ROPE_DIM:
            e[d - ROPE_HALF, lane] = 1.0
            e[d, LANES_V7X + lane] = 1.0
    return np.tile(e, (3, 1))


def kernel(x, c, positions, ada_w, ada_b, kv_ada_w, kv_ada_b, a_w_in, a_b_gates, a_norm_w, a_w_out,
           w_kv, b_w_q, b_sinks, b_w_o, mlp_w_up, mlp_w_down, ln_g, ln_b):
    B, S, D = x.shape
    assert D == D_MODEL and S % ROW_TILE == 0 and ROW_TILE % M_CHUNK == 0

    mod = _adaln(c, ada_w, ada_b, kv_ada_w, kv_ada_b).reshape(B, 1, MOD_WIDTH)

    n_main = 2 * NQ + 2 * D_MODEL
    w_in = a_w_in[0, :, :n_main].astype(BF16)
    w_gate = jnp.pad(a_w_in[0, :, n_main:], ((0, 0), (0, GATE_PAD - 2 * M_HEADS))).astype(BF16)
    h = _mix0(x, mod, w_in, w_gate, a_b_gates[0].reshape(2 * M_HEADS, 1), a_norm_w[0].reshape(1, D),
              a_w_out[0].astype(BF16), ln_g[0].reshape(1, D), ln_b[0].reshape(1, D))

    head_cols = np.concatenate([np.arange(hd * A_HEAD_DIM, (hd + 1) * A_HEAD_DIM) for hd in HEAD_ORDER])
    inv_freq = (ROPE_THETA ** (-jnp.arange(ROPE_HALF, dtype=F32) / ROPE_HALF)).reshape(ROPE_HALF, 1)
    expand = jnp.asarray(_rope_expand_matrix(), BF16)
    h, q, k, v = _mlp0(h, mod, positions.reshape(B, 1, S), mlp_w_up[0].astype(BF16),
                       mlp_w_down[0].astype(BF16), ln_g[1].reshape(1, D), ln_b[1].reshape(1, D),
                       w_kv.astype(BF16), b_w_q[0][:, head_cols].astype(BF16), inv_freq, expand)

    h = _attn(b_sinks[0][np.asarray(HEAD_ORDER)], q, k, v, h, mod, b_w_o[0][head_cols, :].astype(BF16),
              ln_g[2].reshape(1, D), ln_b[2].reshape(1, D))

    return _mlp1(h, mod, mlp_w_up[1].astype(BF16), mlp_w_down[1].astype(BF16),
                 ln_g[3].reshape(1, D), ln_b[3].reshape(1, D))
```

```python
import math

import numpy as np
import jax
import jax.numpy as jnp
from jax import lax
from jax.experimental import pallas as pl
from jax.experimental.pallas import tpu as pltpu

F32 = jnp.float32
BF16 = jnp.bfloat16

D_MODEL = 1024
DEPTH = 2
M_HEADS = 4
M_V_DIM = D_MODEL // M_HEADS
M_QK_DIM = M_V_DIM // 2
M_CHUNK = 128
GATE_CAP = 15.0
A_HEADS = 16
A_KV_HEADS = 4
A_GROUP = A_HEADS // A_KV_HEADS
A_HEAD_DIM = 64
WINDOW = 128
ROPE_DIM = A_HEAD_DIM // 4
ROPE_HALF = ROPE_DIM // 2
ROPE_THETA = 500000.0
D_FF = 4 * D_MODEL
DEEPNORM_ALPHA = (2 * DEPTH) ** 0.25
LN_EPS = 1e-5
RMS_EPS = 1e-6
N_MOD = 6
MOD_KV_BASE = DEPTH * N_MOD * D_MODEL
MOD_WIDTH = MOD_KV_BASE + 2 * D_MODEL

LANES_V7X = 128
SUBLANES_V7X = 8
VMEM_LIMIT_BYTES_V7X = 56 * 1024 * 1024

ROW_TILE = 512
ADALN_COL_TILE = 1024


def _compiler_params(n_axes):
    return pltpu.CompilerParams(
        dimension_semantics=("arbitrary",) * n_axes,
        vmem_limit_bytes=VMEM_LIMIT_BYTES_V7X,
    )


def _const_spec(shape):
    return pl.BlockSpec(shape, lambda *_: (0,) * len(shape))


def _layer_norm(r, g, b):
    mu = jnp.mean(r, axis=-1, keepdims=True)
    d = r - mu
    var = jnp.mean(d * d, axis=-1, keepdims=True)
    return d * lax.rsqrt(var + LN_EPS) * g + b


def _mod_slice(mod_ref, idx):
    return mod_ref[0, :, idx * D_MODEL:(idx + 1) * D_MODEL]


def _dot(a, b):
    return jnp.dot(a, b, preferred_element_type=F32)


def _dot_nt(a, b):
    return lax.dot_general(a, b, (((1,), (1,)), ((), ())), preferred_element_type=F32)


def _dot_tn(a, b):
    return lax.dot_general(a, b, (((0,), (0,)), ((), ())), preferred_element_type=F32)


def _split3_bf16(x):
    hi = x.astype(BF16)
    r1 = x - hi.astype(F32)
    mid = r1.astype(BF16)
    lo = (r1 - mid.astype(F32)).astype(BF16)
    return hi, mid, lo


N_ADA_TILES = DEPTH * N_MOD * D_MODEL // ADALN_COL_TILE
N_KV_TILES = 2 * D_MODEL // ADALN_COL_TILE


def _adaln_kernel(c_ref, wa_ref, wk_ref, ba_ref, bk_ref, o_ref):
    i = pl.program_id(0)
    c = c_ref[...]
    cs = (c * jax.nn.sigmoid(c)).astype(BF16)

    @pl.when(i < N_ADA_TILES)
    def _():
        o_ref[...] = _dot(cs, wa_ref[0].astype(BF16)) + ba_ref[0]

    @pl.when(i >= N_ADA_TILES)
    def _():
        o_ref[...] = _dot(cs, wk_ref[...].astype(BF16)) + bk_ref[0]


def _adaln(c, ada_w, ada_b, kv_ada_w, kv_ada_b):
    batch = c.shape[0]
    tiles_per_layer = N_MOD * D_MODEL // ADALN_COL_TILE
    ba = ada_b.reshape(N_ADA_TILES, 1, ADALN_COL_TILE)
    bk = kv_ada_b.reshape(N_KV_TILES, 1, ADALN_COL_TILE)

    def ada_idx(i):
        return jnp.minimum(i, N_ADA_TILES - 1)

    def kv_idx(i):
        return jnp.maximum(i - N_ADA_TILES, 0)

    return pl.pallas_call(
        _adaln_kernel,
        grid=(N_ADA_TILES + N_KV_TILES,),
        in_specs=[
            _const_spec((batch, D_MODEL)),
            pl.BlockSpec((1, D_MODEL, ADALN_COL_TILE),
                         lambda i: (ada_idx(i) // tiles_per_layer, 0, ada_idx(i) % tiles_per_layer)),
            pl.BlockSpec((D_MODEL, ADALN_COL_TILE), lambda i: (0, kv_idx(i))),
            pl.BlockSpec((1, 1, ADALN_COL_TILE), lambda i: (ada_idx(i), 0, 0)),
            pl.BlockSpec((1, 1, ADALN_COL_TILE), lambda i: (kv_idx(i), 0, 0)),
        ],
        out_specs=pl.BlockSpec((batch, ADALN_COL_TILE), lambda i: (0, i)),
        out_shape=jax.ShapeDtypeStruct((batch, MOD_WIDTH), F32),
        compiler_params=_compiler_params(1),
        name="adaln",
    )(c, ada_w, kv_ada_w, ba, bk)


NQ = M_HEADS * M_QK_DIM
GATE_PAD = LANES_V7X
GATE_QUANTS = 3
GATE_PART_ROWS = 32
GATE_EXPAND_COLS = GATE_QUANTS * M_HEADS * LANES_V7X


def _gate_expand_matrix():
    e = np.zeros((GATE_PART_ROWS, GATE_EXPAND_COLS), np.float32)
    for quant in range(GATE_QUANTS):
        for h in range(M_HEADS):
            row = quant * SUBLANES_V7X + M_HEADS + h
            grp = quant * M_HEADS + h
            e[row, grp * LANES_V7X:(grp + 1) * LANES_V7X] = 1.0
    return np.tile(e, (3, 1))


def _segment_scan(x, op, identity):
    pos = lax.broadcasted_iota(jnp.int32, x.shape, 1) % M_CHUNK
    shift = 1
    while shift < M_CHUNK:
        x = op(x, jnp.where(pos >= shift, pltpu.roll(x, shift, axis=1), identity))
        shift *= 2
    return x


def _gate_prep(g_tm, bg_ref, gexp_ref):
    L = M_CHUNK
    nc = g_tm.shape[0] // L
    gates_t = jnp.concatenate(
        [g_tm[c * L:(c + 1) * L, :].T[0:2 * M_HEADS, :] for c in range(nc)], axis=1) + bg_ref[...]
    capped = GATE_CAP * jnp.tanh(gates_t / GATE_CAP)
    log_f = jnp.minimum(capped, 0.0) - jnp.log1p(jnp.exp(-jnp.abs(capped)))
    bcum = _segment_scan(log_f, jnp.add, 0.0)
    a = pltpu.roll(capped, M_HEADS, axis=0) - bcum
    cmax = _segment_scan(a, jnp.maximum, -jnp.inf)
    stacked = jnp.concatenate([bcum, cmax, a, jnp.zeros_like(a)], axis=0)
    parts = jnp.concatenate(_split3_bf16(stacked), axis=0)
    cols = [_dot_tn(parts[:, c * L:(c + 1) * L], gexp_ref[...]) for c in range(nc)]
    return a, cols


def _mlstm_tile(a_rows, cols, qk_s, v_s, h_s, c_s, n_s, m_s):
    L = M_CHUNK
    W = LANES_V7X
    t_idx = lax.broadcasted_iota(jnp.int32, (L, L), 0)
    s_idx = lax.broadcasted_iota(jnp.int32, (L, L), 1)
    causal = s_idx <= t_idx
    ones = jnp.ones((L, W), BF16)

    for h in range(M_HEADS):
        c_t = c_s[h]
        n_bc = n_s[h]
        m_old = m_s[h:h + 1, :]
        for c in range(len(cols)):
            rows = slice(c * L, (c + 1) * L)
            b_bc = cols[c][:, h * W:(h + 1) * W]
            cm_bc = cols[c][:, (M_HEADS + h) * W:(M_HEADS + h + 1) * W]
            a_bc = cols[c][:, (2 * M_HEADS + h) * W:(2 * M_HEADS + h + 1) * W]
            a_row = a_rows[M_HEADS + h:M_HEADS + h + 1, rows]

            mt = jnp.maximum(cm_bc, m_old)
            w_intra = jnp.exp(jnp.where(causal, a_row - mt, -jnp.inf))
            w_inter = jnp.exp(m_old - mt)
            e_neg = jnp.exp(-b_bc - mt)

            q_h = qk_s[rows, h * M_QK_DIM:(h + 1) * M_QK_DIM]
            k_h = qk_s[rows, NQ + h * M_QK_DIM:NQ + (h + 1) * M_QK_DIM]
            v1 = jnp.concatenate([v_s[rows, h * M_V_DIM:(h + 1) * M_V_DIM], ones], axis=1)

            s_qk = _dot_nt(q_h, k_h) * w_intra
            lhs = jnp.concatenate([(q_h.astype(F32) * w_inter).astype(BF16), s_qk.astype(BF16)], axis=1)
            state = jnp.concatenate([c_t.astype(BF16), n_bc.astype(BF16)], axis=1)
            out = _dot(lhs, jnp.concatenate([state, v1], axis=0))
            r_den = 1.0 / jnp.maximum(jnp.abs(out[:, M_V_DIM:M_V_DIM + W]), e_neg)
            for half in range(M_V_DIM // W):
                h_s[rows, h * M_V_DIM + half * W:h * M_V_DIM + (half + 1) * W] = (
                    out[:, half * W:(half + 1) * W] * r_den)

            mx = jnp.maximum(m_old, cm_bc[L - 1:L, :])
            decay = jnp.exp(m_old - mx)
            kw = (k_h.astype(F32) * jnp.exp(a_bc - mx)).astype(BF16)
            delta = _dot_tn(kw, v1)
            c_t = jnp.concatenate([decay] * (M_V_DIM // W), axis=1) * c_t + delta[:, 0:M_V_DIM]
            n_bc = decay * n_bc + delta[:, M_V_DIM:M_V_DIM + W]
            m_old = b_bc[L - 1:L, :] + mx
        c_s[h] = c_t
        n_s[h] = n_bc
        m_s[h:h + 1, :] = m_old


def _mix0_kernel(x_ref, mod_ref, win_ref, wg_ref, bg_ref, gexp_ref, nw_ref, wout_ref, lng_ref, lnb_ref,
                 o_ref, qk_s, v_s, og_s, h_s, c_s, n_s, m_s):
    @pl.when(pl.program_id(1) == 0)
    def _():
        c_s[...] = jnp.zeros_like(c_s)
        n_s[...] = jnp.zeros_like(n_s)
        m_s[...] = jnp.zeros_like(m_s)

    x = x_ref[0]
    u = (x * (1.0 + _mod_slice(mod_ref, 1)) + _mod_slice(mod_ref, 0)).astype(BF16)
    qk_s[:, 0:NQ] = _dot(u, win_ref[:, 0:NQ]).astype(BF16)
    qk_s[:, NQ:2 * NQ] = (_dot(u, win_ref[:, NQ:2 * NQ]) * (1.0 / math.sqrt(M_QK_DIM))).astype(BF16)
    v_s[...] = _dot(u, win_ref[:, 2 * NQ:2 * NQ + D_MODEL]).astype(BF16)
    og_s[...] = _dot(u, win_ref[:, 2 * NQ + D_MODEL:2 * NQ + 2 * D_MODEL])

    a_rows, cols = _gate_prep(_dot(u, wg_ref[...]), bg_ref, gexp_ref)
    _mlstm_tile(a_rows, cols, qk_s, v_s, h_s, c_s, n_s, m_s)

    normed = []
    for h in range(M_HEADS):
        hh = h_s[:, h * M_V_DIM:(h + 1) * M_V_DIM]
        normed.append(hh * lax.rsqrt(jnp.mean(hh * hh, axis=-1, keepdims=True) + RMS_EPS))
    hn = jnp.concatenate(normed, axis=1)
    gated = hn * nw_ref[...] * jax.nn.sigmoid(og_s[...])
    y = _dot(gated.astype(BF16), wout_ref[...])
    r = DEEPNORM_ALPHA * x + (1.0 + _mod_slice(mod_ref, 2)) * y
    o_ref[0] = _layer_norm(r, lng_ref[...], lnb_ref[...])


def _mix0(x, mod, w_in, w_gate, b_gates, gate_expand, norm_w, w_out, ln_g, ln_b):
    B, S, D = x.shape
    tm = ROW_TILE
    n_main = 2 * NQ + 2 * D_MODEL
    return pl.pallas_call(
        _mix0_kernel,
        grid=(B, S // tm),
        in_specs=[
            pl.BlockSpec((1, tm, D), lambda b, t: (b, t, 0)),
            pl.BlockSpec((1, 1, MOD_WIDTH), lambda b, t: (b, 0, 0)),
            _const_spec((D, n_main)),
            _const_spec((D, GATE_PAD)),
            _const_spec((2 * M_HEADS, 1)),
            _const_spec(gate_expand.shape),
            _const_spec((1, D)),
            _const_spec((D, D)),
            _const_spec((1, D)),
            _const_spec((1, D)),
        ],
        out_specs=pl.BlockSpec((1, tm, D), lambda b, t: (b, t, 0)),
        out_shape=jax.ShapeDtypeStruct((B, S, D), F32),
        scratch_shapes=[
            pltpu.VMEM((tm, 2 * NQ), BF16),
            pltpu.VMEM((tm, D_MODEL), BF16),
            pltpu.VMEM((tm, D_MODEL), F32),
            pltpu.VMEM((tm, D_MODEL), F32),
            pltpu.VMEM((M_HEADS, M_QK_DIM, M_V_DIM), F32),
            pltpu.VMEM((M_HEADS, M_QK_DIM, LANES_V7X), F32),
            pltpu.VMEM((SUBLANES_V7X, LANES_V7X), F32),
        ],
        compiler_params=_compiler_params(2),
        name="mix0",
    )(x, mod, w_in, w_gate, b_gates, gate_expand, norm_w, w_out, ln_g, ln_b)


def _mlp_core(h, mod_ref, layer, wup_ref, wdn_ref, lng_ref, lnb_ref):
    base = layer * N_MOD
    u = (h * (1.0 + _mod_slice(mod_ref, base + 4)) + _mod_slice(mod_ref, base + 3)).astype(BF16)
    a = jnp.maximum(_dot(u, wup_ref[...]), 0.0)
    y = _dot((a * a).astype(BF16), wdn_ref[...])
    r = DEEPNORM_ALPHA * h + (1.0 + _mod_slice(mod_ref, base + 5)) * y
    return _layer_norm(r, lng_ref[...], lnb_ref[...])


def _rope_tables(pos_row, invf_ref, expand_ref):
    ang = pos_row.astype(F32) * invf_ref[...]
    trig = jnp.concatenate([jnp.cos(ang), jnp.sin(ang)], axis=0)
    parts = jnp.concatenate(_split3_bf16(trig), axis=0)
    tab = _dot_tn(parts, expand_ref[...])
    lane = lax.broadcasted_iota(jnp.int32, (1, LANES_V7X), 1)
    cos_t = tab[:, 0:LANES_V7X] + jnp.where(lane % A_HEAD_DIM >= ROPE_DIM, 1.0, 0.0)
    return cos_t, tab[:, LANES_V7X:2 * LANES_V7X], tab[:, 2 * LANES_V7X:3 * LANES_V7X]


def _rope(x, tables):
    cos_t, sin_up, sin_dn = tables
    out = []
    for g in range(x.shape[1] // LANES_V7X):
        xg = x[:, g * LANES_V7X:(g + 1) * LANES_V7X]
        from_lo = pltpu.roll(xg, ROPE_HALF, axis=1)
        from_hi = pltpu.roll(xg, LANES_V7X - ROPE_HALF, axis=1)
        out.append(xg * cos_t + from_lo * sin_up + from_hi * sin_dn)
    return jnp.concatenate(out, axis=1)


def _mlp0_kernel(h_ref, mod_ref, pos_ref, wup_ref, wdn_ref, lng_ref, lnb_ref, wkv_ref, wq_ref,
                 invf_ref, expand_ref, o_ref, q_ref, k_ref, v_ref):
    h1 = _mlp_core(h_ref[0], mod_ref, 0, wup_ref, wdn_ref, lng_ref, lnb_ref)
    o_ref[0] = h1
    kv_dim = A_KV_HEADS * A_HEAD_DIM
    kv_shift = mod_ref[0, :, MOD_KV_BASE:MOD_KV_BASE + D_MODEL]
    kv_scale = mod_ref[0, :, MOD_KV_BASE + D_MODEL:MOD_KV_BASE + 2 * D_MODEL]
    kv = _dot((h1 * (1.0 + kv_scale) + kv_shift).astype(BF16), wkv_ref[...])
    uq = (h1 * (1.0 + _mod_slice(mod_ref, N_MOD + 1)) + _mod_slice(mod_ref, N_MOD)).astype(BF16)
    q = _dot(uq, wq_ref[...])
    tables = _rope_tables(pos_ref[0], invf_ref, expand_ref)
    q_ref[0] = (_rope(q, tables) * (1.0 / math.sqrt(A_HEAD_DIM))).astype(BF16)
    k_ref[0] = _rope(kv[:, 0:kv_dim], tables).astype(BF16)
    v_ref[0] = kv[:, kv_dim:2 * kv_dim].astype(BF16)


def _mlp1_kernel(h_ref, mod_ref, wup_ref, wdn_ref, lng_ref, lnb_ref, o_ref):
    o_ref[0] = _mlp_core(h_ref[0], mod_ref, 1, wup_ref, wdn_ref, lng_ref, lnb_ref)


def _row_spec(tm, width):
    return pl.BlockSpec((1, tm, width), lambda b, t: (b, t, 0))


def _mlp0(h, mod, pos, w_up, w_dn, ln_g, ln_b, w_kv, w_q, inv_freq, expand):
    B, S, D = h.shape
    tm = ROW_TILE
    kv_dim = A_KV_HEADS * A_HEAD_DIM
    return pl.pallas_call(
        _mlp0_kernel,
        grid=(B, S // tm),
        in_specs=[
            _row_spec(tm, D),
            pl.BlockSpec((1, 1, MOD_WIDTH), lambda b, t: (b, 0, 0)),
            pl.BlockSpec((1, 1, tm), lambda b, t: (b, 0, t)),
            _const_spec((D, D_FF)),
            _const_spec((D_FF, D)),
            _const_spec((1, D)),
            _const_spec((1, D)),
            _const_spec((D, 2 * kv_dim)),
            _const_spec((D, A_HEADS * A_HEAD_DIM)),
            _const_spec((ROPE_HALF, 1)),
            _const_spec(expand.shape),
        ],
        out_specs=[_row_spec(tm, D), _row_spec(tm, A_HEADS * A_HEAD_DIM),
                   _row_spec(tm, kv_dim), _row_spec(tm, kv_dim)],
        out_shape=[
            jax.ShapeDtypeStruct((B, S, D), F32),
            jax.ShapeDtypeStruct((B, S, A_HEADS * A_HEAD_DIM), BF16),
            jax.ShapeDtypeStruct((B, S, kv_dim), BF16),
            jax.ShapeDtypeStruct((B, S, kv_dim), BF16),
        ],
        compiler_params=_compiler_params(2),
        name="mlp0",
    )(h, mod, pos, w_up, w_dn, ln_g, ln_b, w_kv, w_q, inv_freq, expand)


def _mlp1(h, mod, w_up, w_dn, ln_g, ln_b):
    B, S, D = h.shape
    tm = ROW_TILE
    return pl.pallas_call(
        _mlp1_kernel,
        grid=(B, S // tm),
        in_specs=[
            _row_spec(tm, D),
            pl.BlockSpec((1, 1, MOD_WIDTH), lambda b, t: (b, 0, 0)),
            _const_spec((D, D_FF)),
            _const_spec((D_FF, D)),
            _const_spec((1, D)),
            _const_spec((1, D)),
        ],
        out_specs=_row_spec(tm, D),
        out_shape=jax.ShapeDtypeStruct((B, S, D), F32),
        compiler_params=_compiler_params(2),
        name="mlp1",
    )(h, mod, w_up, w_dn, ln_g, ln_b)


HEAD_ORDER = tuple(8 * (G // 4) + 4 * half + (G % 4) for G in range(A_HEADS // 2) for half in range(2))
KV_PAIRS = A_KV_HEADS // 2


def _attn_block(q_blk, k_band, v_band, mask, sink_ref):
    L = WINDOW
    lane = lax.broadcasted_iota(jnp.int32, (2 * L, LANES_V7X), 1)
    lo_half = lane < A_HEAD_DIM
    zero = jnp.zeros((), BF16)
    outs = [None] * (A_HEADS // 2)
    for p in range(KV_PAIRS):
        kp = k_band[:, p * LANES_V7X:(p + 1) * LANES_V7X]
        vp = v_band[:, p * LANES_V7X:(p + 1) * LANES_V7X]
        k_bd = jnp.concatenate([jnp.where(lo_half, kp, zero), jnp.where(lo_half, zero, kp)], axis=0)
        v_bd = jnp.concatenate([jnp.where(lo_half, vp, zero), jnp.where(lo_half, zero, vp)], axis=0)
        lhs = jnp.concatenate(
            [q_blk[:, (4 * p + j) * LANES_V7X:(4 * p + j + 1) * LANES_V7X] for j in range(A_GROUP)], axis=0)
        s_all = _dot_nt(lhs, k_bd)
        prob_rows = []
        for j in range(A_GROUP):
            halves = []
            for half in range(2):
                s = s_all[j * L:(j + 1) * L, half * 2 * L:(half + 1) * 2 * L]
                s = jnp.where(mask, s, -jnp.inf)
                sink = sink_ref[2 * (4 * p + j) + half]
                mx = jnp.maximum(jnp.max(s, axis=1, keepdims=True), sink)
                e = jnp.exp(s - mx)
                den = jnp.sum(e, axis=1, keepdims=True) + jnp.exp(sink - mx)
                halves.append((e / den).astype(BF16))
            prob_rows.append(jnp.concatenate(halves, axis=1))
        probs = jnp.concatenate(prob_rows, axis=0)
        o_all = _dot(probs, v_bd)
        for j in range(A_GROUP):
            outs[4 * p + j] = o_all[j * L:(j + 1) * L, :].astype(BF16)
    return jnp.concatenate(outs, axis=1)


def _attn_kernel(sink_ref, q_ref, k_ref, kprev_ref, v_ref, vprev_ref, h_ref, mod_ref, wo_ref,
                 lng_ref, lnb_ref, o_ref, att_s):
    L = WINDOW
    tq = q_ref.shape[1]
    qi = lax.broadcasted_iota(jnp.int32, (L, 2 * L), 0)
    sj = lax.broadcasted_iota(jnp.int32, (L, 2 * L), 1)
    in_window = (sj > qi) & (sj <= qi + L)
    first_mask = in_window & ((sj >= L) | (pl.program_id(1) > 0))
    for blk in range(tq // L):
        r0 = blk * L
        if blk == 0:
            k_prev, v_prev, mask = kprev_ref[0], vprev_ref[0], first_mask
        else:
            k_prev, v_prev, mask = k_ref[0, r0 - L:r0, :], v_ref[0, r0 - L:r0, :], in_window
        k_band = jnp.concatenate([k_prev, k_ref[0, r0:r0 + L, :]], axis=0)
        v_band = jnp.concatenate([v_prev, v_ref[0, r0:r0 + L, :]], axis=0)
        att_s[r0:r0 + L, :] = _attn_block(q_ref[0, r0:r0 + L, :], k_band, v_band, mask, sink_ref)
    y = _dot(att_s[...], wo_ref[...])
    r = DEEPNORM_ALPHA * h_ref[0] + (1.0 + _mod_slice(mod_ref, N_MOD + 2)) * y
    o_ref[0] = _layer_norm(r, lng_ref[...], lnb_ref[...])


def _attn(sinks, q, k, v, h, mod, w_o, ln_g, ln_b):
    B, S, D = h.shape
    tq = ROW_TILE
    kv_dim = A_KV_HEADS * A_HEAD_DIM
    blocks_per_tile = tq // WINDOW

    def prev_map(b, t):
        return (b, jnp.maximum(t * blocks_per_tile - 1, 0), 0)

    return pl.pallas_call(
        _attn_kernel,
        grid=(B, S // tq),
        in_specs=[
            pl.BlockSpec(memory_space=pltpu.SMEM),
            _row_spec(tq, A_HEADS * A_HEAD_DIM),
            _row_spec(tq, kv_dim),
            pl.BlockSpec((1, WINDOW, kv_dim), prev_map),
            _row_spec(tq, kv_dim),
            pl.BlockSpec((1, WINDOW, kv_dim), prev_map),
            _row_spec(tq, D),
            pl.BlockSpec((1, 1, MOD_WIDTH), lambda b, t: (b, 0, 0)),
            _const_spec((A_HEADS * A_HEAD_DIM, D)),
            _const_spec((1, D)),
            _const_spec((1, D)),
        ],
        out_specs=_row_spec(tq, D),
        out_shape=jax.ShapeDtypeStruct((B, S, D), F32),
        scratch_shapes=[pltpu.VMEM((tq, A_HEADS * A_HEAD_DIM), BF16)],
        compiler_params=_compiler_params(2),
        name="attn",
    )(sinks, q, k, k, v, v, h, mod, w_o, ln_g, ln_b)


def _rope_expand_matrix():
    e = np.zeros((2 * ROPE_HALF, 3 * LANES_V7X), np.float32)
    for lane in range(LANES_V7X):
        d = lane % A_HEAD_DIM
        if d < ROPE_HALF:
            e[d, lane] = 1.0
            e[ROPE_HALF + d, 2 * LANES_V7X + lane] = -1.0
        elif d < ROPE_DIM:
            e[d - ROPE_HALF, lane] = 1.0
            e[d, LANES_V7X + lane] = 1.0
    return np.tile(e, (3, 1))


def kernel(x, c, positions, ada_w, ada_b, kv_ada_w, kv_ada_b, a_w_in, a_b_gates, a_norm_w, a_w_out,
           w_kv, b_w_q, b_sinks, b_w_o, mlp_w_up, mlp_w_down, ln_g, ln_b):
    B, S, D = x.shape
    assert D == D_MODEL and S % ROW_TILE == 0 and ROW_TILE % M_CHUNK == 0

    mod = _adaln(c, ada_w, ada_b, kv_ada_w, kv_ada_b).reshape(B, 1, MOD_WIDTH)

    n_main = 2 * NQ + 2 * D_MODEL
    w_in = a_w_in[0, :, :n_main].astype(BF16)
    w_gate = jnp.pad(a_w_in[0, :, n_main:], ((0, 0), (0, GATE_PAD - 2 * M_HEADS))).astype(BF16)
    h = _mix0(x, mod, w_in, w_gate, a_b_gates[0].reshape(2 * M_HEADS, 1),
              jnp.asarray(_gate_expand_matrix(), BF16), a_norm_w[0].reshape(1, D),
              a_w_out[0].astype(BF16), ln_g[0].reshape(1, D), ln_b[0].reshape(1, D))

    head_cols = np.concatenate([np.arange(hd * A_HEAD_DIM, (hd + 1) * A_HEAD_DIM) for hd in HEAD_ORDER])
    inv_freq = (ROPE_THETA ** (-jnp.arange(ROPE_HALF, dtype=F32) / ROPE_HALF)).reshape(ROPE_HALF, 1)
    expand = jnp.asarray(_rope_expand_matrix(), BF16)
    h, q, k, v = _mlp0(h, mod, positions.reshape(B, 1, S), mlp_w_up[0].astype(BF16),
                       mlp_w_down[0].astype(BF16), ln_g[1].reshape(1, D), ln_b[1].reshape(1, D),
                       w_kv.astype(BF16), b_w_q[0][:, head_cols].astype(BF16), inv_freq, expand)

    h = _attn(b_sinks[0][np.asarray(HEAD_ORDER)], q, k, v, h, mod, b_w_o[0][head_cols, :].astype(BF16),
              ln_g[2].reshape(1, D), ln_b[2].reshape(1, D))

    return _mlp1(h, mod, mlp_w_up[1].astype(BF16), mlp_w_down[1].astype(BF16),
                 ln_g[3].reshape(1, D), ln_b[3].reshape(1, D))
```

```python
import math

import numpy as np
import jax
import jax.numpy as jnp
from jax import lax
from jax.experimental import pallas as pl
from jax.experimental.pallas import tpu as pltpu

F32 = jnp.float32
BF16 = jnp.bfloat16

D_MODEL = 1024
DEPTH = 2
M_HEADS = 4
M_V_DIM = D_MODEL // M_HEADS
M_QK_DIM = M_V_DIM // 2
M_CHUNK = 128
GATE_CAP = 15.0
A_HEADS = 16
A_KV_HEADS = 4
A_GROUP = A_HEADS // A_KV_HEADS
A_HEAD_DIM = 64
WINDOW = 128
ROPE_DIM = A_HEAD_DIM // 4
ROPE_HALF = ROPE_DIM // 2
ROPE_THETA = 500000.0
D_FF = 4 * D_MODEL
DEEPNORM_ALPHA = (2 * DEPTH) ** 0.25
LN_EPS = 1e-5
RMS_EPS = 1e-6
LOG2_E = math.log2(math.e)
N_MOD = 6
MOD_KV_BASE = DEPTH * N_MOD * D_MODEL
MOD_WIDTH = MOD_KV_BASE + 2 * D_MODEL

LANES_V7X = 128
SUBLANES_V7X = 8
VMEM_LIMIT_BYTES_V7X = 56 * 1024 * 1024

ROW_TILE = 512
MLP_SUB_TILES = 2
ADALN_COL_TILE = 1024


def _compiler_params(n_axes):
    return pltpu.CompilerParams(
        dimension_semantics=("arbitrary",) * n_axes,
        vmem_limit_bytes=VMEM_LIMIT_BYTES_V7X,
    )


def _const_spec(shape):
    return pl.BlockSpec(shape, lambda *_: (0,) * len(shape))


def _layer_norm(r, g, b):
    mu = jnp.mean(r, axis=-1, keepdims=True)
    d = r - mu
    var = jnp.mean(d * d, axis=-1, keepdims=True)
    return d * lax.rsqrt(var + LN_EPS) * g + b


def _mod_slice(mod_ref, idx):
    return mod_ref[0, :, idx * D_MODEL:(idx + 1) * D_MODEL]


def _dot(a, b):
    return jnp.dot(a, b, preferred_element_type=F32)


def _dot_nt(a, b):
    return lax.dot_general(a, b, (((1,), (1,)), ((), ())), preferred_element_type=F32)


def _dot_tn(a, b):
    return lax.dot_general(a, b, (((0,), (0,)), ((), ())), preferred_element_type=F32)


def _split3_bf16(x):
    hi = x.astype(BF16)
    r1 = x - hi.astype(F32)
    mid = r1.astype(BF16)
    lo = (r1 - mid.astype(F32)).astype(BF16)
    return hi, mid, lo


N_ADA_TILES = DEPTH * N_MOD * D_MODEL // ADALN_COL_TILE
N_KV_TILES = 2 * D_MODEL // ADALN_COL_TILE


def _adaln_kernel(c_ref, wa_ref, wk_ref, ba_ref, bk_ref, o_ref):
    i = pl.program_id(0)
    c = c_ref[...]
    cs = (c * jax.nn.sigmoid(c)).astype(BF16)

    @pl.when(i < N_ADA_TILES)
    def _():
        o_ref[...] = _dot(cs, wa_ref[0].astype(BF16)) + ba_ref[0]

    @pl.when(i >= N_ADA_TILES)
    def _():
        o_ref[...] = _dot(cs, wk_ref[...].astype(BF16)) + bk_ref[0]


def _adaln(c, ada_w, ada_b, kv_ada_w, kv_ada_b):
    batch = c.shape[0]
    tiles_per_layer = N_MOD * D_MODEL // ADALN_COL_TILE
    ba = ada_b.reshape(N_ADA_TILES, 1, ADALN_COL_TILE)
    bk = kv_ada_b.reshape(N_KV_TILES, 1, ADALN_COL_TILE)

    def ada_idx(i):
        return jnp.minimum(i, N_ADA_TILES - 1)

    def kv_idx(i):
        return jnp.maximum(i - N_ADA_TILES, 0)

    return pl.pallas_call(
        _adaln_kernel,
        grid=(N_ADA_TILES + N_KV_TILES,),
        in_specs=[
            _const_spec((batch, D_MODEL)),
            pl.BlockSpec((1, D_MODEL, ADALN_COL_TILE),
                         lambda i: (ada_idx(i) // tiles_per_layer, 0, ada_idx(i) % tiles_per_layer)),
            pl.BlockSpec((D_MODEL, ADALN_COL_TILE), lambda i: (0, kv_idx(i))),
            pl.BlockSpec((1, 1, ADALN_COL_TILE), lambda i: (ada_idx(i), 0, 0)),
            pl.BlockSpec((1, 1, ADALN_COL_TILE), lambda i: (kv_idx(i), 0, 0)),
        ],
        out_specs=pl.BlockSpec((batch, ADALN_COL_TILE), lambda i: (0, i)),
        out_shape=jax.ShapeDtypeStruct((batch, MOD_WIDTH), F32),
        compiler_params=_compiler_params(1),
        name="adaln",
    )(c, ada_w, kv_ada_w, ba, bk)


NQ = M_HEADS * M_QK_DIM
GATE_PAD = LANES_V7X
GATE_QUANTS = 3
GATE_PART_ROWS = 32
GATE_EXPAND_COLS = GATE_QUANTS * M_HEADS * LANES_V7X


def _gate_expand_matrix():
    e = np.zeros((GATE_PART_ROWS, GATE_EXPAND_COLS), np.float32)
    for quant in range(GATE_QUANTS):
        for h in range(M_HEADS):
            row = quant * SUBLANES_V7X + M_HEADS + h
            grp = quant * M_HEADS + h
            e[row, grp * LANES_V7X:(grp + 1) * LANES_V7X] = 1.0
    return np.tile(e, (3, 1))


def _segment_scan(x, op, identity):
    pos = lax.broadcasted_iota(jnp.int32, x.shape, 1) % M_CHUNK
    shift = 1
    while shift < M_CHUNK:
        x = op(x, jnp.where(pos >= shift, pltpu.roll(x, shift, axis=1), identity))
        shift *= 2
    return x


def _gate_prep(g_tm, bg_ref, gexp_ref):
    L = M_CHUNK
    nc = g_tm.shape[0] // L
    gates_t = jnp.concatenate(
        [g_tm[c * L:(c + 1) * L, :].T[0:2 * M_HEADS, :] for c in range(nc)], axis=1) + bg_ref[...]
    capped = GATE_CAP * jnp.tanh(gates_t / GATE_CAP)
    log_f = jnp.minimum(capped, 0.0) - jnp.log1p(jnp.exp(-jnp.abs(capped)))
    bcum = _segment_scan(log_f, jnp.add, 0.0)
    a = pltpu.roll(capped, M_HEADS, axis=0) - bcum
    cmax = _segment_scan(a, jnp.maximum, -jnp.inf)
    stacked = jnp.concatenate([bcum, cmax, a, jnp.zeros_like(a)], axis=0)
    parts = jnp.concatenate(_split3_bf16(stacked), axis=0)
    cols = [_dot_tn(parts[:, c * L:(c + 1) * L], gexp_ref[...]) for c in range(nc)]
    return a, cols


def _mlstm_tile(a_rows, cols, qk_s, v_s, h_s, c_s, n_s, m_s):
    L = M_CHUNK
    W = LANES_V7X
    t_idx = lax.broadcasted_iota(jnp.int32, (L, L), 0)
    s_idx = lax.broadcasted_iota(jnp.int32, (L, L), 1)
    causal = s_idx <= t_idx
    ones = jnp.ones((L, W), BF16)

    for h in range(M_HEADS):
        c_t = c_s[h]
        n_bc = n_s[h]
        m_old = m_s[h:h + 1, :]
        for c in range(len(cols)):
            rows = slice(c * L, (c + 1) * L)
            b_bc = cols[c][:, h * W:(h + 1) * W]
            cm_bc = cols[c][:, (M_HEADS + h) * W:(M_HEADS + h + 1) * W]
            a_bc = cols[c][:, (2 * M_HEADS + h) * W:(2 * M_HEADS + h + 1) * W]
            a_row = a_rows[M_HEADS + h:M_HEADS + h + 1, rows]

            mt = jnp.maximum(cm_bc, m_old)
            w_intra = jnp.exp(jnp.where(causal, a_row - mt, -jnp.inf))
            w_inter = jnp.exp(m_old - mt)
            e_neg = jnp.exp(-b_bc - mt)

            q_h = qk_s[rows, h * M_QK_DIM:(h + 1) * M_QK_DIM]
            k_h = qk_s[rows, NQ + h * M_QK_DIM:NQ + (h + 1) * M_QK_DIM]
            v1 = jnp.concatenate([v_s[rows, h * M_V_DIM:(h + 1) * M_V_DIM], ones], axis=1)

            s_qk = _dot_nt(q_h, k_h) * w_intra
            lhs = jnp.concatenate([(q_h.astype(F32) * w_inter).astype(BF16), s_qk.astype(BF16)], axis=1)
            state = jnp.concatenate([c_t.astype(BF16), n_bc.astype(BF16)], axis=1)
            out = _dot(lhs, jnp.concatenate([state, v1], axis=0))
            r_den = 1.0 / jnp.maximum(jnp.abs(out[:, M_V_DIM:M_V_DIM + W]), e_neg)
            for half in range(M_V_DIM // W):
                h_s[rows, h * M_V_DIM + half * W:h * M_V_DIM + (half + 1) * W] = (
                    out[:, half * W:(half + 1) * W] * r_den)

            mx = jnp.maximum(m_old, cm_bc[L - 1:L, :])
            decay = jnp.exp(m_old - mx)
            kw = (k_h.astype(F32) * jnp.exp(a_bc - mx)).astype(BF16)
            delta = _dot_tn(kw, v1)
            c_t = jnp.concatenate([decay] * (M_V_DIM // W), axis=1) * c_t + delta[:, 0:M_V_DIM]
            n_bc = decay * n_bc + delta[:, M_V_DIM:M_V_DIM + W]
            m_old = b_bc[L - 1:L, :] + mx
        c_s[h] = c_t
        n_s[h] = n_bc
        m_s[h:h + 1, :] = m_old


def _mix0_kernel(x_ref, mod_ref, win_ref, wg_ref, bg_ref, gexp_ref, nw_ref, wout_ref, lng_ref, lnb_ref,
                 o_ref, qk_s, v_s, og_s, h_s, c_s, n_s, m_s):
    @pl.when(pl.program_id(1) == 0)
    def _():
        c_s[...] = jnp.zeros_like(c_s)
        n_s[...] = jnp.zeros_like(n_s)
        m_s[...] = jnp.zeros_like(m_s)

    x = x_ref[0]
    u = (x * (1.0 + _mod_slice(mod_ref, 1)) + _mod_slice(mod_ref, 0)).astype(BF16)
    qk_s[:, 0:NQ] = _dot(u, win_ref[:, 0:NQ]).astype(BF16)
    qk_s[:, NQ:2 * NQ] = (_dot(u, win_ref[:, NQ:2 * NQ]) * (1.0 / math.sqrt(M_QK_DIM))).astype(BF16)
    v_s[...] = _dot(u, win_ref[:, 2 * NQ:2 * NQ + D_MODEL]).astype(BF16)
    og_s[...] = _dot(u, win_ref[:, 2 * NQ + D_MODEL:2 * NQ + 2 * D_MODEL])

    a_rows, cols = _gate_prep(_dot(u, wg_ref[...]), bg_ref, gexp_ref)
    _mlstm_tile(a_rows, cols, qk_s, v_s, h_s, c_s, n_s, m_s)

    normed = []
    for h in range(M_HEADS):
        hh = h_s[:, h * M_V_DIM:(h + 1) * M_V_DIM]
        normed.append(hh * lax.rsqrt(jnp.mean(hh * hh, axis=-1, keepdims=True) + RMS_EPS))
    hn = jnp.concatenate(normed, axis=1)
    gated = hn * nw_ref[...] * jax.nn.sigmoid(og_s[...])
    y = _dot(gated.astype(BF16), wout_ref[...])
    r = DEEPNORM_ALPHA * x + (1.0 + _mod_slice(mod_ref, 2)) * y
    o_ref[0] = _layer_norm(r, lng_ref[...], lnb_ref[...])


def _mix0(x, mod, w_in, w_gate, b_gates, gate_expand, norm_w, w_out, ln_g, ln_b):
    B, S, D = x.shape
    tm = ROW_TILE
    n_main = 2 * NQ + 2 * D_MODEL
    return pl.pallas_call(
        _mix0_kernel,
        grid=(B, S // tm),
        in_specs=[
            pl.BlockSpec((1, tm, D), lambda b, t: (b, t, 0)),
            pl.BlockSpec((1, 1, MOD_WIDTH), lambda b, t: (b, 0, 0)),
            _const_spec((D, n_main)),
            _const_spec((D, GATE_PAD)),
            _const_spec((2 * M_HEADS, 1)),
            _const_spec(gate_expand.shape),
            _const_spec((1, D)),
            _const_spec((D, D)),
            _const_spec((1, D)),
            _const_spec((1, D)),
        ],
        out_specs=pl.BlockSpec((1, tm, D), lambda b, t: (b, t, 0)),
        out_shape=jax.ShapeDtypeStruct((B, S, D), F32),
        scratch_shapes=[
            pltpu.VMEM((tm, 2 * NQ), BF16),
            pltpu.VMEM((tm, D_MODEL), BF16),
            pltpu.VMEM((tm, D_MODEL), F32),
            pltpu.VMEM((tm, D_MODEL), F32),
            pltpu.VMEM((M_HEADS, M_QK_DIM, M_V_DIM), F32),
            pltpu.VMEM((M_HEADS, M_QK_DIM, LANES_V7X), F32),
            pltpu.VMEM((SUBLANES_V7X, LANES_V7X), F32),
        ],
        compiler_params=_compiler_params(2),
        name="mix0",
    )(x, mod, w_in, w_gate, b_gates, gate_expand, norm_w, w_out, ln_g, ln_b)


def _mlp_core(h, mod_ref, layer, wup_ref, wdn_ref, lng_ref, lnb_ref):
    base = layer * N_MOD
    u = (h * (1.0 + _mod_slice(mod_ref, base + 4)) + _mod_slice(mod_ref, base + 3)).astype(BF16)
    a = jnp.maximum(_dot(u, wup_ref[...]), 0.0)
    y = _dot((a * a).astype(BF16), wdn_ref[...])
    r = DEEPNORM_ALPHA * h + (1.0 + _mod_slice(mod_ref, base + 5)) * y
    return _layer_norm(r, lng_ref[...], lnb_ref[...])


def _rope_tables(pos_row, invf_ref, expand_ref):
    ang = pos_row.astype(F32) * invf_ref[...]
    trig = jnp.concatenate([jnp.cos(ang), jnp.sin(ang)], axis=0)
    parts = jnp.concatenate(_split3_bf16(trig), axis=0)
    tab = _dot_tn(parts, expand_ref[...])
    lane = lax.broadcasted_iota(jnp.int32, (1, LANES_V7X), 1)
    cos_t = tab[:, 0:LANES_V7X] + jnp.where(lane % A_HEAD_DIM >= ROPE_DIM, 1.0, 0.0)
    return cos_t, tab[:, LANES_V7X:2 * LANES_V7X], tab[:, 2 * LANES_V7X:3 * LANES_V7X]


def _rope(x, tables):
    cos_t, sin_up, sin_dn = tables
    out = []
    for g in range(x.shape[1] // LANES_V7X):
        xg = x[:, g * LANES_V7X:(g + 1) * LANES_V7X]
        from_lo = pltpu.roll(xg, ROPE_HALF, axis=1)
        from_hi = pltpu.roll(xg, LANES_V7X - ROPE_HALF, axis=1)
        out.append(xg * cos_t + from_lo * sin_up + from_hi * sin_dn)
    return jnp.concatenate(out, axis=1)


def _mlp0_kernel(h_ref, mod_ref, pos_ref, wup_ref, wdn_ref, lng_ref, lnb_ref, wkv_ref, wq_ref,
                 invf_ref, expand_ref, o_ref, q_ref, k_ref, v_ref):
    tm = h_ref.shape[1]
    kv_dim = A_KV_HEADS * A_HEAD_DIM
    kv_shift = mod_ref[0, :, MOD_KV_BASE:MOD_KV_BASE + D_MODEL]
    kv_scale = mod_ref[0, :, MOD_KV_BASE + D_MODEL:MOD_KV_BASE + 2 * D_MODEL]
    for s in range(MLP_SUB_TILES):
        rows = slice(s * tm // MLP_SUB_TILES, (s + 1) * tm // MLP_SUB_TILES)
        h1 = _mlp_core(h_ref[0, rows, :], mod_ref, 0, wup_ref, wdn_ref, lng_ref, lnb_ref)
        o_ref[0, rows, :] = h1
        kv = _dot((h1 * (1.0 + kv_scale) + kv_shift).astype(BF16), wkv_ref[...])
        uq = (h1 * (1.0 + _mod_slice(mod_ref, N_MOD + 1)) + _mod_slice(mod_ref, N_MOD)).astype(BF16)
        q = _dot(uq, wq_ref[...])
        tables = _rope_tables(pos_ref[0, :, rows], invf_ref, expand_ref)
        q_ref[0, rows, :] = (_rope(q, tables) * (LOG2_E / math.sqrt(A_HEAD_DIM))).astype(BF16)
        k_ref[0, rows, :] = _rope(kv[:, 0:kv_dim], tables).astype(BF16)
        v_ref[0, rows, :] = kv[:, kv_dim:2 * kv_dim].astype(BF16)


def _mlp1_kernel(h_ref, mod_ref, wup_ref, wdn_ref, lng_ref, lnb_ref, o_ref):
    tm = h_ref.shape[1]
    for s in range(MLP_SUB_TILES):
        rows = slice(s * tm // MLP_SUB_TILES, (s + 1) * tm // MLP_SUB_TILES)
        o_ref[0, rows, :] = _mlp_core(h_ref[0, rows, :], mod_ref, 1, wup_ref, wdn_ref, lng_ref, lnb_ref)


def _row_spec(tm, width):
    return pl.BlockSpec((1, tm, width), lambda b, t: (b, t, 0))


def _mlp0(h, mod, pos, w_up, w_dn, ln_g, ln_b, w_kv, w_q, inv_freq, expand):
    B, S, D = h.shape
    tm = ROW_TILE
    kv_dim = A_KV_HEADS * A_HEAD_DIM
    return pl.pallas_call(
        _mlp0_kernel,
        grid=(B, S // tm),
        in_specs=[
            _row_spec(tm, D),
            pl.BlockSpec((1, 1, MOD_WIDTH), lambda b, t: (b, 0, 0)),
            pl.BlockSpec((1, 1, tm), lambda b, t: (b, 0, t)),
            _const_spec((D, D_FF)),
            _const_spec((D_FF, D)),
            _const_spec((1, D)),
            _const_spec((1, D)),
            _const_spec((D, 2 * kv_dim)),
            _const_spec((D, A_HEADS * A_HEAD_DIM)),
            _const_spec((ROPE_HALF, 1)),
            _const_spec(expand.shape),
        ],
        out_specs=[_row_spec(tm, D), _row_spec(tm, A_HEADS * A_HEAD_DIM),
                   _row_spec(tm, kv_dim), _row_spec(tm, kv_dim)],
        out_shape=[
            jax.ShapeDtypeStruct((B, S, D), F32),
            jax.ShapeDtypeStruct((B, S, A_HEADS * A_HEAD_DIM), BF16),
            jax.ShapeDtypeStruct((B, S, kv_dim), BF16),
            jax.ShapeDtypeStruct((B, S, kv_dim), BF16),
        ],
        compiler_params=_compiler_params(2),
        name="mlp0",
    )(h, mod, pos, w_up, w_dn, ln_g, ln_b, w_kv, w_q, inv_freq, expand)


def _mlp1(h, mod, w_up, w_dn, ln_g, ln_b):
    B, S, D = h.shape
    tm = ROW_TILE
    return pl.pallas_call(
        _mlp1_kernel,
        grid=(B, S // tm),
        in_specs=[
            _row_spec(tm, D),
            pl.BlockSpec((1, 1, MOD_WIDTH), lambda b, t: (b, 0, 0)),
            _const_spec((D, D_FF)),
            _const_spec((D_FF, D)),
            _const_spec((1, D)),
            _const_spec((1, D)),
        ],
        out_specs=_row_spec(tm, D),
        out_shape=jax.ShapeDtypeStruct((B, S, D), F32),
        compiler_params=_compiler_params(2),
        name="mlp1",
    )(h, mod, w_up, w_dn, ln_g, ln_b)


HEAD_ORDER = tuple(8 * (G // 4) + 4 * half + (G % 4) for G in range(A_HEADS // 2) for half in range(2))
KV_PAIRS = A_KV_HEADS // 2


def _attn_block(q_blk, k_band, v_band, bias, sink_ref):
    L = WINDOW
    W = LANES_V7X
    lo_half = lax.broadcasted_iota(jnp.int32, (2 * L, W), 1) < A_HEAD_DIM
    lo_half_q = lax.broadcasted_iota(jnp.int32, (L, W), 1) < A_HEAD_DIM
    zero = jnp.zeros((), BF16)
    ones_bd = jnp.concatenate([jnp.where(lo_half, 1.0, 0.0), jnp.where(lo_half, 0.0, 1.0)], axis=0).astype(BF16)
    outs = [None] * (A_HEADS // 2)
    for p in range(KV_PAIRS):
        kp = k_band[:, p * W:(p + 1) * W]
        vp = v_band[:, p * W:(p + 1) * W]
        k_bd = jnp.concatenate([jnp.where(lo_half, kp, zero), jnp.where(lo_half, zero, kp)], axis=0)
        v_bd = jnp.concatenate([jnp.where(lo_half, vp, zero), jnp.where(lo_half, zero, vp)], axis=0)
        lhs = jnp.concatenate([q_blk[:, (4 * p + j) * W:(4 * p + j + 1) * W] for j in range(A_GROUP)], axis=0)
        s_all = _dot_nt(lhs, k_bd)
        e_rows, sink_rows = [], []
        for j in range(A_GROUP):
            halves, sink_terms = [], []
            for half in range(2):
                s = s_all[j * L:(j + 1) * L, half * 2 * L:(half + 1) * 2 * L] + bias
                sink = sink_ref[2 * (4 * p + j) + half] * LOG2_E
                mx = jnp.maximum(jnp.max(s, axis=1, keepdims=True), sink)
                halves.append(jnp.exp2(s - mx).astype(BF16))
                sink_terms.append(jnp.exp2(sink - mx))
            e_rows.append(jnp.concatenate(halves, axis=1))
            sink_rows.append(jnp.where(lo_half_q, sink_terms[0], sink_terms[1]))
        e_all = jnp.concatenate(e_rows, axis=0)
        o_ext = _dot(e_all, jnp.concatenate([v_bd, ones_bd], axis=1))
        o_all = o_ext[:, 0:W] / (o_ext[:, W:2 * W] + jnp.concatenate(sink_rows, axis=0))
        for j in range(A_GROUP):
            outs[4 * p + j] = o_all[j * L:(j + 1) * L, :].astype(BF16)
    return jnp.concatenate(outs, axis=1)


def _attn_kernel(sink_ref, q_ref, k_ref, kprev_ref, v_ref, vprev_ref, h_ref, mod_ref, wo_ref,
                 lng_ref, lnb_ref, o_ref, att_s):
    L = WINDOW
    tq = q_ref.shape[1]
    qi = lax.broadcasted_iota(jnp.int32, (L, 2 * L), 0)
    sj = lax.broadcasted_iota(jnp.int32, (L, 2 * L), 1)
    in_window = (sj > qi) & (sj <= qi + L)
    bias = jnp.where(in_window, 0.0, -jnp.inf)
    first_bias = jnp.where(in_window & ((sj >= L) | (pl.program_id(1) > 0)), 0.0, -jnp.inf)
    for blk in range(tq // L):
        r0 = blk * L
        if blk == 0:
            k_prev, v_prev, blk_bias = kprev_ref[0], vprev_ref[0], first_bias
        else:
            k_prev, v_prev, blk_bias = k_ref[0, r0 - L:r0, :], v_ref[0, r0 - L:r0, :], bias
        k_band = jnp.concatenate([k_prev, k_ref[0, r0:r0 + L, :]], axis=0)
        v_band = jnp.concatenate([v_prev, v_ref[0, r0:r0 + L, :]], axis=0)
        att_s[r0:r0 + L, :] = _attn_block(q_ref[0, r0:r0 + L, :], k_band, v_band, blk_bias, sink_ref)
    y = _dot(att_s[...], wo_ref[...])
    r = DEEPNORM_ALPHA * h_ref[0] + (1.0 + _mod_slice(mod_ref, N_MOD + 2)) * y
    o_ref[0] = _layer_norm(r, lng_ref[...], lnb_ref[...])


def _attn(sinks, q, k, v, h, mod, w_o, ln_g, ln_b):
    B, S, D = h.shape
    tq = ROW_TILE
    kv_dim = A_KV_HEADS * A_HEAD_DIM
    blocks_per_tile = tq // WINDOW

    def prev_map(b, t):
        return (b, jnp.maximum(t * blocks_per_tile - 1, 0), 0)

    return pl.pallas_call(
        _attn_kernel,
        grid=(B, S // tq),
        in_specs=[
            pl.BlockSpec(memory_space=pltpu.SMEM),
            _row_spec(tq, A_HEADS * A_HEAD_DIM),
            _row_spec(tq, kv_dim),
            pl.BlockSpec((1, WINDOW, kv_dim), prev_map),
            _row_spec(tq, kv_dim),
            pl.BlockSpec((1, WINDOW, kv_dim), prev_map),
            _row_spec(tq, D),
            pl.BlockSpec((1, 1, MOD_WIDTH), lambda b, t: (b, 0, 0)),
            _const_spec((A_HEADS * A_HEAD_DIM, D)),
            _const_spec((1, D)),
            _const_spec((1, D)),
        ],
        out_specs=_row_spec(tq, D),
        out_shape=jax.ShapeDtypeStruct((B, S, D), F32),
        scratch_shapes=[pltpu.VMEM((tq, A_HEADS * A_HEAD_DIM), BF16)],
        compiler_params=_compiler_params(2),
        name="attn",
    )(sinks, q, k, k, v, v, h, mod, w_o, ln_g, ln_b)


def _rope_expand_matrix():
    e = np.zeros((2 * ROPE_HALF, 3 * LANES_V7X), np.float32)
    for lane in range(LANES_V7X):
        d = lane % A_HEAD_DIM
        if d < ROPE_HALF:
            e[d, lane] = 1.0
            e[ROPE_HALF + d, 2 * LANES_V7X + lane] = -1.0
        elif d < ROPE_DIM:
            e[d - ROPE_HALF, lane] = 1.0
            e[d, LANES_V7X + lane] = 1.0
    return np.tile(e, (3, 1))


def kernel(x, c, positions, ada_w, ada_b, kv_ada_w, kv_ada_b, a_w_in, a_b_gates, a_norm_w, a_w_out,
           w_kv, b_w_q, b_sinks, b_w_o, mlp_w_up, mlp_w_down, ln_g, ln_b):
    B, S, D = x.shape
    assert D == D_MODEL and S % ROW_TILE == 0 and ROW_TILE % M_CHUNK == 0

    mod = _adaln(c, ada_w, ada_b, kv_ada_w, kv_ada_b).reshape(B, 1, MOD_WIDTH)

    n_main = 2 * NQ + 2 * D_MODEL
    w_in = a_w_in[0, :, :n_main].astype(BF16)
    w_gate = jnp.pad(a_w_in[0, :, n_main:], ((0, 0), (0, GATE_PAD - 2 * M_HEADS))).astype(BF16)
    h = _mix0(x, mod, w_in, w_gate, a_b_gates[0].reshape(2 * M_HEADS, 1),
              jnp.asarray(_gate_expand_matrix(), BF16), a_norm_w[0].reshape(1, D),
              a_w_out[0].astype(BF16), ln_g[0].reshape(1, D), ln_b[0].reshape(1, D))

    head_cols = np.concatenate([np.arange(hd * A_HEAD_DIM, (hd + 1) * A_HEAD_DIM) for hd in HEAD_ORDER])
    inv_freq = (ROPE_THETA ** (-jnp.arange(ROPE_HALF, dtype=F32) / ROPE_HALF)).reshape(ROPE_HALF, 1)
    expand = jnp.asarray(_rope_expand_matrix(), BF16)
    h, q, k, v = _mlp0(h, mod, positions.reshape(B, 1, S), mlp_w_up[0].astype(BF16),
                       mlp_w_down[0].astype(BF16), ln_g[1].reshape(1, D), ln_b[1].reshape(1, D),
                       w_kv.astype(BF16), b_w_q[0][:, head_cols].astype(BF16), inv_freq, expand)

    h = _attn(b_sinks[0][np.asarray(HEAD_ORDER)], q, k, v, h, mod, b_w_o[0][head_cols, :].astype(BF16),
              ln_g[2].reshape(1, D), ln_b[2].reshape(1, D))

    return _mlp1(h, mod, mlp_w_up[1].astype(BF16), mlp_w_down[1].astype(BF16),
                 ln_g[3].reshape(1, D), ln_b[3].reshape(1, D))
```

```python
import math

import numpy as np
import jax
import jax.numpy as jnp
from jax import lax
from jax.experimental import pallas as pl
from jax.experimental.pallas import tpu as pltpu

F32 = jnp.float32
BF16 = jnp.bfloat16

D_MODEL = 1024
DEPTH = 2
M_HEADS = 4
M_V_DIM = D_MODEL // M_HEADS
M_QK_DIM = M_V_DIM // 2
M_CHUNK = 128
GATE_CAP = 15.0
A_HEADS = 16
A_KV_HEADS = 4
A_GROUP = A_HEADS // A_KV_HEADS
A_HEAD_DIM = 64
WINDOW = 128
ROPE_DIM = A_HEAD_DIM // 4
ROPE_HALF = ROPE_DIM // 2
ROPE_THETA = 500000.0
D_FF = 4 * D_MODEL
DEEPNORM_ALPHA = (2 * DEPTH) ** 0.25
LN_EPS = 1e-5
RMS_EPS = 1e-6
LOG2_E = math.log2(math.e)
N_MOD = 6
MOD_KV_BASE = DEPTH * N_MOD * D_MODEL
MOD_WIDTH = MOD_KV_BASE + 2 * D_MODEL

LANES_V7X = 128
SUBLANES_V7X = 8
VMEM_LIMIT_BYTES_V7X = 56 * 1024 * 1024

ROW_TILE = 512
ADALN_COL_TILE = 1024


def _compiler_params(n_axes):
    return pltpu.CompilerParams(
        dimension_semantics=("arbitrary",) * n_axes,
        vmem_limit_bytes=VMEM_LIMIT_BYTES_V7X,
    )


def _const_spec(shape):
    return pl.BlockSpec(shape, lambda *_: (0,) * len(shape))


def _layer_norm(r, g, b):
    mu = jnp.mean(r, axis=-1, keepdims=True)
    d = r - mu
    var = jnp.mean(d * d, axis=-1, keepdims=True)
    return d * lax.rsqrt(var + LN_EPS) * g + b


def _mod_slice(mod_ref, idx):
    return mod_ref[0, :, idx * D_MODEL:(idx + 1) * D_MODEL]


def _dot(a, b):
    return jnp.dot(a, b, preferred_element_type=F32)


def _dot_nt(a, b):
    return lax.dot_general(a, b, (((1,), (1,)), ((), ())), preferred_element_type=F32)


def _dot_tn(a, b):
    return lax.dot_general(a, b, (((0,), (0,)), ((), ())), preferred_element_type=F32)


def _split3_bf16(x):
    hi = x.astype(BF16)
    r1 = x - hi.astype(F32)
    mid = r1.astype(BF16)
    lo = (r1 - mid.astype(F32)).astype(BF16)
    return hi, mid, lo


N_ADA_TILES = DEPTH * N_MOD * D_MODEL // ADALN_COL_TILE
N_KV_TILES = 2 * D_MODEL // ADALN_COL_TILE


def _adaln_kernel(c_ref, wa_ref, wk_ref, ba_ref, bk_ref, o_ref):
    i = pl.program_id(0)
    c = c_ref[...]
    cs = (c * jax.nn.sigmoid(c)).astype(BF16)

    @pl.when(i < N_ADA_TILES)
    def _():
        o_ref[...] = _dot(cs, wa_ref[0].astype(BF16)) + ba_ref[0]

    @pl.when(i >= N_ADA_TILES)
    def _():
        o_ref[...] = _dot(cs, wk_ref[...].astype(BF16)) + bk_ref[0]


def _adaln(c, ada_w, ada_b, kv_ada_w, kv_ada_b):
    batch = c.shape[0]
    tiles_per_layer = N_MOD * D_MODEL // ADALN_COL_TILE
    ba = ada_b.reshape(N_ADA_TILES, 1, ADALN_COL_TILE)
    bk = kv_ada_b.reshape(N_KV_TILES, 1, ADALN_COL_TILE)

    def ada_idx(i):
        return jnp.minimum(i, N_ADA_TILES - 1)

    def kv_idx(i):
        return jnp.maximum(i - N_ADA_TILES, 0)

    return pl.pallas_call(
        _adaln_kernel,
        grid=(N_ADA_TILES + N_KV_TILES,),
        in_specs=[
            _const_spec((batch, D_MODEL)),
            pl.BlockSpec((1, D_MODEL, ADALN_COL_TILE),
                         lambda i: (ada_idx(i) // tiles_per_layer, 0, ada_idx(i) % tiles_per_layer)),
            pl.BlockSpec((D_MODEL, ADALN_COL_TILE), lambda i: (0, kv_idx(i))),
            pl.BlockSpec((1, 1, ADALN_COL_TILE), lambda i: (ada_idx(i), 0, 0)),
            pl.BlockSpec((1, 1, ADALN_COL_TILE), lambda i: (kv_idx(i), 0, 0)),
        ],
        out_specs=pl.BlockSpec((batch, ADALN_COL_TILE), lambda i: (0, i)),
        out_shape=jax.ShapeDtypeStruct((batch, MOD_WIDTH), F32),
        compiler_params=_compiler_params(1),
        name="adaln",
    )(c, ada_w, kv_ada_w, ba, bk)


NQ = M_HEADS * M_QK_DIM
GATE_PAD = LANES_V7X
GATE_QUANTS = 3
GATE_PART_ROWS = 32
GATE_EXPAND_COLS = GATE_QUANTS * M_HEADS * LANES_V7X


def _gate_expand_matrix():
    e = np.zeros((GATE_PART_ROWS, GATE_EXPAND_COLS), np.float32)
    for quant in range(GATE_QUANTS):
        for h in range(M_HEADS):
            row = quant * SUBLANES_V7X + M_HEADS + h
            grp = quant * M_HEADS + h
            e[row, grp * LANES_V7X:(grp + 1) * LANES_V7X] = 1.0
    return np.tile(e, (3, 1))


def _segment_scan(x, op, identity):
    pos = lax.broadcasted_iota(jnp.int32, x.shape, 1) % M_CHUNK
    shift = 1
    while shift < M_CHUNK:
        x = op(x, jnp.where(pos >= shift, pltpu.roll(x, shift, axis=1), identity))
        shift *= 2
    return x


def _gate_scan(g_tm, bg_ref):
    L = M_CHUNK
    nc = g_tm.shape[0] // L
    gates_t = jnp.concatenate(
        [g_tm[c * L:(c + 1) * L, :].T[0:2 * M_HEADS, :] for c in range(nc)], axis=1) + bg_ref[...]
    capped = GATE_CAP * jnp.tanh(gates_t / GATE_CAP)
    log_f = jnp.minimum(capped, 0.0) - jnp.log1p(jnp.exp(-jnp.abs(capped)))
    bcum = _segment_scan(log_f, jnp.add, 0.0)
    a = pltpu.roll(capped, M_HEADS, axis=0) - bcum
    cmax = _segment_scan(a, jnp.maximum, -jnp.inf)
    stacked = jnp.concatenate([bcum, cmax, a, jnp.zeros_like(a)], axis=0)
    return a, jnp.concatenate(_split3_bf16(stacked), axis=0)


def _gate_expand(parts, gexp_ref):
    L = M_CHUNK
    return [_dot_tn(parts[:, c * L:(c + 1) * L], gexp_ref[...]) for c in range(parts.shape[1] // L)]


def _mlstm_tile(a_rows, cols, qk_s, v_s, h_s, c_s, n_s, m_s):
    L = M_CHUNK
    W = LANES_V7X
    nc = len(cols)
    heads = range(M_HEADS)
    causal = lax.broadcasted_iota(jnp.int32, (L, L), 1) <= lax.broadcasted_iota(jnp.int32, (L, L), 0)
    ones = jnp.ones((L, W), BF16)

    def rows(c):
        return slice(c * L, (c + 1) * L)

    def q_of(h, c):
        return qk_s[rows(c), h * M_QK_DIM:(h + 1) * M_QK_DIM]

    def k_of(h, c):
        return qk_s[rows(c), NQ + h * M_QK_DIM:NQ + (h + 1) * M_QK_DIM]

    def v1_of(h, c):
        return jnp.concatenate([v_s[rows(c), h * M_V_DIM:(h + 1) * M_V_DIM], ones], axis=1)

    def col(c, quant, h):
        g = quant * M_HEADS + h
        return cols[c][:, g * W:(g + 1) * W]

    scores = [[_dot_nt(q_of(h, c), k_of(h, c)) for c in range(nc)] for h in heads]

    m_in = [[None] * nc for _ in heads]
    mx = [[None] * nc for _ in heads]
    for h in heads:
        m_old = m_s[h:h + 1, :]
        for c in range(nc):
            m_in[h][c] = m_old
            mx[h][c] = jnp.maximum(m_old, col(c, 1, h)[L - 1:L, :])
            m_old = col(c, 0, h)[L - 1:L, :] + mx[h][c]
        m_s[h:h + 1, :] = m_old

    deltas = [[_dot_tn((k_of(h, c).astype(F32) * jnp.exp(col(c, 2, h) - mx[h][c])).astype(BF16), v1_of(h, c))
               for c in range(nc)] for h in heads]

    for h in heads:
        c_t = c_s[h]
        n_bc = n_s[h]
        for c in range(nc):
            m_old = m_in[h][c]
            a_row = a_rows[M_HEADS + h:M_HEADS + h + 1, rows(c)]
            mt = jnp.maximum(col(c, 1, h), m_old)
            w_intra = jnp.exp(jnp.where(causal, a_row - mt, -jnp.inf))
            w_inter = jnp.exp(m_old - mt)
            e_neg = jnp.exp(-col(c, 0, h) - mt)
            s_qk = scores[h][c] * w_intra
            lhs = jnp.concatenate([(q_of(h, c).astype(F32) * w_inter).astype(BF16), s_qk.astype(BF16)], axis=1)
            state = jnp.concatenate([c_t.astype(BF16), n_bc.astype(BF16)], axis=1)
            out = _dot(lhs, jnp.concatenate([state, v1_of(h, c)], axis=0))
            r_den = 1.0 / jnp.maximum(jnp.abs(out[:, M_V_DIM:M_V_DIM + W]), e_neg)
            for half in range(M_V_DIM // W):
                h_s[rows(c), h * M_V_DIM + half * W:h * M_V_DIM + (half + 1) * W] = (
                    out[:, half * W:(half + 1) * W] * r_den)
            decay = jnp.exp(m_old - mx[h][c])
            c_t = jnp.concatenate([decay] * (M_V_DIM // W), axis=1) * c_t + deltas[h][c][:, 0:M_V_DIM]
            n_bc = decay * n_bc + deltas[h][c][:, M_V_DIM:M_V_DIM + W]
        c_s[h] = c_t
        n_s[h] = n_bc


def _mix0_kernel(x_ref, mod_ref, win_ref, wg_ref, bg_ref, gexp_ref, nw_ref, wout_ref, lng_ref, lnb_ref,
                 o_ref, qk_s, v_s, og_s, h_s, c_s, n_s, m_s):
    @pl.when(pl.program_id(1) == 0)
    def _():
        c_s[...] = jnp.zeros_like(c_s)
        n_s[...] = jnp.zeros_like(n_s)
        m_s[...] = jnp.zeros_like(m_s)

    x = x_ref[0]
    u = (x * (1.0 + _mod_slice(mod_ref, 1)) + _mod_slice(mod_ref, 0)).astype(BF16)
    a_rows, parts = _gate_scan(_dot(u, wg_ref[...]), bg_ref)
    qk_s[:, 0:NQ] = _dot(u, win_ref[:, 0:NQ]).astype(BF16)
    qk_s[:, NQ:2 * NQ] = (_dot(u, win_ref[:, NQ:2 * NQ]) * (1.0 / math.sqrt(M_QK_DIM))).astype(BF16)
    v_s[...] = _dot(u, win_ref[:, 2 * NQ:2 * NQ + D_MODEL]).astype(BF16)
    og_s[...] = _dot(u, win_ref[:, 2 * NQ + D_MODEL:2 * NQ + 2 * D_MODEL])

    _mlstm_tile(a_rows, _gate_expand(parts, gexp_ref), qk_s, v_s, h_s, c_s, n_s, m_s)

    normed = []
    for h in range(M_HEADS):
        hh = h_s[:, h * M_V_DIM:(h + 1) * M_V_DIM]
        normed.append(hh * lax.rsqrt(jnp.mean(hh * hh, axis=-1, keepdims=True) + RMS_EPS))
    hn = jnp.concatenate(normed, axis=1)
    gated = hn * nw_ref[...] * jax.nn.sigmoid(og_s[...])
    y = _dot(gated.astype(BF16), wout_ref[...])
    r = DEEPNORM_ALPHA * x + (1.0 + _mod_slice(mod_ref, 2)) * y
    o_ref[0] = _layer_norm(r, lng_ref[...], lnb_ref[...])


def _mix0(x, mod, w_in, w_gate, b_gates, gate_expand, norm_w, w_out, ln_g, ln_b):
    B, S, D = x.shape
    tm = ROW_TILE
    n_main = 2 * NQ + 2 * D_MODEL
    return pl.pallas_call(
        _mix0_kernel,
        grid=(B, S // tm),
        in_specs=[
            pl.BlockSpec((1, tm, D), lambda b, t: (b, t, 0)),
            pl.BlockSpec((1, 1, MOD_WIDTH), lambda b, t: (b, 0, 0)),
            _const_spec((D, n_main)),
            _const_spec((D, GATE_PAD)),
            _const_spec((2 * M_HEADS, 1)),
            _const_spec(gate_expand.shape),
            _const_spec((1, D)),
            _const_spec((D, D)),
            _const_spec((1, D)),
            _const_spec((1, D)),
        ],
        out_specs=pl.BlockSpec((1, tm, D), lambda b, t: (b, t, 0)),
        out_shape=jax.ShapeDtypeStruct((B, S, D), F32),
        scratch_shapes=[
            pltpu.VMEM((tm, 2 * NQ), BF16),
            pltpu.VMEM((tm, D_MODEL), BF16),
            pltpu.VMEM((tm, D_MODEL), F32),
            pltpu.VMEM((tm, D_MODEL), F32),
            pltpu.VMEM((M_HEADS, M_QK_DIM, M_V_DIM), F32),
            pltpu.VMEM((M_HEADS, M_QK_DIM, LANES_V7X), F32),
            pltpu.VMEM((SUBLANES_V7X, LANES_V7X), F32),
        ],
        compiler_params=_compiler_params(2),
        name="mix0",
    )(x, mod, w_in, w_gate, b_gates, gate_expand, norm_w, w_out, ln_g, ln_b)


def _mlp_core(h, mod_ref, layer, wup_ref, wdn_ref, lng_ref, lnb_ref):
    base = layer * N_MOD
    u = (h * (1.0 + _mod_slice(mod_ref, base + 4)) + _mod_slice(mod_ref, base + 3)).astype(BF16)
    a = jnp.maximum(_dot(u, wup_ref[...]), 0.0)
    y = _dot((a * a).astype(BF16), wdn_ref[...])
    r = DEEPNORM_ALPHA * h + (1.0 + _mod_slice(mod_ref, base + 5)) * y
    return _layer_norm(r, lng_ref[...], lnb_ref[...])


def _rope_tables(pos_row, invf_ref, expand_ref):
    ang = pos_row.astype(F32) * invf_ref[...]
    trig = jnp.concatenate([jnp.cos(ang), jnp.sin(ang)], axis=0)
    parts = jnp.concatenate(_split3_bf16(trig), axis=0)
    tab = _dot_tn(parts, expand_ref[...])
    lane = lax.broadcasted_iota(jnp.int32, (1, LANES_V7X), 1)
    cos_t = tab[:, 0:LANES_V7X] + jnp.where(lane % A_HEAD_DIM >= ROPE_DIM, 1.0, 0.0)
    return cos_t, tab[:, LANES_V7X:2 * LANES_V7X], tab[:, 2 * LANES_V7X:3 * LANES_V7X]


def _rope(x, tables):
    cos_t, sin_up, sin_dn = tables
    out = []
    for g in range(x.shape[1] // LANES_V7X):
        xg = x[:, g * LANES_V7X:(g + 1) * LANES_V7X]
        from_lo = pltpu.roll(xg, ROPE_HALF, axis=1)
        from_hi = pltpu.roll(xg, LANES_V7X - ROPE_HALF, axis=1)
        out.append(xg * cos_t + from_lo * sin_up + from_hi * sin_dn)
    return jnp.concatenate(out, axis=1)


def _mlp0_kernel(h_ref, mod_ref, pos_ref, wup_ref, wdn_ref, lng_ref, lnb_ref, wkv_ref, wq_ref,
                 invf_ref, expand_ref, o_ref, q_ref, k_ref, v_ref):
    kv_dim = A_KV_HEADS * A_HEAD_DIM
    kv_shift = mod_ref[0, :, MOD_KV_BASE:MOD_KV_BASE + D_MODEL]
    kv_scale = mod_ref[0, :, MOD_KV_BASE + D_MODEL:MOD_KV_BASE + 2 * D_MODEL]
    h1 = _mlp_core(h_ref[0], mod_ref, 0, wup_ref, wdn_ref, lng_ref, lnb_ref)
    o_ref[0] = h1
    kv = _dot((h1 * (1.0 + kv_scale) + kv_shift).astype(BF16), wkv_ref[...])
    uq = (h1 * (1.0 + _mod_slice(mod_ref, N_MOD + 1)) + _mod_slice(mod_ref, N_MOD)).astype(BF16)
    q = _dot(uq, wq_ref[...])
    tables = _rope_tables(pos_ref[0], invf_ref, expand_ref)
    q_ref[0] = (_rope(q, tables) * (LOG2_E / math.sqrt(A_HEAD_DIM))).astype(BF16)
    k_ref[0] = _rope(kv[:, 0:kv_dim], tables).astype(BF16)
    v_ref[0] = kv[:, kv_dim:2 * kv_dim].astype(BF16)


def _mlp1_kernel(h_ref, mod_ref, wup_ref, wdn_ref, lng_ref, lnb_ref, o_ref):
    o_ref[0] = _mlp_core(h_ref[0], mod_ref, 1, wup_ref, wdn_ref, lng_ref, lnb_ref)


def _row_spec(tm, width):
    return pl.BlockSpec((1, tm, width), lambda b, t: (b, t, 0))


def _mlp0(h, mod, pos, w_up, w_dn, ln_g, ln_b, w_kv, w_q, inv_freq, expand):
    B, S, D = h.shape
    tm = ROW_TILE
    kv_dim = A_KV_HEADS * A_HEAD_DIM
    return pl.pallas_call(
        _mlp0_kernel,
        grid=(B, S // tm),
        in_specs=[
            _row_spec(tm, D),
            pl.BlockSpec((1, 1, MOD_WIDTH), lambda b, t: (b, 0, 0)),
            pl.BlockSpec((1, 1, tm), lambda b, t: (b, 0, t)),
            _const_spec((D, D_FF)),
            _const_spec((D_FF, D)),
            _const_spec((1, D)),
            _const_spec((1, D)),
            _const_spec((D, 2 * kv_dim)),
            _const_spec((D, A_HEADS * A_HEAD_DIM)),
            _const_spec((ROPE_HALF, 1)),
            _const_spec(expand.shape),
        ],
        out_specs=[_row_spec(tm, D), _row_spec(tm, A_HEADS * A_HEAD_DIM),
                   _row_spec(tm, kv_dim), _row_spec(tm, kv_dim)],
        out_shape=[
            jax.ShapeDtypeStruct((B, S, D), F32),
            jax.ShapeDtypeStruct((B, S, A_HEADS * A_HEAD_DIM), BF16),
            jax.ShapeDtypeStruct((B, S, kv_dim), BF16),
            jax.ShapeDtypeStruct((B, S, kv_dim), BF16),
        ],
        compiler_params=_compiler_params(2),
        name="mlp0",
    )(h, mod, pos, w_up, w_dn, ln_g, ln_b, w_kv, w_q, inv_freq, expand)


def _mlp1(h, mod, w_up, w_dn, ln_g, ln_b):
    B, S, D = h.shape
    tm = ROW_TILE
    return pl.pallas_call(
        _mlp1_kernel,
        grid=(B, S // tm),
        in_specs=[
            _row_spec(tm, D),
            pl.BlockSpec((1, 1, MOD_WIDTH), lambda b, t: (b, 0, 0)),
            _const_spec((D, D_FF)),
            _const_spec((D_FF, D)),
            _const_spec((1, D)),
            _const_spec((1, D)),
        ],
        out_specs=_row_spec(tm, D),
        out_shape=jax.ShapeDtypeStruct((B, S, D), F32),
        compiler_params=_compiler_params(2),
        name="mlp1",
    )(h, mod, w_up, w_dn, ln_g, ln_b)


HEAD_ORDER = tuple(8 * (G // 4) + 4 * half + (G % 4) for G in range(A_HEADS // 2) for half in range(2))
KV_PAIRS = A_KV_HEADS // 2


def _attn_block(q_blk, k_band, v_band, bias, sink_ref):
    L = WINDOW
    W = LANES_V7X
    lo_half = lax.broadcasted_iota(jnp.int32, (2 * L, W), 1) < A_HEAD_DIM
    lo_half_q = lax.broadcasted_iota(jnp.int32, (L, W), 1) < A_HEAD_DIM
    zero = jnp.zeros((), BF16)
    ones_bd = jnp.concatenate([jnp.where(lo_half, 1.0, 0.0), jnp.where(lo_half, 0.0, 1.0)], axis=0).astype(BF16)
    outs = [None] * (A_HEADS // 2)
    for p in range(KV_PAIRS):
        kp = k_band[:, p * W:(p + 1) * W]
        vp = v_band[:, p * W:(p + 1) * W]
        k_bd = jnp.concatenate([jnp.where(lo_half, kp, zero), jnp.where(lo_half, zero, kp)], axis=0)
        v_bd = jnp.concatenate([jnp.where(lo_half, vp, zero), jnp.where(lo_half, zero, vp)], axis=0)
        lhs = jnp.concatenate([q_blk[:, (4 * p + j) * W:(4 * p + j + 1) * W] for j in range(A_GROUP)], axis=0)
        s_all = _dot_nt(lhs, k_bd)
        e_rows, sink_rows = [], []
        for j in range(A_GROUP):
            halves, sink_terms = [], []
            for half in range(2):
                s = s_all[j * L:(j + 1) * L, half * 2 * L:(half + 1) * 2 * L] + bias
                sink = sink_ref[2 * (4 * p + j) + half] * LOG2_E
                mx = jnp.maximum(jnp.max(s, axis=1, keepdims=True), sink)
                halves.append(jnp.exp2(s - mx).astype(BF16))
                sink_terms.append(jnp.exp2(sink - mx))
            e_rows.append(jnp.concatenate(halves, axis=1))
            sink_rows.append(jnp.where(lo_half_q, sink_terms[0], sink_terms[1]))
        e_all = jnp.concatenate(e_rows, axis=0)
        o_ext = _dot(e_all, jnp.concatenate([v_bd, ones_bd], axis=1))
        o_all = o_ext[:, 0:W] / (o_ext[:, W:2 * W] + jnp.concatenate(sink_rows, axis=0))
        for j in range(A_GROUP):
            outs[4 * p + j] = o_all[j * L:(j + 1) * L, :].astype(BF16)
    return jnp.concatenate(outs, axis=1)


def _attn_kernel(sink_ref, q_ref, k_ref, kprev_ref, v_ref, vprev_ref, h_ref, mod_ref, wo_ref,
                 lng_ref, lnb_ref, o_ref, att_s):
    L = WINDOW
    tq = q_ref.shape[1]
    qi = lax.broadcasted_iota(jnp.int32, (L, 2 * L), 0)
    sj = lax.broadcasted_iota(jnp.int32, (L, 2 * L), 1)
    in_window = (sj > qi) & (sj <= qi + L)
    bias = jnp.where(in_window, 0.0, -jnp.inf)
    first_bias = jnp.where(in_window & ((sj >= L) | (pl.program_id(1) > 0)), 0.0, -jnp.inf)
    for blk in range(tq // L):
        r0 = blk * L
        if blk == 0:
            k_prev, v_prev, blk_bias = kprev_ref[0], vprev_ref[0], first_bias
        else:
            k_prev, v_prev, blk_bias = k_ref[0, r0 - L:r0, :], v_ref[0, r0 - L:r0, :], bias
        k_band = jnp.concatenate([k_prev, k_ref[0, r0:r0 + L, :]], axis=0)
        v_band = jnp.concatenate([v_prev, v_ref[0, r0:r0 + L, :]], axis=0)
        att_s[r0:r0 + L, :] = _attn_block(q_ref[0, r0:r0 + L, :], k_band, v_band, blk_bias, sink_ref)
    y = _dot(att_s[...], wo_ref[...])
    r = DEEPNORM_ALPHA * h_ref[0] + (1.0 + _mod_slice(mod_ref, N_MOD + 2)) * y
    o_ref[0] = _layer_norm(r, lng_ref[...], lnb_ref[...])


def _attn(sinks, q, k, v, h, mod, w_o, ln_g, ln_b):
    B, S, D = h.shape
    tq = ROW_TILE
    kv_dim = A_KV_HEADS * A_HEAD_DIM
    blocks_per_tile = tq // WINDOW

    def prev_map(b, t):
        return (b, jnp.maximum(t * blocks_per_tile - 1, 0), 0)

    return pl.pallas_call(
        _attn_kernel,
        grid=(B, S // tq),
        in_specs=[
            pl.BlockSpec(memory_space=pltpu.SMEM),
            _row_spec(tq, A_HEADS * A_HEAD_DIM),
            _row_spec(tq, kv_dim),
            pl.BlockSpec((1, WINDOW, kv_dim), prev_map),
            _row_spec(tq, kv_dim),
            pl.BlockSpec((1, WINDOW, kv_dim), prev_map),
            _row_spec(tq, D),
            pl.BlockSpec((1, 1, MOD_WIDTH), lambda b, t: (b, 0, 0)),
            _const_spec((A_HEADS * A_HEAD_DIM, D)),
            _const_spec((1, D)),
            _const_spec((1, D)),
        ],
        out_specs=_row_spec(tq, D),
        out_shape=jax.ShapeDtypeStruct((B, S, D), F32),
        scratch_shapes=[pltpu.VMEM((tq, A_HEADS * A_HEAD_DIM), BF16)],
        compiler_params=_compiler_params(2),
        name="attn",
    )(sinks, q, k, k, v, v, h, mod, w_o, ln_g, ln_b)


def _rope_expand_matrix():
    e = np.zeros((2 * ROPE_HALF, 3 * LANES_V7X), np.float32)
    for lane in range(LANES_V7X):
        d = lane % A_HEAD_DIM
        if d < ROPE_HALF:
            e[d, lane] = 1.0
            e[ROPE_HALF + d, 2 * LANES_V7X + lane] = -1.0
        elif d < ROPE_DIM:
            e[d - ROPE_HALF, lane] = 1.0
            e[d, LANES_V7X + lane] = 1.0
    return np.tile(e, (3, 1))


def kernel(x, c, positions, ada_w, ada_b, kv_ada_w, kv_ada_b, a_w_in, a_b_gates, a_norm_w, a_w_out,
           w_kv, b_w_q, b_sinks, b_w_o, mlp_w_up, mlp_w_down, ln_g, ln_b):
    B, S, D = x.shape
    assert D == D_MODEL and S % ROW_TILE == 0 and ROW_TILE % M_CHUNK == 0

    mod = _adaln(c, ada_w, ada_b, kv_ada_w, kv_ada_b).reshape(B, 1, MOD_WIDTH)

    n_main = 2 * NQ + 2 * D_MODEL
    w_in = a_w_in[0, :, :n_main].astype(BF16)
    w_gate = jnp.pad(a_w_in[0, :, n_main:], ((0, 0), (0, GATE_PAD - 2 * M_HEADS))).astype(BF16)
    h = _mix0(x, mod, w_in, w_gate, a_b_gates[0].reshape(2 * M_HEADS, 1),
              jnp.asarray(_gate_expand_matrix(), BF16), a_norm_w[0].reshape(1, D),
              a_w_out[0].astype(BF16), ln_g[0].reshape(1, D), ln_b[0].reshape(1, D))

    head_cols = np.concatenate([np.arange(hd * A_HEAD_DIM, (hd + 1) * A_HEAD_DIM) for hd in HEAD_ORDER])
    inv_freq = (ROPE_THETA ** (-jnp.arange(ROPE_HALF, dtype=F32) / ROPE_HALF)).reshape(ROPE_HALF, 1)
    expand = jnp.asarray(_rope_expand_matrix(), BF16)
    h, q, k, v = _mlp0(h, mod, positions.reshape(B, 1, S), mlp_w_up[0].astype(BF16),
                       mlp_w_down[0].astype(BF16), ln_g[1].reshape(1, D), ln_b[1].reshape(1, D),
                       w_kv.astype(BF16), b_w_q[0][:, head_cols].astype(BF16), inv_freq, expand)

    h = _attn(b_sinks[0][np.asarray(HEAD_ORDER)], q, k, v, h, mod, b_w_o[0][head_cols, :].astype(BF16),
              ln_g[2].reshape(1, D), ln_b[2].reshape(1, D))

    return _mlp1(h, mod, mlp_w_up[1].astype(BF16), mlp_w_down[1].astype(BF16),
                 ln_g[3].reshape(1, D), ln_b[3].reshape(1, D))
```

```python
import functools
import math

import numpy as np
import jax
import jax.numpy as jnp
from jax import lax
from jax.experimental import pallas as pl
from jax.experimental.pallas import tpu as pltpu

F32 = jnp.float32
BF16 = jnp.bfloat16

D_MODEL = 1024
DEPTH = 2
M_HEADS = 4
M_V_DIM = D_MODEL // M_HEADS
M_QK_DIM = M_V_DIM // 2
M_CHUNK = 128
GATE_CAP = 15.0
A_HEADS = 16
A_KV_HEADS = 4
A_GROUP = A_HEADS // A_KV_HEADS
A_HEAD_DIM = 64
A_KV_DIM = A_KV_HEADS * A_HEAD_DIM
WINDOW = 128
ROPE_DIM = A_HEAD_DIM // 4
ROPE_HALF = ROPE_DIM // 2
ROPE_THETA = 500000.0
D_FF = 4 * D_MODEL
DEEPNORM_ALPHA = (2 * DEPTH) ** 0.25
LN_EPS = 1e-5
RMS_EPS = 1e-6
LOG2_E = math.log2(math.e)
N_MOD = 6
MOD_KV_BASE = DEPTH * N_MOD * D_MODEL
MOD_WIDTH = MOD_KV_BASE + 2 * D_MODEL

LANES_V7X = 128
SUBLANES_V7X = 8
VMEM_LIMIT_BYTES_V7X = 56 * 1024 * 1024

ROW_TILE = 512
ADALN_COL_TILE = 1024
N_STAGE = 8


def _compiler_params():
    return pltpu.CompilerParams(dimension_semantics=("arbitrary",), vmem_limit_bytes=VMEM_LIMIT_BYTES_V7X)


def _const_spec(shape):
    return pl.BlockSpec(shape, lambda *_: (0,) * len(shape))


def _tile_index(i, tiles_per_seq):
    j = jnp.maximum(i - N_STAGE, 0)
    return j // tiles_per_seq, j % tiles_per_seq


def _row_spec(tm, width, tiles_per_seq):
    def index(i):
        b, t = _tile_index(i, tiles_per_seq)
        return b, t, 0
    return pl.BlockSpec((1, tm, width), index)


def _mod_spec(tiles_per_seq):
    return pl.BlockSpec((1, 1, MOD_WIDTH), lambda i: (_tile_index(i, tiles_per_seq)[0], 0, 0))


def _stage_spec(w, layer):
    _, rows, cols = w.shape
    return pl.BlockSpec((1, rows // N_STAGE, cols), lambda i: (layer, jnp.minimum(i, N_STAGE - 1), 0))


def _stage_store(i, dst_ref, chunk):
    r = chunk.shape[0]
    dst_ref[pl.ds(pl.multiple_of(i * r, r), r), :] = chunk


def _layer_norm(r, g, b):
    mu = jnp.mean(r, axis=-1, keepdims=True)
    d = r - mu
    var = jnp.mean(d * d, axis=-1, keepdims=True)
    return d * lax.rsqrt(var + LN_EPS) * g + b


def _mod_slice(mod_ref, idx):
    return mod_ref[0, :, idx * D_MODEL:(idx + 1) * D_MODEL]


def _dot(a, b):
    return jnp.dot(a, b, preferred_element_type=F32)


def _dot_nt(a, b):
    return lax.dot_general(a, b, (((1,), (1,)), ((), ())), preferred_element_type=F32)


def _dot_tn(a, b):
    return lax.dot_general(a, b, (((0,), (0,)), ((), ())), preferred_element_type=F32)


def _split3_bf16(x):
    hi = x.astype(BF16)
    r1 = x - hi.astype(F32)
    mid = r1.astype(BF16)
    lo = (r1 - mid.astype(F32)).astype(BF16)
    return hi, mid, lo


N_ADA_TILES = DEPTH * N_MOD * D_MODEL // ADALN_COL_TILE
N_KV_TILES = 2 * D_MODEL // ADALN_COL_TILE


def _adaln_kernel(c_ref, wa_ref, wk_ref, ba_ref, bk_ref, o_ref):
    i = pl.program_id(0)
    c = c_ref[...]
    cs = (c * jax.nn.sigmoid(c)).astype(BF16)

    @pl.when(i < N_ADA_TILES)
    def _():
        o_ref[...] = _dot(cs, wa_ref[0].astype(BF16)) + ba_ref[0]

    @pl.when(i >= N_ADA_TILES)
    def _():
        o_ref[...] = _dot(cs, wk_ref[...].astype(BF16)) + bk_ref[0]


def _adaln(c, ada_w, ada_b, kv_ada_w, kv_ada_b):
    batch = c.shape[0]
    tiles_per_layer = N_MOD * D_MODEL // ADALN_COL_TILE
    ba = ada_b.reshape(N_ADA_TILES, 1, ADALN_COL_TILE)
    bk = kv_ada_b.reshape(N_KV_TILES, 1, ADALN_COL_TILE)

    def ada_idx(i):
        return jnp.minimum(i, N_ADA_TILES - 1)

    def kv_idx(i):
        return jnp.maximum(i - N_ADA_TILES, 0)

    return pl.pallas_call(
        _adaln_kernel,
        grid=(N_ADA_TILES + N_KV_TILES,),
        in_specs=[
            _const_spec((batch, D_MODEL)),
            pl.BlockSpec((1, D_MODEL, ADALN_COL_TILE),
                         lambda i: (ada_idx(i) // tiles_per_layer, 0, ada_idx(i) % tiles_per_layer)),
            pl.BlockSpec((D_MODEL, ADALN_COL_TILE), lambda i: (0, kv_idx(i))),
            pl.BlockSpec((1, 1, ADALN_COL_TILE), lambda i: (ada_idx(i), 0, 0)),
            pl.BlockSpec((1, 1, ADALN_COL_TILE), lambda i: (kv_idx(i), 0, 0)),
        ],
        out_specs=pl.BlockSpec((batch, ADALN_COL_TILE), lambda i: (0, i)),
        out_shape=jax.ShapeDtypeStruct((batch, MOD_WIDTH), F32),
        compiler_params=_compiler_params(),
        name="adaln",
    )(c, ada_w, kv_ada_w, ba, bk)


NQ = M_HEADS * M_QK_DIM
N_MAIN = 2 * NQ + 2 * D_MODEL
GATE_PAD = LANES_V7X
GATE_QUANTS = 3
GATE_PART_ROWS = 32
GATE_EXPAND_COLS = GATE_QUANTS * M_HEADS * LANES_V7X


def _gate_expand_matrix():
    e = np.zeros((GATE_PART_ROWS, GATE_EXPAND_COLS), np.float32)
    for quant in range(GATE_QUANTS):
        for h in range(M_HEADS):
            row = quant * SUBLANES_V7X + M_HEADS + h
            grp = quant * M_HEADS + h
            e[row, grp * LANES_V7X:(grp + 1) * LANES_V7X] = 1.0
    return np.tile(e, (3, 1))


def _segment_scan(x, op, identity):
    pos = lax.broadcasted_iota(jnp.int32, x.shape, 1) % M_CHUNK
    shift = 1
    while shift < M_CHUNK:
        x = op(x, jnp.where(pos >= shift, pltpu.roll(x, shift, axis=1), identity))
        shift *= 2
    return x


def _gate_scan(g_tm, bg_ref):
    L = M_CHUNK
    nc = g_tm.shape[0] // L
    gates_t = jnp.concatenate(
        [g_tm[c * L:(c + 1) * L, :].T[0:2 * M_HEADS, :] for c in range(nc)], axis=1) + bg_ref[...]
    capped = GATE_CAP * jnp.tanh(gates_t / GATE_CAP)
    log_f = jnp.minimum(capped, 0.0) - jnp.log1p(jnp.exp(-jnp.abs(capped)))
    bcum = _segment_scan(log_f, jnp.add, 0.0)
    a = pltpu.roll(capped, M_HEADS, axis=0) - bcum
    cmax = _segment_scan(a, jnp.maximum, -jnp.inf)
    stacked = jnp.concatenate([bcum, cmax, a, jnp.zeros_like(a)], axis=0)
    return a, jnp.concatenate(_split3_bf16(stacked), axis=0)


def _gate_expand(parts, gexp_ref):
    L = M_CHUNK
    return [_dot_tn(parts[:, c * L:(c + 1) * L], gexp_ref[...]) for c in range(parts.shape[1] // L)]


def _mlstm_tile(a_rows, cols, qk_s, v_s, h_s, c_s, n_s, m_s):
    L = M_CHUNK
    W = LANES_V7X
    nc = len(cols)
    heads = range(M_HEADS)
    causal = lax.broadcasted_iota(jnp.int32, (L, L), 1) <= lax.broadcasted_iota(jnp.int32, (L, L), 0)
    ones = jnp.ones((L, W), BF16)

    def rows(c):
        return slice(c * L, (c + 1) * L)

    def q_of(h, c):
        return qk_s[rows(c), h * M_QK_DIM:(h + 1) * M_QK_DIM]

    def k_of(h, c):
        return qk_s[rows(c), NQ + h * M_QK_DIM:NQ + (h + 1) * M_QK_DIM]

    def v1_of(h, c):
        return jnp.concatenate([v_s[rows(c), h * M_V_DIM:(h + 1) * M_V_DIM], ones], axis=1)

    def col(c, quant, h):
        g = quant * M_HEADS + h
        return cols[c][:, g * W:(g + 1) * W]

    scores = [[_dot_nt(q_of(h, c), k_of(h, c)) for c in range(nc)] for h in heads]

    m_in = [[None] * nc for _ in heads]
    mx = [[None] * nc for _ in heads]
    for h in heads:
        m_old = m_s[h:h + 1, :]
        for c in range(nc):
            m_in[h][c] = m_old
            mx[h][c] = jnp.maximum(m_old, col(c, 1, h)[L - 1:L, :])
            m_old = col(c, 0, h)[L - 1:L, :] + mx[h][c]
        m_s[h:h + 1, :] = m_old

    deltas = [[_dot_tn((k_of(h, c).astype(F32) * jnp.exp(col(c, 2, h) - mx[h][c])).astype(BF16), v1_of(h, c))
               for c in range(nc)] for h in heads]

    for h in heads:
        c_t = c_s[h]
        n_bc = n_s[h]
        for c in range(nc):
            m_old = m_in[h][c]
            a_row = a_rows[M_HEADS + h:M_HEADS + h + 1, rows(c)]
            mt = jnp.maximum(col(c, 1, h), m_old)
            w_intra = jnp.exp(jnp.where(causal, a_row - mt, -jnp.inf))
            w_inter = jnp.exp(m_old - mt)
            e_neg = jnp.exp(-col(c, 0, h) - mt)
            s_qk = scores[h][c] * w_intra
            lhs = jnp.concatenate([(q_of(h, c).astype(F32) * w_inter).astype(BF16), s_qk.astype(BF16)], axis=1)
            state = jnp.concatenate([c_t.astype(BF16), n_bc.astype(BF16)], axis=1)
            out = _dot(lhs, jnp.concatenate([state, v1_of(h, c)], axis=0))
            r_den = 1.0 / jnp.maximum(jnp.abs(out[:, M_V_DIM:M_V_DIM + W]), e_neg)
            for half in range(M_V_DIM // W):
                h_s[rows(c), h * M_V_DIM + half * W:h * M_V_DIM + (half + 1) * W] = (
                    out[:, half * W:(half + 1) * W] * r_den)
            decay = jnp.exp(m_old - mx[h][c])
            c_t = jnp.concatenate([decay] * (M_V_DIM // W), axis=1) * c_t + deltas[h][c][:, 0:M_V_DIM]
            n_bc = decay * n_bc + deltas[h][c][:, M_V_DIM:M_V_DIM + W]
        c_s[h] = c_t
        n_s[h] = n_bc


def _mix0_tile(tiles_per_seq, x_ref, mod_ref, wg_ref, bg_ref, gexp_ref, nw_ref, lng_ref, lnb_ref,
               o_ref, win_s, wout_s, qk_s, v_s, og_s, h_s, c_s, n_s, m_s):
    @pl.when((pl.program_id(0) - N_STAGE) % tiles_per_seq == 0)
    def _():
        c_s[...] = jnp.zeros_like(c_s)
        n_s[...] = jnp.zeros_like(n_s)
        m_s[...] = jnp.zeros_like(m_s)

    x = x_ref[0]
    u = (x * (1.0 + _mod_slice(mod_ref, 1)) + _mod_slice(mod_ref, 0)).astype(BF16)
    a_rows, parts = _gate_scan(_dot(u, wg_ref[...]), bg_ref)
    qk_s[:, 0:NQ] = _dot(u, win_s[:, 0:NQ]).astype(BF16)
    qk_s[:, NQ:2 * NQ] = (_dot(u, win_s[:, NQ:2 * NQ]) * (1.0 / math.sqrt(M_QK_DIM))).astype(BF16)
    v_s[...] = _dot(u, win_s[:, 2 * NQ:2 * NQ + D_MODEL]).astype(BF16)
    og_s[...] = _dot(u, win_s[:, 2 * NQ + D_MODEL:N_MAIN])

    _mlstm_tile(a_rows, _gate_expand(parts, gexp_ref), qk_s, v_s, h_s, c_s, n_s, m_s)

    normed = []
    for h in range(M_HEADS):
        hh = h_s[:, h * M_V_DIM:(h + 1) * M_V_DIM]
        normed.append(hh * lax.rsqrt(jnp.mean(hh * hh, axis=-1, keepdims=True) + RMS_EPS))
    hn = jnp.concatenate(normed, axis=1)
    gated = hn * nw_ref[...] * jax.nn.sigmoid(og_s[...])
    y = _dot(gated.astype(BF16), wout_s[...])
    r = DEEPNORM_ALPHA * x + (1.0 + _mod_slice(mod_ref, 2)) * y
    o_ref[0] = _layer_norm(r, lng_ref[...], lnb_ref[...])


def _mix0_kernel(tiles_per_seq, x_ref, mod_ref, win_c, wout_c, wg_ref, bg_ref, gexp_ref, nw_ref,
                 lng_ref, lnb_ref, o_ref, win_s, wout_s, *scratch):
    i = pl.program_id(0)

    @pl.when(i < N_STAGE)
    def _():
        _stage_store(i, win_s, win_c[0][:, 0:N_MAIN].astype(BF16))
        _stage_store(i, wout_s, wout_c[0].astype(BF16))

    @pl.when(i >= N_STAGE)
    def _():
        _mix0_tile(tiles_per_seq, x_ref, mod_ref, wg_ref, bg_ref, gexp_ref, nw_ref, lng_ref, lnb_ref,
                   o_ref, win_s, wout_s, *scratch)


def _mix0(x, mod, a_w_in, a_w_out, w_gate, b_gates, gate_expand, norm_w, ln_g, ln_b):
    B, S, D = x.shape
    tm = ROW_TILE
    n_t = S // tm
    return pl.pallas_call(
        functools.partial(_mix0_kernel, n_t),
        grid=(N_STAGE + B * n_t,),
        in_specs=[
            _row_spec(tm, D, n_t),
            _mod_spec(n_t),
            _stage_spec(a_w_in, 0),
            _stage_spec(a_w_out, 0),
            _const_spec((D, GATE_PAD)),
            _const_spec((2 * M_HEADS, 1)),
            _const_spec(gate_expand.shape),
            _const_spec((1, D)),
            _const_spec((1, D)),
            _const_spec((1, D)),
        ],
        out_specs=_row_spec(tm, D, n_t),
        out_shape=jax.ShapeDtypeStruct((B, S, D), F32),
        scratch_shapes=[
            pltpu.VMEM((D, N_MAIN), BF16),
            pltpu.VMEM((D, D), BF16),
            pltpu.VMEM((tm, 2 * NQ), BF16),
            pltpu.VMEM((tm, D_MODEL), BF16),
            pltpu.VMEM((tm, D_MODEL), F32),
            pltpu.VMEM((tm, D_MODEL), F32),
            pltpu.VMEM((M_HEADS, M_QK_DIM, M_V_DIM), F32),
            pltpu.VMEM((M_HEADS, M_QK_DIM, LANES_V7X), F32),
            pltpu.VMEM((SUBLANES_V7X, LANES_V7X), F32),
        ],
        compiler_params=_compiler_params(),
        name="mix0",
    )(x, mod, a_w_in, a_w_out, w_gate, b_gates, gate_expand, norm_w, ln_g, ln_b)


def _mlp_core(h, mod_ref, layer, wup_s, wdn_s, lng_ref, lnb_ref):
    base = layer * N_MOD
    u = (h * (1.0 + _mod_slice(mod_ref, base + 4)) + _mod_slice(mod_ref, base + 3)).astype(BF16)
    a = jnp.maximum(_dot(u, wup_s[...]), 0.0)
    y = _dot((a * a).astype(BF16), wdn_s[...])
    r = DEEPNORM_ALPHA * h + (1.0 + _mod_slice(mod_ref, base + 5)) * y
    return _layer_norm(r, lng_ref[...], lnb_ref[...])


def _rope_tables(pos_row, invf_ref, expand_ref):
    ang = pos_row.astype(F32) * invf_ref[...]
    trig = jnp.concatenate([jnp.cos(ang), jnp.sin(ang)], axis=0)
    parts = jnp.concatenate(_split3_bf16(trig), axis=0)
    tab = _dot_tn(parts, expand_ref[...])
    lane = lax.broadcasted_iota(jnp.int32, (1, LANES_V7X), 1)
    cos_t = tab[:, 0:LANES_V7X] + jnp.where(lane % A_HEAD_DIM >= ROPE_DIM, 1.0, 0.0)
    return cos_t, tab[:, LANES_V7X:2 * LANES_V7X], tab[:, 2 * LANES_V7X:3 * LANES_V7X]


def _rope(x, tables):
    cos_t, sin_up, sin_dn = tables
    out = []
    for g in range(x.shape[1] // LANES_V7X):
        xg = x[:, g * LANES_V7X:(g + 1) * LANES_V7X]
        from_lo = pltpu.roll(xg, ROPE_HALF, axis=1)
        from_hi = pltpu.roll(xg, LANES_V7X - ROPE_HALF, axis=1)
        out.append(xg * cos_t + from_lo * sin_up + from_hi * sin_dn)
    return jnp.concatenate(out, axis=1)


def _mlp0_kernel(h_ref, mod_ref, pos_ref, wup_c, wdn_c, wkv_c, wq_c, perm_ref, lng_ref, lnb_ref,
                 invf_ref, expand_ref, o_ref, q_ref, k_ref, v_ref, wup_s, wdn_s, wkv_s, wq_s):
    i = pl.program_id(0)

    @pl.when(i < N_STAGE)
    def _():
        _stage_store(i, wup_s, wup_c[0].astype(BF16))
        _stage_store(i, wdn_s, wdn_c[0].astype(BF16))
        _stage_store(i, wkv_s, wkv_c[0].astype(BF16))
        _stage_store(i, wq_s, _dot(wq_c[0].astype(BF16), perm_ref[...]).astype(BF16))

    @pl.when(i >= N_STAGE)
    def _():
        kv_shift = mod_ref[0, :, MOD_KV_BASE:MOD_KV_BASE + D_MODEL]
        kv_scale = mod_ref[0, :, MOD_KV_BASE + D_MODEL:MOD_KV_BASE + 2 * D_MODEL]
        h1 = _mlp_core(h_ref[0], mod_ref, 0, wup_s, wdn_s, lng_ref, lnb_ref)
        o_ref[0] = h1
        kv = _dot((h1 * (1.0 + kv_scale) + kv_shift).astype(BF16), wkv_s[...])
        uq = (h1 * (1.0 + _mod_slice(mod_ref, N_MOD + 1)) + _mod_slice(mod_ref, N_MOD)).astype(BF16)
        q = _dot(uq, wq_s[...])
        tables = _rope_tables(pos_ref[0], invf_ref, expand_ref)
        q_ref[0] = (_rope(q, tables) * (LOG2_E / math.sqrt(A_HEAD_DIM))).astype(BF16)
        k_ref[0] = _rope(kv[:, 0:A_KV_DIM], tables).astype(BF16)
        v_ref[0] = kv[:, A_KV_DIM:2 * A_KV_DIM].astype(BF16)


def _mlp1_kernel(h_ref, mod_ref, wup_c, wdn_c, lng_ref, lnb_ref, o_ref, wup_s, wdn_s):
    i = pl.program_id(0)

    @pl.when(i < N_STAGE)
    def _():
        _stage_store(i, wup_s, wup_c[0].astype(BF16))
        _stage_store(i, wdn_s, wdn_c[0].astype(BF16))

    @pl.when(i >= N_STAGE)
    def _():
        o_ref[0] = _mlp_core(h_ref[0], mod_ref, 1, wup_s, wdn_s, lng_ref, lnb_ref)


def _mlp0(h, mod, pos, mlp_w_up, mlp_w_down, w_kv, b_w_q, head_perm, ln_g, ln_b, inv_freq, expand):
    B, S, D = h.shape
    tm = ROW_TILE
    n_t = S // tm
    q_dim = A_HEADS * A_HEAD_DIM

    def pos_index(i):
        b, t = _tile_index(i, n_t)
        return b, 0, t

    return pl.pallas_call(
        _mlp0_kernel,
        grid=(N_STAGE + B * n_t,),
        in_specs=[
            _row_spec(tm, D, n_t),
            _mod_spec(n_t),
            pl.BlockSpec((1, 1, tm), pos_index),
            _stage_spec(mlp_w_up, 0),
            _stage_spec(mlp_w_down, 0),
            _stage_spec(w_kv, 0),
            _stage_spec(b_w_q, 0),
            _const_spec(head_perm.shape),
            _const_spec((1, D)),
            _const_spec((1, D)),
            _const_spec((ROPE_HALF, 1)),
            _const_spec(expand.shape),
        ],
        out_specs=[_row_spec(tm, D, n_t), _row_spec(tm, q_dim, n_t),
                   _row_spec(tm, A_KV_DIM, n_t), _row_spec(tm, A_KV_DIM, n_t)],
        out_shape=[
            jax.ShapeDtypeStruct((B, S, D), F32),
            jax.ShapeDtypeStruct((B, S, q_dim), BF16),
            jax.ShapeDtypeStruct((B, S, A_KV_DIM), BF16),
            jax.ShapeDtypeStruct((B, S, A_KV_DIM), BF16),
        ],
        scratch_shapes=[
            pltpu.VMEM((D, D_FF), BF16),
            pltpu.VMEM((D_FF, D), BF16),
            pltpu.VMEM((D, 2 * A_KV_DIM), BF16),
            pltpu.VMEM((D, q_dim), BF16),
        ],
        compiler_params=_compiler_params(),
        name="mlp0",
    )(h, mod, pos, mlp_w_up, mlp_w_down, w_kv, b_w_q, head_perm, ln_g, ln_b, inv_freq, expand)


def _mlp1(h, mod, mlp_w_up, mlp_w_down, ln_g, ln_b):
    B, S, D = h.shape
    tm = ROW_TILE
    n_t = S // tm
    return pl.pallas_call(
        _mlp1_kernel,
        grid=(N_STAGE + B * n_t,),
        in_specs=[
            _row_spec(tm, D, n_t),
            _mod_spec(n_t),
            _stage_spec(mlp_w_up, 1),
            _stage_spec(mlp_w_down, 1),
            _const_spec((1, D)),
            _const_spec((1, D)),
        ],
        out_specs=_row_spec(tm, D, n_t),
        out_shape=jax.ShapeDtypeStruct((B, S, D), F32),
        scratch_shapes=[pltpu.VMEM((D, D_FF), BF16), pltpu.VMEM((D_FF, D), BF16)],
        compiler_params=_compiler_params(),
        name="mlp1",
    )(h, mod, mlp_w_up, mlp_w_down, ln_g, ln_b)


HEAD_ORDER = tuple(8 * (G // 4) + 4 * half + (G % 4) for G in range(A_HEADS // 2) for half in range(2))
KV_PAIRS = A_KV_HEADS // 2


def _head_perm_matrix():
    p = np.zeros((A_HEADS * A_HEAD_DIM, A_HEADS * A_HEAD_DIM), np.float32)
    for new, old in enumerate(HEAD_ORDER):
        for d in range(A_HEAD_DIM):
            p[old * A_HEAD_DIM + d, new * A_HEAD_DIM + d] = 1.0
    return p


def _attn_pair_operands(k_band, v_band, p, lo_half, ones_bd):
    W = LANES_V7X
    zero = jnp.zeros((), BF16)
    kp = k_band[:, p * W:(p + 1) * W]
    vp = v_band[:, p * W:(p + 1) * W]
    k_bd = jnp.concatenate([jnp.where(lo_half, kp, zero), jnp.where(lo_half, zero, kp)], axis=0)
    v_bd = jnp.concatenate([jnp.where(lo_half, vp, zero), jnp.where(lo_half, zero, vp)], axis=0)
    return k_bd, jnp.concatenate([v_bd, ones_bd], axis=1)


def _attn_softmax_pv(s_all, v_ext, p, bias, sink_ref, lo_half_q):
    L = WINDOW
    W = LANES_V7X
    e_rows, sink_rows = [], []
    for j in range(A_GROUP):
        halves, sink_terms = [], []
        for half in range(2):
            s = s_all[j * L:(j + 1) * L, half * 2 * L:(half + 1) * 2 * L] + bias
            sink = sink_ref[2 * (4 * p + j) + half] * LOG2_E
            mx = jnp.maximum(jnp.max(s, axis=1, keepdims=True), sink)
            halves.append(jnp.exp2(s - mx).astype(BF16))
            sink_terms.append(jnp.exp2(sink - mx))
        e_rows.append(jnp.concatenate(halves, axis=1))
        sink_rows.append(jnp.where(lo_half_q, sink_terms[0], sink_terms[1]))
    o_ext = _dot(jnp.concatenate(e_rows, axis=0), v_ext)
    o_all = o_ext[:, 0:W] / (o_ext[:, W:2 * W] + jnp.concatenate(sink_rows, axis=0))
    return [o_all[j * L:(j + 1) * L, :].astype(BF16) for j in range(A_GROUP)]


def _attn_tile(tiles_per_seq, sink_ref, q_ref, k_ref, kprev_ref, v_ref, vprev_ref, h_ref, mod_ref,
               lng_ref, lnb_ref, o_ref, wo_s, att_s):
    L = WINDOW
    W = LANES_V7X
    tq = q_ref.shape[1]
    qi = lax.broadcasted_iota(jnp.int32, (L, 2 * L), 0)
    sj = lax.broadcasted_iota(jnp.int32, (L, 2 * L), 1)
    in_window = (sj > qi) & (sj <= qi + L)
    bias = jnp.where(in_window, 0.0, -jnp.inf)
    seq_start = (pl.program_id(0) - N_STAGE) % tiles_per_seq == 0
    first_bias = jnp.where(in_window & ((sj >= L) | jnp.logical_not(seq_start)), 0.0, -jnp.inf)
    lo_half = lax.broadcasted_iota(jnp.int32, (2 * L, W), 1) < A_HEAD_DIM
    lo_half_q = lax.broadcasted_iota(jnp.int32, (L, W), 1) < A_HEAD_DIM
    ones_bd = jnp.concatenate([jnp.where(lo_half, 1.0, 0.0), jnp.where(lo_half, 0.0, 1.0)], axis=0).astype(BF16)

    for blk in range(tq // L):
        r0 = blk * L
        if blk == 0:
            k_prev, v_prev, blk_bias = kprev_ref[0], vprev_ref[0], first_bias
        else:
            k_prev, v_prev, blk_bias = k_ref[0, r0 - L:r0, :], v_ref[0, r0 - L:r0, :], bias
        k_band = jnp.concatenate([k_prev, k_ref[0, r0:r0 + L, :]], axis=0)
        v_band = jnp.concatenate([v_prev, v_ref[0, r0:r0 + L, :]], axis=0)
        for p in range(KV_PAIRS):
            k_bd, v_ext = _attn_pair_operands(k_band, v_band, p, lo_half, ones_bd)
            lhs = jnp.concatenate(
                [q_ref[0, r0:r0 + L, (4 * p + j) * W:(4 * p + j + 1) * W] for j in range(A_GROUP)], axis=0)
            outs = _attn_softmax_pv(_dot_nt(lhs, k_bd), v_ext, p, blk_bias, sink_ref, lo_half_q)
            for j in range(A_GROUP):
                att_s[r0:r0 + L, (4 * p + j) * W:(4 * p + j + 1) * W] = outs[j]

    y = _dot(att_s[...], wo_s[...])
    r = DEEPNORM_ALPHA * h_ref[0] + (1.0 + _mod_slice(mod_ref, N_MOD + 2)) * y
    o_ref[0] = _layer_norm(r, lng_ref[...], lnb_ref[...])


def _attn_kernel(tiles_per_seq, sink_ref, q_ref, k_ref, kprev_ref, v_ref, vprev_ref, h_ref, mod_ref,
                 wo_lo_c, wo_hi_c, lng_ref, lnb_ref, o_ref, wo_s, att_s):
    i = pl.program_id(0)

    @pl.when(i < N_STAGE)
    def _():
        r = pl.multiple_of(i * LANES_V7X, LANES_V7X)
        wo_s[pl.ds(r, A_HEAD_DIM), :] = wo_lo_c[0].astype(BF16)
        wo_s[pl.ds(pl.multiple_of(r + A_HEAD_DIM, A_HEAD_DIM), A_HEAD_DIM), :] = wo_hi_c[0].astype(BF16)

    @pl.when(i >= N_STAGE)
    def _():
        _attn_tile(tiles_per_seq, sink_ref, q_ref, k_ref, kprev_ref, v_ref, vprev_ref, h_ref, mod_ref,
                   lng_ref, lnb_ref, o_ref, wo_s, att_s)


def _attn(sinks, q, k, v, h, mod, b_w_o, ln_g, ln_b):
    B, S, D = h.shape
    tq = ROW_TILE
    n_t = S // tq
    q_dim = A_HEADS * A_HEAD_DIM
    blocks_per_tile = tq // WINDOW
    assert N_STAGE == A_HEADS // 2

    def prev_index(i):
        b, t = _tile_index(i, n_t)
        return b, jnp.maximum(t * blocks_per_tile - 1, 0), 0

    def wo_index(half):
        def index(i):
            g = jnp.minimum(i, N_STAGE - 1)
            return 0, 8 * (g // 4) + 4 * half + g % 4, 0
        return index

    return pl.pallas_call(
        functools.partial(_attn_kernel, n_t),
        grid=(N_STAGE + B * n_t,),
        in_specs=[
            pl.BlockSpec(memory_space=pltpu.SMEM),
            _row_spec(tq, q_dim, n_t),
            _row_spec(tq, A_KV_DIM, n_t),
            pl.BlockSpec((1, WINDOW, A_KV_DIM), prev_index),
            _row_spec(tq, A_KV_DIM, n_t),
            pl.BlockSpec((1, WINDOW, A_KV_DIM), prev_index),
            _row_spec(tq, D, n_t),
            _mod_spec(n_t),
            pl.BlockSpec((1, A_HEAD_DIM, D), wo_index(0)),
            pl.BlockSpec((1, A_HEAD_DIM, D), wo_index(1)),
            _const_spec((1, D)),
            _const_spec((1, D)),
        ],
        out_specs=_row_spec(tq, D, n_t),
        out_shape=jax.ShapeDtypeStruct((B, S, D), F32),
        scratch_shapes=[pltpu.VMEM((q_dim, D), BF16), pltpu.VMEM((tq, q_dim), BF16)],
        compiler_params=_compiler_params(),
        name="attn",
    )(sinks, q, k, k, v, v, h, mod, b_w_o, b_w_o, ln_g, ln_b)


def _rope_expand_matrix():
    e = np.zeros((2 * ROPE_HALF, 3 * LANES_V7X), np.float32)
    for lane in range(LANES_V7X):
        d = lane % A_HEAD_DIM
        if d < ROPE_HALF:
            e[d, lane] = 1.0
            e[ROPE_HALF + d, 2 * LANES_V7X + lane] = -1.0
        elif d < ROPE_DIM:
            e[d - ROPE_HALF, lane] = 1.0
            e[d, LANES_V7X + lane] = 1.0
    return np.tile(e, (3, 1))


def kernel(x, c, positions, ada_w, ada_b, kv_ada_w, kv_ada_b, a_w_in, a_b_gates, a_norm_w, a_w_out,
           w_kv, b_w_q, b_sinks, b_w_o, mlp_w_up, mlp_w_down, ln_g, ln_b):
    B, S, D = x.shape
    assert D == D_MODEL and S % ROW_TILE == 0 and ROW_TILE % M_CHUNK == 0

    mod = _adaln(c, ada_w, ada_b, kv_ada_w, kv_ada_b).reshape(B, 1, MOD_WIDTH)

    w_gate = jnp.pad(a_w_in[0, :, N_MAIN:], ((0, 0), (0, GATE_PAD - 2 * M_HEADS))).astype(BF16)
    h = _mix0(x, mod, a_w_in, a_w_out, w_gate, a_b_gates[0].reshape(2 * M_HEADS, 1),
              jnp.asarray(_gate_expand_matrix(), BF16), a_norm_w[0].reshape(1, D),
              ln_g[0].reshape(1, D), ln_b[0].reshape(1, D))

    inv_freq = (ROPE_THETA ** (-jnp.arange(ROPE_HALF, dtype=F32) / ROPE_HALF)).reshape(ROPE_HALF, 1)
    h, q, k, v = _mlp0(h, mod, positions.reshape(B, 1, S), mlp_w_up, mlp_w_down, w_kv[None], b_w_q,
                       jnp.asarray(_head_perm_matrix(), BF16), ln_g[1].reshape(1, D), ln_b[1].reshape(1, D),
                       inv_freq, jnp.asarray(_rope_expand_matrix(), BF16))

    h = _attn(b_sinks[0][np.asarray(HEAD_ORDER)], q, k, v, h, mod, b_w_o,
              ln_g[2].reshape(1, D), ln_b[2].reshape(1, D))

    return _mlp1(h, mod, mlp_w_up, mlp_w_down, ln_g[3].reshape(1, D), ln_b[3].reshape(1, D))
```

```python
import functools
import math

import numpy as np
import jax
import jax.numpy as jnp
from jax import lax
from jax.experimental import pallas as pl
from jax.experimental.pallas import tpu as pltpu

F32 = jnp.float32
BF16 = jnp.bfloat16

D_MODEL = 1024
DEPTH = 2
M_HEADS = 4
M_V_DIM = D_MODEL // M_HEADS
M_QK_DIM = M_V_DIM // 2
M_CHUNK = 128
GATE_CAP = 15.0
A_HEADS = 16
A_KV_HEADS = 4
A_GROUP = A_HEADS // A_KV_HEADS
A_HEAD_DIM = 64
A_KV_DIM = A_KV_HEADS * A_HEAD_DIM
WINDOW = 128
ROPE_DIM = A_HEAD_DIM // 4
ROPE_HALF = ROPE_DIM // 2
ROPE_THETA = 500000.0
D_FF = 4 * D_MODEL
DEEPNORM_ALPHA = (2 * DEPTH) ** 0.25
LN_EPS = 1e-5
RMS_EPS = 1e-6
LOG2_E = math.log2(math.e)
N_MOD = 6
MOD_KV_BASE = DEPTH * N_MOD * D_MODEL
MOD_WIDTH = MOD_KV_BASE + 2 * D_MODEL

LANES_V7X = 128
SUBLANES_V7X = 8
VMEM_LIMIT_BYTES_V7X = 56 * 1024 * 1024

ROW_TILE = 512
ADALN_COL_TILE = 1024
N_STAGE = 8

def _compiler_params():
    return pltpu.CompilerParams(dimension_semantics=("arbitrary",), vmem_limit_bytes=VMEM_LIMIT_BYTES_V7X)


def _const_spec(shape):
    return pl.BlockSpec(shape, lambda *_: (0,) * len(shape))


def _tile_index(i, tiles):
    j = jnp.maximum(i - N_STAGE, 0)
    return j // tiles[1], j % tiles[1]


def _row_spec(tm, width, tiles):
    def index(i):
        b, t = _tile_index(i, tiles)
        return b, t, 0
    return pl.BlockSpec((1, tm, width), index)


def _mod_spec(tiles):
    return pl.BlockSpec((1, 1, MOD_WIDTH), lambda i: (_tile_index(i, tiles)[0], 0, 0))


def _stage_spec(w, layer):
    _, rows, cols = w.shape
    return pl.BlockSpec((1, rows // N_STAGE, cols), lambda i: (layer, jnp.minimum(i, N_STAGE - 1), 0))


def _stage_store(i, dst_ref, chunk):
    r = chunk.shape[0]
    dst_ref[pl.ds(pl.multiple_of(i * r, r), r), :] = chunk


def _layer_norm(r, g, b):
    mu = jnp.mean(r, axis=-1, keepdims=True)
    d = r - mu
    var = jnp.mean(d * d, axis=-1, keepdims=True)
    return d * lax.rsqrt(var + LN_EPS) * g + b


def _mod_slice(mod_ref, idx):
    return mod_ref[0, :, idx * D_MODEL:(idx + 1) * D_MODEL]


def _dot(a, b):
    return jnp.dot(a, b, preferred_element_type=F32)


def _dot_nt(a, b):
    return lax.dot_general(a, b, (((1,), (1,)), ((), ())), preferred_element_type=F32)


def _dot_tn(a, b):
    return lax.dot_general(a, b, (((0,), (0,)), ((), ())), preferred_element_type=F32)


def _split3_bf16(x):
    hi = x.astype(BF16)
    r1 = x - hi.astype(F32)
    mid = r1.astype(BF16)
    lo = (r1 - mid.astype(F32)).astype(BF16)
    return hi, mid, lo


N_ADA_TILES = DEPTH * N_MOD * D_MODEL // ADALN_COL_TILE
N_KV_TILES = 2 * D_MODEL // ADALN_COL_TILE


def _adaln_kernel(c_ref, wa_ref, wk_ref, ba_ref, bk_ref, o_ref):
    i = pl.program_id(0)
    c = c_ref[...]
    cs = (c * jax.nn.sigmoid(c)).astype(BF16)

    @pl.when(i < N_ADA_TILES)
    def _():
        o_ref[...] = _dot(cs, wa_ref[0].astype(BF16)) + ba_ref[0]

    @pl.when(i >= N_ADA_TILES)
    def _():
        o_ref[...] = _dot(cs, wk_ref[...].astype(BF16)) + bk_ref[0]


def _adaln(c, ada_w, ada_b, kv_ada_w, kv_ada_b):
    batch = c.shape[0]
    tiles_per_layer = N_MOD * D_MODEL // ADALN_COL_TILE
    ba = ada_b.reshape(N_ADA_TILES, 1, ADALN_COL_TILE)
    bk = kv_ada_b.reshape(N_KV_TILES, 1, ADALN_COL_TILE)

    def ada_idx(i):
        return jnp.minimum(i, N_ADA_TILES - 1)

    def kv_idx(i):
        return jnp.maximum(i - N_ADA_TILES, 0)

    return pl.pallas_call(
        _adaln_kernel,
        grid=(N_ADA_TILES + N_KV_TILES,),
        in_specs=[
            _const_spec((batch, D_MODEL)),
            pl.BlockSpec((1, D_MODEL, ADALN_COL_TILE),
                         lambda i: (ada_idx(i) // tiles_per_layer, 0, ada_idx(i) % tiles_per_layer)),
            pl.BlockSpec((D_MODEL, ADALN_COL_TILE), lambda i: (0, kv_idx(i))),
            pl.BlockSpec((1, 1, ADALN_COL_TILE), lambda i: (ada_idx(i), 0, 0)),
            pl.BlockSpec((1, 1, ADALN_COL_TILE), lambda i: (kv_idx(i), 0, 0)),
        ],
        out_specs=pl.BlockSpec((batch, ADALN_COL_TILE), lambda i: (0, i)),
        out_shape=jax.ShapeDtypeStruct((batch, MOD_WIDTH), F32),
        compiler_params=_compiler_params(),
        name="adaln",
    )(c, ada_w, kv_ada_w, ba, bk)


NQ = M_HEADS * M_QK_DIM
N_MAIN = 2 * NQ + 2 * D_MODEL
GATE_PAD = LANES_V7X
GATE_QUANTS = 3
GATE_PART_ROWS = 32
GATE_EXPAND_COLS = GATE_QUANTS * M_HEADS * LANES_V7X


def _gate_expand_matrix():
    e = np.zeros((GATE_PART_ROWS, GATE_EXPAND_COLS), np.float32)
    for quant in range(GATE_QUANTS):
        for h in range(M_HEADS):
            row = quant * SUBLANES_V7X + M_HEADS + h
            grp = quant * M_HEADS + h
            e[row, grp * LANES_V7X:(grp + 1) * LANES_V7X] = 1.0
    return np.tile(e, (3, 1))


def _segment_scan(x, op, identity):
    pos = lax.broadcasted_iota(jnp.int32, x.shape, 1) % M_CHUNK
    shift = 1
    while shift < M_CHUNK:
        x = op(x, jnp.where(pos >= shift, pltpu.roll(x, shift, axis=1), identity))
        shift *= 2
    return x


def _gate_scan(g_tm, bg_ref):
    L = M_CHUNK
    nc = g_tm.shape[0] // L
    gates_t = jnp.concatenate(
        [g_tm[c * L:(c + 1) * L, :].T[0:2 * M_HEADS, :] for c in range(nc)], axis=1) + bg_ref[...]
    capped = GATE_CAP * jnp.tanh(gates_t / GATE_CAP)
    log_f = jnp.minimum(capped, 0.0) - jnp.log1p(jnp.exp(-jnp.abs(capped)))
    bcum = _segment_scan(log_f * LOG2_E, jnp.add, 0.0)
    a = pltpu.roll(capped * LOG2_E, M_HEADS, axis=0) - bcum
    cmax = _segment_scan(a, jnp.maximum, -jnp.inf)
    stacked = jnp.concatenate([bcum, cmax, a, jnp.zeros_like(a)], axis=0)
    return a, jnp.concatenate(_split3_bf16(stacked), axis=0)


def _gate_expand(parts, gexp_ref):
    L = M_CHUNK
    return [_dot_tn(parts[:, c * L:(c + 1) * L], gexp_ref[...]) for c in range(parts.shape[1] // L)]


def _mlstm_tile(a_rows, cols, qk_s, v_s, h_s, c_s, n_s, m_s):
    L = M_CHUNK
    W = LANES_V7X
    nc = len(cols)
    heads = range(M_HEADS)
    causal = lax.broadcasted_iota(jnp.int32, (L, L), 1) <= lax.broadcasted_iota(jnp.int32, (L, L), 0)
    ones = jnp.ones((L, W), BF16)

    def rows(c):
        return slice(c * L, (c + 1) * L)

    def q_of(h, c):
        return qk_s[rows(c), h * M_QK_DIM:(h + 1) * M_QK_DIM]

    def k_of(h, c):
        return qk_s[rows(c), NQ + h * M_QK_DIM:NQ + (h + 1) * M_QK_DIM]

    def v1_of(h, c):
        return jnp.concatenate([v_s[rows(c), h * M_V_DIM:(h + 1) * M_V_DIM], ones], axis=1)

    def col(c, quant, h):
        g = quant * M_HEADS + h
        return cols[c][:, g * W:(g + 1) * W]

    scores = [[_dot_nt(q_of(h, c), k_of(h, c)) for c in range(nc)] for h in heads]

    m_in = [[None] * nc for _ in heads]
    mx = [[None] * nc for _ in heads]
    for h in heads:
        m_old = m_s[h:h + 1, :]
        for c in range(nc):
            m_in[h][c] = m_old
            mx[h][c] = jnp.maximum(m_old, col(c, 1, h)[L - 1:L, :])
            m_old = col(c, 0, h)[L - 1:L, :] + mx[h][c]
        m_s[h:h + 1, :] = m_old

    deltas = [[_dot_tn((k_of(h, c).astype(F32) * jnp.exp2(col(c, 2, h) - mx[h][c])).astype(BF16), v1_of(h, c))
               for c in range(nc)] for h in heads]

    for h in heads:
        c_t = c_s[h]
        n_bc = n_s[h]
        for c in range(nc):
            m_old = m_in[h][c]
            a_row = a_rows[M_HEADS + h:M_HEADS + h + 1, rows(c)]
            mt = jnp.maximum(col(c, 1, h), m_old)
            w_intra = jnp.exp2(jnp.where(causal, a_row - mt, -jnp.inf))
            w_inter = jnp.exp2(m_old - mt)
            e_neg = jnp.exp2(-col(c, 0, h) - mt)
            s_qk = scores[h][c] * w_intra
            lhs = jnp.concatenate([(q_of(h, c).astype(F32) * w_inter).astype(BF16), s_qk.astype(BF16)], axis=1)
            state = jnp.concatenate([c_t.astype(BF16), n_bc.astype(BF16)], axis=1)
            out = _dot(lhs, jnp.concatenate([state, v1_of(h, c)], axis=0))
            r_den = 1.0 / jnp.maximum(jnp.abs(out[:, M_V_DIM:M_V_DIM + W]), e_neg)
            for half in range(M_V_DIM // W):
                h_s[rows(c), h * M_V_DIM + half * W:h * M_V_DIM + (half + 1) * W] = (
                    out[:, half * W:(half + 1) * W] * r_den)
            decay = jnp.exp2(m_old - mx[h][c])
            c_t = jnp.concatenate([decay] * (M_V_DIM // W), axis=1) * c_t + deltas[h][c][:, 0:M_V_DIM]
            n_bc = decay * n_bc + deltas[h][c][:, M_V_DIM:M_V_DIM + W]
        c_s[h] = c_t
        n_s[h] = n_bc


def _mix0_tile(tiles_per_seq, x_ref, mod_ref, wg_ref, bg_ref, gexp_ref, nw_ref, lng_ref, lnb_ref,
               o_ref, win_s, wout_s, qk_s, v_s, og_s, h_s, c_s, n_s, m_s):
    @pl.when((pl.program_id(0) - N_STAGE) % tiles_per_seq == 0)
    def _():
        c_s[...] = jnp.zeros_like(c_s)
        n_s[...] = jnp.zeros_like(n_s)
        m_s[...] = jnp.zeros_like(m_s)

    x = x_ref[0]
    u = (x * (1.0 + _mod_slice(mod_ref, 1)) + _mod_slice(mod_ref, 0)).astype(BF16)
    a_rows, parts = _gate_scan(_dot(u, wg_ref[...]), bg_ref)
    qk_s[:, 0:NQ] = _dot(u, win_s[:, 0:NQ]).astype(BF16)
    qk_s[:, NQ:2 * NQ] = (_dot(u, win_s[:, NQ:2 * NQ]) * (1.0 / math.sqrt(M_QK_DIM))).astype(BF16)
    v_s[...] = _dot(u, win_s[:, 2 * NQ:2 * NQ + D_MODEL]).astype(BF16)
    og_s[...] = _dot(u, win_s[:, 2 * NQ + D_MODEL:N_MAIN])

    _mlstm_tile(a_rows, _gate_expand(parts, gexp_ref), qk_s, v_s, h_s, c_s, n_s, m_s)

    normed = []
    for h in range(M_HEADS):
        hh = h_s[:, h * M_V_DIM:(h + 1) * M_V_DIM]
        normed.append(hh * lax.rsqrt(jnp.mean(hh * hh, axis=-1, keepdims=True) + RMS_EPS))
    hn = jnp.concatenate(normed, axis=1)
    gated = hn * nw_ref[...] * jax.nn.sigmoid(og_s[...])
    y = _dot(gated.astype(BF16), wout_s[...])
    r = DEEPNORM_ALPHA * x + (1.0 + _mod_slice(mod_ref, 2)) * y
    o_ref[0] = _layer_norm(r, lng_ref[...], lnb_ref[...])


def _mix0_kernel(tiles_per_seq, x_ref, mod_ref, win_c, wout_c, wg_ref, bg_ref, gexp_ref, nw_ref,
                 lng_ref, lnb_ref, o_ref, win_s, wout_s, *scratch):
    i = pl.program_id(0)

    @pl.when(i < N_STAGE)
    def _():
        _stage_store(i, win_s, win_c[0].astype(BF16))
        _stage_store(i, wout_s, wout_c[0].astype(BF16))

    @pl.when(i >= N_STAGE)
    def _():
        _mix0_tile(tiles_per_seq, x_ref, mod_ref, wg_ref, bg_ref, gexp_ref, nw_ref, lng_ref, lnb_ref,
                   o_ref, win_s, wout_s, *scratch)


def _mix0(x, mod, a_w_in, a_w_out, w_gate, b_gates, gate_expand, norm_w, ln_g, ln_b):
    B, S, D = x.shape
    tm = ROW_TILE
    n_t = S // tm
    tiles = (B, n_t)
    return pl.pallas_call(
        functools.partial(_mix0_kernel, n_t),
        grid=(N_STAGE + B * n_t,),
        in_specs=[
            _row_spec(tm, D, tiles),
            _mod_spec(tiles),
            _stage_spec(a_w_in, 0),
            _stage_spec(a_w_out, 0),
            _const_spec((D, GATE_PAD)),
            _const_spec((2 * M_HEADS, 1)),
            _const_spec(gate_expand.shape),
            _const_spec((1, D)),
            _const_spec((1, D)),
            _const_spec((1, D)),
        ],
        out_specs=_row_spec(tm, D, tiles),
        out_shape=jax.ShapeDtypeStruct((B, S, D), F32),
        scratch_shapes=[
            pltpu.VMEM((D, N_MAIN), BF16),
            pltpu.VMEM((D, D), BF16),
            pltpu.VMEM((tm, 2 * NQ), BF16),
            pltpu.VMEM((tm, D_MODEL), BF16),
            pltpu.VMEM((tm, D_MODEL), F32),
            pltpu.VMEM((tm, D_MODEL), F32),
            pltpu.VMEM((M_HEADS, M_QK_DIM, M_V_DIM), F32),
            pltpu.VMEM((M_HEADS, M_QK_DIM, LANES_V7X), F32),
            pltpu.VMEM((SUBLANES_V7X, LANES_V7X), F32),
        ],
        compiler_params=_compiler_params(),
        name="mix0",
    )(x, mod, a_w_in, a_w_out, w_gate, b_gates, gate_expand, norm_w, ln_g, ln_b)


def _mlp_core(h, mod_ref, layer, wup_s, wdn_s, lng_ref, lnb_ref):
    base = layer * N_MOD
    u = (h * (1.0 + _mod_slice(mod_ref, base + 4)) + _mod_slice(mod_ref, base + 3)).astype(BF16)
    a = jnp.maximum(_dot(u, wup_s[...]), 0.0)
    y = _dot((a * a).astype(BF16), wdn_s[...])
    r = DEEPNORM_ALPHA * h + (1.0 + _mod_slice(mod_ref, base + 5)) * y
    return _layer_norm(r, lng_ref[...], lnb_ref[...])


def _rope_tables(pos_row, invf_ref, expand_ref):
    ang = pos_row.astype(F32) * invf_ref[...]
    trig = jnp.concatenate([jnp.cos(ang), jnp.sin(ang)], axis=0)
    parts = jnp.concatenate(_split3_bf16(trig), axis=0)
    tab = _dot_tn(parts, expand_ref[...])
    lane = lax.broadcasted_iota(jnp.int32, (1, LANES_V7X), 1)
    cos_t = tab[:, 0:LANES_V7X] + jnp.where(lane % A_HEAD_DIM >= ROPE_DIM, 1.0, 0.0)
    return cos_t, tab[:, LANES_V7X:2 * LANES_V7X], tab[:, 2 * LANES_V7X:3 * LANES_V7X]


def _rope(x, tables):
    cos_t, sin_up, sin_dn = tables
    out = []
    for g in range(x.shape[1] // LANES_V7X):
        xg = x[:, g * LANES_V7X:(g + 1) * LANES_V7X]
        from_lo = pltpu.roll(xg, ROPE_HALF, axis=1)
        from_hi = pltpu.roll(xg, LANES_V7X - ROPE_HALF, axis=1)
        out.append(xg * cos_t + from_lo * sin_up + from_hi * sin_dn)
    return jnp.concatenate(out, axis=1)


def _mlp0_kernel(h_ref, mod_ref, pos_ref, wup_c, wdn_c, wkv_c, wq_c, perm_ref, lng_ref, lnb_ref,
                 invf_ref, expand_ref, o_ref, q_ref, k_ref, v_ref, wup_s, wdn_s, wkv_s, wq_s):
    i = pl.program_id(0)

    @pl.when(i < N_STAGE)
    def _():
        _stage_store(i, wup_s, wup_c[0].astype(BF16))
        _stage_store(i, wdn_s, wdn_c[0].astype(BF16))
        _stage_store(i, wkv_s, wkv_c[0].astype(BF16))
        _stage_store(i, wq_s, _dot(wq_c[0].astype(BF16), perm_ref[...]).astype(BF16))

    @pl.when(i >= N_STAGE)
    def _():
        kv_shift = mod_ref[0, :, MOD_KV_BASE:MOD_KV_BASE + D_MODEL]
        kv_scale = mod_ref[0, :, MOD_KV_BASE + D_MODEL:MOD_KV_BASE + 2 * D_MODEL]
        h1 = _mlp_core(h_ref[0], mod_ref, 0, wup_s, wdn_s, lng_ref, lnb_ref)
        o_ref[0] = h1
        kv = _dot((h1 * (1.0 + kv_scale) + kv_shift).astype(BF16), wkv_s[...])
        uq = (h1 * (1.0 + _mod_slice(mod_ref, N_MOD + 1)) + _mod_slice(mod_ref, N_MOD)).astype(BF16)
        q = _dot(uq, wq_s[...])
        tables = _rope_tables(pos_ref[0], invf_ref, expand_ref)
        q_ref[0] = (_rope(q, tables) * (LOG2_E / math.sqrt(A_HEAD_DIM))).astype(BF16)
        k_ref[0] = _rope(kv[:, 0:A_KV_DIM], tables).astype(BF16)
        v_ref[0] = kv[:, A_KV_DIM:2 * A_KV_DIM].astype(BF16)


def _mlp1_kernel(h_ref, mod_ref, wup_c, wdn_c, lng_ref, lnb_ref, o_ref, wup_s, wdn_s):
    i = pl.program_id(0)

    @pl.when(i < N_STAGE)
    def _():
        _stage_store(i, wup_s, wup_c[0].astype(BF16))
        _stage_store(i, wdn_s, wdn_c[0].astype(BF16))

    @pl.when(i >= N_STAGE)
    def _():
        o_ref[0] = _mlp_core(h_ref[0], mod_ref, 1, wup_s, wdn_s, lng_ref, lnb_ref)


def _mlp0(h, mod, pos, mlp_w_up, mlp_w_down, w_kv, b_w_q, head_perm, ln_g, ln_b, inv_freq, expand):
    B, S, D = h.shape
    tm = ROW_TILE
    n_t = S // tm
    tiles = (B, n_t)
    q_dim = A_HEADS * A_HEAD_DIM

    def pos_index(i):
        b, t = _tile_index(i, tiles)
        return b, 0, t

    return pl.pallas_call(
        _mlp0_kernel,
        grid=(N_STAGE + B * n_t,),
        in_specs=[
            _row_spec(tm, D, tiles),
            _mod_spec(tiles),
            pl.BlockSpec((1, 1, tm), pos_index),
            _stage_spec(mlp_w_up, 0),
            _stage_spec(mlp_w_down, 0),
            _stage_spec(w_kv, 0),
            _stage_spec(b_w_q, 0),
            _const_spec(head_perm.shape),
            _const_spec((1, D)),
            _const_spec((1, D)),
            _const_spec((ROPE_HALF, 1)),
            _const_spec(expand.shape),
        ],
        out_specs=[_row_spec(tm, D, tiles), _row_spec(tm, q_dim, tiles),
                   _row_spec(tm, A_KV_DIM, tiles), _row_spec(tm, A_KV_DIM, tiles)],
        out_shape=[
            jax.ShapeDtypeStruct((B, S, D), F32),
            jax.ShapeDtypeStruct((B, S, q_dim), BF16),
            jax.ShapeDtypeStruct((B, S, A_KV_DIM), BF16),
            jax.ShapeDtypeStruct((B, S, A_KV_DIM), BF16),
        ],
        scratch_shapes=[
            pltpu.VMEM((D, D_FF), BF16),
            pltpu.VMEM((D_FF, D), BF16),
            pltpu.VMEM((D, 2 * A_KV_DIM), BF16),
            pltpu.VMEM((D, q_dim), BF16),
        ],
        compiler_params=_compiler_params(),
        name="mlp0",
    )(h, mod, pos, mlp_w_up, mlp_w_down, w_kv, b_w_q, head_perm, ln_g, ln_b, inv_freq, expand)


def _mlp1(h, mod, mlp_w_up, mlp_w_down, ln_g, ln_b):
    B, S, D = h.shape
    tm = ROW_TILE
    n_t = S // tm
    tiles = (B, n_t)
    return pl.pallas_call(
        _mlp1_kernel,
        grid=(N_STAGE + B * n_t,),
        in_specs=[
            _row_spec(tm, D, tiles),
            _mod_spec(tiles),
            _stage_spec(mlp_w_up, 1),
            _stage_spec(mlp_w_down, 1),
            _const_spec((1, D)),
            _const_spec((1, D)),
        ],
        out_specs=_row_spec(tm, D, tiles),
        out_shape=jax.ShapeDtypeStruct((B, S, D), F32),
        scratch_shapes=[pltpu.VMEM((D, D_FF), BF16), pltpu.VMEM((D_FF, D), BF16)],
        compiler_params=_compiler_params(),
        name="mlp1",
    )(h, mod, mlp_w_up, mlp_w_down, ln_g, ln_b)


HEAD_ORDER = tuple(8 * (G // 4) + 4 * half + (G % 4) for G in range(A_HEADS // 2) for half in range(2))
KV_PAIRS = A_KV_HEADS // 2


def _head_perm_matrix():
    p = np.zeros((A_HEADS * A_HEAD_DIM, A_HEADS * A_HEAD_DIM), np.float32)
    for new, old in enumerate(HEAD_ORDER):
        for d in range(A_HEAD_DIM):
            p[old * A_HEAD_DIM + d, new * A_HEAD_DIM + d] = 1.0
    return p


def _attn_pair_operands(k_band, v_band, p, lo_half, ones_bd):
    W = LANES_V7X
    zero = jnp.zeros((), BF16)
    kp = k_band[:, p * W:(p + 1) * W]
    vp = v_band[:, p * W:(p + 1) * W]
    k_bd = jnp.concatenate([jnp.where(lo_half, kp, zero), jnp.where(lo_half, zero, kp)], axis=0)
    v_bd = jnp.concatenate([jnp.where(lo_half, vp, zero), jnp.where(lo_half, zero, vp)], axis=0)
    return k_bd, jnp.concatenate([v_bd, ones_bd], axis=1)


def _attn_softmax_pv(s_all, v_ext, p, start_bias, sink_ref, prev_visible, lo_half_q):
    L = WINDOW
    W = LANES_V7X
    zero = jnp.zeros((), BF16)
    e_rows, sink_rows = [], []
    for j in range(A_GROUP):
        halves, sink_terms = [], []
        for half in range(2):
            c0 = half * 2 * L
            s = jnp.where(prev_visible, s_all[j * L:(j + 1) * L, c0:c0 + L], s_all[j * L:(j + 1) * L, c0 + L:c0 + 2 * L])
            if start_bias is not None:
                s = s + start_bias
            sink = sink_ref[2 * (4 * p + j) + half] * LOG2_E
            mx = jnp.maximum(jnp.max(s, axis=1, keepdims=True), sink)
            e = jnp.exp2(s - mx).astype(BF16)
            halves += [jnp.where(prev_visible, e, zero), jnp.where(prev_visible, zero, e)]
            sink_terms.append(jnp.exp2(sink - mx))
        e_rows.append(jnp.concatenate(halves, axis=1))
        sink_rows.append(jnp.where(lo_half_q, sink_terms[0], sink_terms[1]))
    o_ext = _dot(jnp.concatenate(e_rows, axis=0), v_ext)
    o_all = o_ext[:, 0:W] / (o_ext[:, W:2 * W] + jnp.concatenate(sink_rows, axis=0))
    return [o_all[j * L:(j + 1) * L, :].astype(BF16) for j in range(A_GROUP)]


def _attn_tile(tiles_per_seq, sink_ref, q_ref, k_ref, kprev_ref, v_ref, vprev_ref, h_ref, mod_ref,
               lng_ref, lnb_ref, o_ref, wo_s, att_s):
    L = WINDOW
    W = LANES_V7X
    tq = q_ref.shape[1]
    prev_visible = lax.broadcasted_iota(jnp.int32, (L, L), 1) > lax.broadcasted_iota(jnp.int32, (L, L), 0)
    seq_start = (pl.program_id(0) - N_STAGE) % tiles_per_seq == 0
    start_bias = jnp.where(prev_visible & seq_start, -jnp.inf, 0.0)
    lo_half = lax.broadcasted_iota(jnp.int32, (2 * L, W), 1) < A_HEAD_DIM
    lo_half_q = lax.broadcasted_iota(jnp.int32, (L, W), 1) < A_HEAD_DIM
    ones_bd = jnp.concatenate([jnp.where(lo_half, 1.0, 0.0), jnp.where(lo_half, 0.0, 1.0)], axis=0).astype(BF16)

    for blk in range(tq // L):
        r0 = blk * L
        if blk == 0:
            k_prev, v_prev, blk_bias = kprev_ref[0], vprev_ref[0], start_bias
        else:
            k_prev, v_prev, blk_bias = k_ref[0, r0 - L:r0, :], v_ref[0, r0 - L:r0, :], None
        k_band = jnp.concatenate([k_prev, k_ref[0, r0:r0 + L, :]], axis=0)
        v_band = jnp.concatenate([v_prev, v_ref[0, r0:r0 + L, :]], axis=0)
        for p in range(KV_PAIRS):
            k_bd, v_ext = _attn_pair_operands(k_band, v_band, p, lo_half, ones_bd)
            lhs = jnp.concatenate(
                [q_ref[0, r0:r0 + L, (4 * p + j) * W:(4 * p + j + 1) * W] for j in range(A_GROUP)], axis=0)
            outs = _attn_softmax_pv(_dot_nt(lhs, k_bd), v_ext, p, blk_bias, sink_ref, prev_visible, lo_half_q)
            for j in range(A_GROUP):
                att_s[r0:r0 + L, (4 * p + j) * W:(4 * p + j + 1) * W] = outs[j]

    y = _dot(att_s[...], wo_s[...])
    r = DEEPNORM_ALPHA * h_ref[0] + (1.0 + _mod_slice(mod_ref, N_MOD + 2)) * y
    o_ref[0] = _layer_norm(r, lng_ref[...], lnb_ref[...])


def _attn_kernel(tiles_per_seq, sink_ref, q_ref, k_ref, kprev_ref, v_ref, vprev_ref, h_ref, mod_ref,
                 wo_lo_c, wo_hi_c, lng_ref, lnb_ref, o_ref, wo_s, att_s):
    i = pl.program_id(0)

    @pl.when(i < N_STAGE)
    def _():
        r = pl.multiple_of(i * LANES_V7X, LANES_V7X)
        wo_s[pl.ds(r, A_HEAD_DIM), :] = wo_lo_c[0].astype(BF16)
        wo_s[pl.ds(pl.multiple_of(r + A_HEAD_DIM, A_HEAD_DIM), A_HEAD_DIM), :] = wo_hi_c[0].astype(BF16)

    @pl.when(i >= N_STAGE)
    def _():
        _attn_tile(tiles_per_seq, sink_ref, q_ref, k_ref, kprev_ref, v_ref, vprev_ref, h_ref, mod_ref,
                   lng_ref, lnb_ref, o_ref, wo_s, att_s)


def _attn(sinks, q, k, v, h, mod, b_w_o, ln_g, ln_b):
    B, S, D = h.shape
    tq = ROW_TILE
    n_t = S // tq
    tiles = (B, n_t)
    q_dim = A_HEADS * A_HEAD_DIM
    blocks_per_tile = tq // WINDOW
    assert N_STAGE == A_HEADS // 2

    def prev_index(i):
        b, t = _tile_index(i, tiles)
        return b, jnp.maximum(t * blocks_per_tile - 1, 0), 0

    def wo_index(half):
        def index(i):
            g = jnp.minimum(i, N_STAGE - 1)
            return 0, 8 * (g // 4) + 4 * half + g % 4, 0
        return index

    return pl.pallas_call(
        functools.partial(_attn_kernel, n_t),
        grid=(N_STAGE + B * n_t,),
        in_specs=[
            pl.BlockSpec(memory_space=pltpu.SMEM),
            _row_spec(tq, q_dim, tiles),
            _row_spec(tq, A_KV_DIM, tiles),
            pl.BlockSpec((1, WINDOW, A_KV_DIM), prev_index),
            _row_spec(tq, A_KV_DIM, tiles),
            pl.BlockSpec((1, WINDOW, A_KV_DIM), prev_index),
            _row_spec(tq, D, tiles),
            _mod_spec(tiles),
            pl.BlockSpec((1, A_HEAD_DIM, D), wo_index(0)),
            pl.BlockSpec((1, A_HEAD_DIM, D), wo_index(1)),
            _const_spec((1, D)),
            _const_spec((1, D)),
        ],
        out_specs=_row_spec(tq, D, tiles),
        out_shape=jax.ShapeDtypeStruct((B, S, D), F32),
        scratch_shapes=[pltpu.VMEM((q_dim, D), BF16), pltpu.VMEM((tq, q_dim), BF16)],
        compiler_params=_compiler_params(),
        name="attn",
    )(sinks, q, k, k, v, v, h, mod, b_w_o, b_w_o, ln_g, ln_b)


def _rope_expand_matrix():
    e = np.zeros((2 * ROPE_HALF, 3 * LANES_V7X), np.float32)
    for lane in range(LANES_V7X):
        d = lane % A_HEAD_DIM
        if d < ROPE_HALF:
            e[d, lane] = 1.0
            e[ROPE_HALF + d, 2 * LANES_V7X + lane] = -1.0
        elif d < ROPE_DIM:
            e[d - ROPE_HALF, lane] = 1.0
            e[d, LANES_V7X + lane] = 1.0
    return np.tile(e, (3, 1))


def kernel(x, c, positions, ada_w, ada_b, kv_ada_w, kv_ada_b, a_w_in, a_b_gates, a_norm_w, a_w_out,
           w_kv, b_w_q, b_sinks, b_w_o, mlp_w_up, mlp_w_down, ln_g, ln_b):
    B, S, D = x.shape
    assert D == D_MODEL and S % ROW_TILE == 0 and ROW_TILE % M_CHUNK == 0

    mod = _adaln(c, ada_w, ada_b, kv_ada_w, kv_ada_b).reshape(B, 1, MOD_WIDTH)

    w_in = a_w_in[:, :, :N_MAIN].astype(BF16)
    w_gate = jnp.pad(a_w_in[0, :, N_MAIN:], ((0, 0), (0, GATE_PAD - 2 * M_HEADS))).astype(BF16)
    h = _mix0(x, mod, w_in, a_w_out, w_gate, a_b_gates[0].reshape(2 * M_HEADS, 1),
              jnp.asarray(_gate_expand_matrix(), BF16), a_norm_w[0].reshape(1, D),
              ln_g[0].reshape(1, D), ln_b[0].reshape(1, D))

    inv_freq = (ROPE_THETA ** (-jnp.arange(ROPE_HALF, dtype=F32) / ROPE_HALF)).reshape(ROPE_HALF, 1)
    h, q, k, v = _mlp0(h, mod, positions.reshape(B, 1, S), mlp_w_up, mlp_w_down, w_kv[None], b_w_q,
                       jnp.asarray(_head_perm_matrix(), BF16), ln_g[1].reshape(1, D), ln_b[1].reshape(1, D),
                       inv_freq, jnp.asarray(_rope_expand_matrix(), BF16))

    h = _attn(b_sinks[0][np.asarray(HEAD_ORDER)], q, k, v, h, mod, b_w_o,
              ln_g[2].reshape(1, D), ln_b[2].reshape(1, D))

    return _mlp1(h, mod, mlp_w_up, mlp_w_down, ln_g[3].reshape(1, D), ln_b[3].reshape(1, D))
```

```python
import functools
import math

import numpy as np
import jax
import jax.numpy as jnp
from jax import lax
from jax.experimental import pallas as pl
from jax.experimental.pallas import tpu as pltpu

F32 = jnp.float32
BF16 = jnp.bfloat16

D_MODEL = 1024
DEPTH = 2
M_HEADS = 4
M_V_DIM = D_MODEL // M_HEADS
M_QK_DIM = M_V_DIM // 2
M_CHUNK = 128
GATE_CAP = 15.0
A_HEADS = 16
A_KV_HEADS = 4
A_GROUP = A_HEADS // A_KV_HEADS
A_HEAD_DIM = 64
A_KV_DIM = A_KV_HEADS * A_HEAD_DIM
WINDOW = 128
ROPE_DIM = A_HEAD_DIM // 4
ROPE_HALF = ROPE_DIM // 2
ROPE_THETA = 500000.0
D_FF = 4 * D_MODEL
DEEPNORM_ALPHA = (2 * DEPTH) ** 0.25
LN_EPS = 1e-5
RMS_EPS = 1e-6
LOG2_E = math.log2(math.e)
N_MOD = 6
MOD_KV_BASE = DEPTH * N_MOD * D_MODEL
MOD_WIDTH = MOD_KV_BASE + 2 * D_MODEL

LANES_V7X = 128
SUBLANES_V7X = 8
VMEM_LIMIT_BYTES_V7X = 56 * 1024 * 1024

ROW_TILE = 512
ADALN_COL_TILE = 1024
N_STAGE = 8

def _compiler_params():
    return pltpu.CompilerParams(dimension_semantics=("arbitrary",), vmem_limit_bytes=VMEM_LIMIT_BYTES_V7X)


def _const_spec(shape):
    return pl.BlockSpec(shape, lambda *_: (0,) * len(shape))


def _tile_index(i, tiles, lag=0):
    n_b, n_t = tiles
    j = jnp.clip(i - N_STAGE - lag, 0, n_b * n_t - 1)
    return j // n_t, j % n_t


def _row_spec(tm, width, tiles, lag=0):
    def index(i):
        b, t = _tile_index(i, tiles, lag)
        return b, t, 0
    return pl.BlockSpec((1, tm, width), index)


def _mod_spec(tiles, lag=0):
    return pl.BlockSpec((1, 1, MOD_WIDTH), lambda i: (_tile_index(i, tiles, lag)[0], 0, 0))


def _stage_spec(w, layer):
    _, rows, cols = w.shape
    return pl.BlockSpec((1, rows // N_STAGE, cols), lambda i: (layer, jnp.minimum(i, N_STAGE - 1), 0))


def _stage_store(i, dst_ref, chunk):
    r = chunk.shape[0]
    dst_ref[pl.ds(pl.multiple_of(i * r, r), r), :] = chunk


def _layer_norm(r, g, b):
    mu = jnp.mean(r, axis=-1, keepdims=True)
    d = r - mu
    var = jnp.mean(d * d, axis=-1, keepdims=True)
    return d * lax.rsqrt(var + LN_EPS) * g + b


def _mod_slice(mod_ref, idx):
    return mod_ref[0, :, idx * D_MODEL:(idx + 1) * D_MODEL]


def _dot(a, b):
    return jnp.dot(a, b, preferred_element_type=F32)


def _dot_nt(a, b):
    return lax.dot_general(a, b, (((1,), (1,)), ((), ())), preferred_element_type=F32)


def _dot_tn(a, b):
    return lax.dot_general(a, b, (((0,), (0,)), ((), ())), preferred_element_type=F32)


def _split3_bf16(x):
    hi = x.astype(BF16)
    r1 = x - hi.astype(F32)
    mid = r1.astype(BF16)
    lo = (r1 - mid.astype(F32)).astype(BF16)
    return hi, mid, lo


N_ADA_TILES = DEPTH * N_MOD * D_MODEL // ADALN_COL_TILE
N_KV_TILES = 2 * D_MODEL // ADALN_COL_TILE


def _adaln_kernel(c_ref, wa_ref, wk_ref, ba_ref, bk_ref, o_ref):
    i = pl.program_id(0)
    c = c_ref[...]
    cs = (c * jax.nn.sigmoid(c)).astype(BF16)

    @pl.when(i < N_ADA_TILES)
    def _():
        o_ref[...] = _dot(cs, wa_ref[0].astype(BF16)) + ba_ref[0]

    @pl.when(i >= N_ADA_TILES)
    def _():
        o_ref[...] = _dot(cs, wk_ref[...].astype(BF16)) + bk_ref[0]


def _adaln(c, ada_w, ada_b, kv_ada_w, kv_ada_b):
    batch = c.shape[0]
    tiles_per_layer = N_MOD * D_MODEL // ADALN_COL_TILE
    ba = ada_b.reshape(N_ADA_TILES, 1, ADALN_COL_TILE)
    bk = kv_ada_b.reshape(N_KV_TILES, 1, ADALN_COL_TILE)

    def ada_idx(i):
        return jnp.minimum(i, N_ADA_TILES - 1)

    def kv_idx(i):
        return jnp.maximum(i - N_ADA_TILES, 0)

    return pl.pallas_call(
        _adaln_kernel,
        grid=(N_ADA_TILES + N_KV_TILES,),
        in_specs=[
            _const_spec((batch, D_MODEL)),
            pl.BlockSpec((1, D_MODEL, ADALN_COL_TILE),
                         lambda i: (ada_idx(i) // tiles_per_layer, 0, ada_idx(i) % tiles_per_layer)),
            pl.BlockSpec((D_MODEL, ADALN_COL_TILE), lambda i: (0, kv_idx(i))),
            pl.BlockSpec((1, 1, ADALN_COL_TILE), lambda i: (ada_idx(i), 0, 0)),
            pl.BlockSpec((1, 1, ADALN_COL_TILE), lambda i: (kv_idx(i), 0, 0)),
        ],
        out_specs=pl.BlockSpec((batch, ADALN_COL_TILE), lambda i: (0, i)),
        out_shape=jax.ShapeDtypeStruct((batch, MOD_WIDTH), F32),
        compiler_params=_compiler_params(),
        name="adaln",
    )(c, ada_w, kv_ada_w, ba, bk)


NQ = M_HEADS * M_QK_DIM
N_MAIN = 2 * NQ + 2 * D_MODEL
GATE_PAD = LANES_V7X
GATE_QUANTS = 3
GATE_PART_ROWS = 32
GATE_EXPAND_COLS = GATE_QUANTS * M_HEADS * LANES_V7X


def _gate_expand_matrix():
    e = np.zeros((GATE_PART_ROWS, GATE_EXPAND_COLS), np.float32)
    for quant in range(GATE_QUANTS):
        for h in range(M_HEADS):
            row = quant * SUBLANES_V7X + M_HEADS + h
            grp = quant * M_HEADS + h
            e[row, grp * LANES_V7X:(grp + 1) * LANES_V7X] = 1.0
    return np.tile(e, (3, 1))


def _segment_scan(x, op, identity):
    pos = lax.broadcasted_iota(jnp.int32, x.shape, 1) % M_CHUNK
    shift = 1
    while shift < M_CHUNK:
        x = op(x, jnp.where(pos >= shift, pltpu.roll(x, shift, axis=1), identity))
        shift *= 2
    return x


def _gate_scan(g_tm, bg_ref):
    L = M_CHUNK
    nc = g_tm.shape[0] // L
    gates_t = jnp.concatenate(
        [g_tm[c * L:(c + 1) * L, :].T[0:2 * M_HEADS, :] for c in range(nc)], axis=1) + bg_ref[...]
    capped = GATE_CAP * jnp.tanh(gates_t / GATE_CAP)
    log_f = jnp.minimum(capped, 0.0) - jnp.log1p(jnp.exp(-jnp.abs(capped)))
    bcum = _segment_scan(log_f * LOG2_E, jnp.add, 0.0)
    a = pltpu.roll(capped * LOG2_E, M_HEADS, axis=0) - bcum
    cmax = _segment_scan(a, jnp.maximum, -jnp.inf)
    stacked = jnp.concatenate([bcum, cmax, a, jnp.zeros_like(a)], axis=0)
    return a, jnp.concatenate(_split3_bf16(stacked), axis=0)


def _gate_expand(parts, gexp_ref):
    L = M_CHUNK
    return [_dot_tn(parts[:, c * L:(c + 1) * L], gexp_ref[...]) for c in range(parts.shape[1] // L)]


def _mlstm_tile(a_rows, cols, qk_s, v_s, h_s, c_s, n_s, m_s):
    L = M_CHUNK
    W = LANES_V7X
    nc = len(cols)
    heads = range(M_HEADS)
    causal = lax.broadcasted_iota(jnp.int32, (L, L), 1) <= lax.broadcasted_iota(jnp.int32, (L, L), 0)
    ones = jnp.ones((L, W), BF16)

    def rows(c):
        return slice(c * L, (c + 1) * L)

    def q_of(h, c):
        return qk_s[rows(c), h * M_QK_DIM:(h + 1) * M_QK_DIM]

    def k_of(h, c):
        return qk_s[rows(c), NQ + h * M_QK_DIM:NQ + (h + 1) * M_QK_DIM]

    def v1_of(h, c):
        return jnp.concatenate([v_s[rows(c), h * M_V_DIM:(h + 1) * M_V_DIM], ones], axis=1)

    def col(c, quant, h):
        g = quant * M_HEADS + h
        return cols[c][:, g * W:(g + 1) * W]

    scores = [[_dot_nt(q_of(h, c), k_of(h, c)) for c in range(nc)] for h in heads]

    m_in = [[None] * nc for _ in heads]
    mx = [[None] * nc for _ in heads]
    for h in heads:
        m_old = m_s[h:h + 1, :]
        for c in range(nc):
            m_in[h][c] = m_old
            mx[h][c] = jnp.maximum(m_old, col(c, 1, h)[L - 1:L, :])
            m_old = col(c, 0, h)[L - 1:L, :] + mx[h][c]
        m_s[h:h + 1, :] = m_old

    deltas = [[_dot_tn((k_of(h, c).astype(F32) * jnp.exp2(col(c, 2, h) - mx[h][c])).astype(BF16), v1_of(h, c))
               for c in range(nc)] for h in heads]

    for h in heads:
        c_t = c_s[h]
        n_bc = n_s[h]
        for c in range(nc):
            m_old = m_in[h][c]
            a_row = a_rows[M_HEADS + h:M_HEADS + h + 1, rows(c)]
            mt = jnp.maximum(col(c, 1, h), m_old)
            w_intra = jnp.exp2(jnp.where(causal, a_row - mt, -jnp.inf))
            w_inter = jnp.exp2(m_old - mt)
            e_neg = jnp.exp2(-col(c, 0, h) - mt)
            s_qk = scores[h][c] * w_intra
            lhs = jnp.concatenate([(q_of(h, c).astype(F32) * w_inter).astype(BF16), s_qk.astype(BF16)], axis=1)
            state = jnp.concatenate([c_t.astype(BF16), n_bc.astype(BF16)], axis=1)
            out = _dot(lhs, jnp.concatenate([state, v1_of(h, c)], axis=0))
            r_den = 1.0 / jnp.maximum(jnp.abs(out[:, M_V_DIM:M_V_DIM + W]), e_neg)
            for half in range(M_V_DIM // W):
                h_s[rows(c), h * M_V_DIM + half * W:h * M_V_DIM + (half + 1) * W] = (
                    out[:, half * W:(half + 1) * W] * r_den)
            decay = jnp.exp2(m_old - mx[h][c])
            c_t = jnp.concatenate([decay] * (M_V_DIM // W), axis=1) * c_t + deltas[h][c][:, 0:M_V_DIM]
            n_bc = decay * n_bc + deltas[h][c][:, M_V_DIM:M_V_DIM + W]
        c_s[h] = c_t
        n_s[h] = n_bc


def _mix0_tile(tiles_per_seq, x_ref, mod_ref, wg_ref, bg_ref, gexp_ref, nw_ref, lng_ref, lnb_ref,
               o_ref, win_s, wout_s, qk_s, v_s, og_s, h_s, c_s, n_s, m_s):
    @pl.when((pl.program_id(0) - N_STAGE) % tiles_per_seq == 0)
    def _():
        c_s[...] = jnp.zeros_like(c_s)
        n_s[...] = jnp.zeros_like(n_s)
        m_s[...] = jnp.zeros_like(m_s)

    x = x_ref[0]
    u = (x * (1.0 + _mod_slice(mod_ref, 1)) + _mod_slice(mod_ref, 0)).astype(BF16)
    a_rows, parts = _gate_scan(_dot(u, wg_ref[...]), bg_ref)
    qk_s[:, 0:NQ] = _dot(u, win_s[:, 0:NQ]).astype(BF16)
    qk_s[:, NQ:2 * NQ] = (_dot(u, win_s[:, NQ:2 * NQ]) * (1.0 / math.sqrt(M_QK_DIM))).astype(BF16)
    v_s[...] = _dot(u, win_s[:, 2 * NQ:2 * NQ + D_MODEL]).astype(BF16)
    og_s[...] = _dot(u, win_s[:, 2 * NQ + D_MODEL:N_MAIN])

    _mlstm_tile(a_rows, _gate_expand(parts, gexp_ref), qk_s, v_s, h_s, c_s, n_s, m_s)

    normed = []
    for h in range(M_HEADS):
        hh = h_s[:, h * M_V_DIM:(h + 1) * M_V_DIM]
        normed.append(hh * lax.rsqrt(jnp.mean(hh * hh, axis=-1, keepdims=True) + RMS_EPS))
    hn = jnp.concatenate(normed, axis=1)
    gated = hn * nw_ref[...] * jax.nn.sigmoid(og_s[...])
    y = _dot(gated.astype(BF16), wout_s[...])
    r = DEEPNORM_ALPHA * x + (1.0 + _mod_slice(mod_ref, 2)) * y
    o_ref[0] = _layer_norm(r, lng_ref[...], lnb_ref[...])


def _mix0_kernel(tiles_per_seq, x_ref, mod_ref, win_ref, wout_c, wg_ref, bg_ref, gexp_ref, nw_ref,
                 lng_ref, lnb_ref, o_ref, wout_s, *scratch):
    i = pl.program_id(0)

    @pl.when(i < N_STAGE)
    def _():
        _stage_store(i, wout_s, wout_c[0].astype(BF16))

    @pl.when(i >= N_STAGE)
    def _():
        _mix0_tile(tiles_per_seq, x_ref, mod_ref, wg_ref, bg_ref, gexp_ref, nw_ref, lng_ref, lnb_ref,
                   o_ref, win_ref, wout_s, *scratch)


def _mix0(x, mod, w_in, a_w_out, w_gate, b_gates, gate_expand, norm_w, ln_g, ln_b):
    B, S, D = x.shape
    tm = ROW_TILE
    n_t = S // tm
    tiles = (B, n_t)
    return pl.pallas_call(
        functools.partial(_mix0_kernel, n_t),
        grid=(N_STAGE + B * n_t,),
        in_specs=[
            _row_spec(tm, D, tiles),
            _mod_spec(tiles),
            _const_spec((D, N_MAIN)),
            _stage_spec(a_w_out, 0),
            _const_spec((D, GATE_PAD)),
            _const_spec((2 * M_HEADS, 1)),
            _const_spec(gate_expand.shape),
            _const_spec((1, D)),
            _const_spec((1, D)),
            _const_spec((1, D)),
        ],
        out_specs=_row_spec(tm, D, tiles),
        out_shape=jax.ShapeDtypeStruct((B, S, D), F32),
        scratch_shapes=[
            pltpu.VMEM((D, D), BF16),
            pltpu.VMEM((tm, 2 * NQ), BF16),
            pltpu.VMEM((tm, D_MODEL), BF16),
            pltpu.VMEM((tm, D_MODEL), F32),
            pltpu.VMEM((tm, D_MODEL), F32),
            pltpu.VMEM((M_HEADS, M_QK_DIM, M_V_DIM), F32),
            pltpu.VMEM((M_HEADS, M_QK_DIM, LANES_V7X), F32),
            pltpu.VMEM((SUBLANES_V7X, LANES_V7X), F32),
        ],
        compiler_params=_compiler_params(),
        name="mix0",
    )(x, mod, w_in, a_w_out, w_gate, b_gates, gate_expand, norm_w, ln_g, ln_b)


def _mlp_core(h, mod_ref, layer, wup_s, wdn_s, lng_ref, lnb_ref):
    base = layer * N_MOD
    u = (h * (1.0 + _mod_slice(mod_ref, base + 4)) + _mod_slice(mod_ref, base + 3)).astype(BF16)
    a = jnp.maximum(_dot(u, wup_s[...]), 0.0)
    y = _dot((a * a).astype(BF16), wdn_s[...])
    r = DEEPNORM_ALPHA * h + (1.0 + _mod_slice(mod_ref, base + 5)) * y
    return _layer_norm(r, lng_ref[...], lnb_ref[...])


def _rope_tables(pos_row, invf_ref, expand_ref):
    ang = pos_row.astype(F32) * invf_ref[...]
    trig = jnp.concatenate([jnp.cos(ang), jnp.sin(ang)], axis=0)
    parts = jnp.concatenate(_split3_bf16(trig), axis=0)
    tab = _dot_tn(parts, expand_ref[...])
    lane = lax.broadcasted_iota(jnp.int32, (1, LANES_V7X), 1)
    cos_t = tab[:, 0:LANES_V7X] + jnp.where(lane % A_HEAD_DIM >= ROPE_DIM, 1.0, 0.0)
    return cos_t, tab[:, LANES_V7X:2 * LANES_V7X], tab[:, 2 * LANES_V7X:3 * LANES_V7X]


def _rope(x, tables):
    cos_t, sin_up, sin_dn = tables
    out = []
    for g in range(x.shape[1] // LANES_V7X):
        xg = x[:, g * LANES_V7X:(g + 1) * LANES_V7X]
        from_lo = pltpu.roll(xg, ROPE_HALF, axis=1)
        from_hi = pltpu.roll(xg, LANES_V7X - ROPE_HALF, axis=1)
        out.append(xg * cos_t + from_lo * sin_up + from_hi * sin_dn)
    return jnp.concatenate(out, axis=1)


def _mlp0_kernel(h_ref, mod_ref, modp_ref, posp_ref, wup_c, wdn_c, wkv_c, wq_c, perm_ref, lng_ref, lnb_ref,
                 invf_ref, expand_ref, o_ref, q_ref, k_ref, v_ref, wup_s, wdn_s, wkv_s, wq_s, r_s):
    i = pl.program_id(0)

    @pl.when(i < N_STAGE)
    def _():
        _stage_store(i, wup_s, wup_c[0].astype(BF16))
        _stage_store(i, wdn_s, wdn_c[0].astype(BF16))
        _stage_store(i, wkv_s, wkv_c[0].astype(BF16))
        _stage_store(i, wq_s, _dot(wq_c[0].astype(BF16), perm_ref[...]).astype(BF16))

        @pl.when(i == 0)
        def _():
            r_s[...] = jnp.zeros_like(r_s)

    @pl.when(i >= N_STAGE)
    def _():
        h = h_ref[0]
        u = (h * (1.0 + _mod_slice(mod_ref, 4)) + _mod_slice(mod_ref, 3)).astype(BF16)
        a = jnp.maximum(_dot(u, wup_s[...]), 0.0)

        h1 = _layer_norm(r_s[...], lng_ref[...], lnb_ref[...])
        o_ref[0] = h1
        kv_shift = modp_ref[0, :, MOD_KV_BASE:MOD_KV_BASE + D_MODEL]
        kv_scale = modp_ref[0, :, MOD_KV_BASE + D_MODEL:MOD_KV_BASE + 2 * D_MODEL]
        kv = _dot((h1 * (1.0 + kv_scale) + kv_shift).astype(BF16), wkv_s[...])
        uq = (h1 * (1.0 + _mod_slice(modp_ref, N_MOD + 1)) + _mod_slice(modp_ref, N_MOD)).astype(BF16)
        q = _dot(uq, wq_s[...])
        tables = _rope_tables(posp_ref[0], invf_ref, expand_ref)
        q_ref[0] = (_rope(q, tables) * (LOG2_E / math.sqrt(A_HEAD_DIM))).astype(BF16)
        k_ref[0] = _rope(kv[:, 0:A_KV_DIM], tables).astype(BF16)
        v_ref[0] = kv[:, A_KV_DIM:2 * A_KV_DIM].astype(BF16)

        y = _dot((a * a).astype(BF16), wdn_s[...])
        r_s[...] = DEEPNORM_ALPHA * h + (1.0 + _mod_slice(mod_ref, 5)) * y


def _mlp1_kernel(h_ref, mod_ref, wup_c, wdn_c, lng_ref, lnb_ref, o_ref, wup_s, wdn_s):
    i = pl.program_id(0)

    @pl.when(i < N_STAGE)
    def _():
        _stage_store(i, wup_s, wup_c[0].astype(BF16))
        _stage_store(i, wdn_s, wdn_c[0].astype(BF16))

    @pl.when(i >= N_STAGE)
    def _():
        o_ref[0] = _mlp_core(h_ref[0], mod_ref, 1, wup_s, wdn_s, lng_ref, lnb_ref)


def _mlp0(h, mod, pos, mlp_w_up, mlp_w_down, w_kv, b_w_q, head_perm, ln_g, ln_b, inv_freq, expand):
    B, S, D = h.shape
    tm = ROW_TILE
    n_t = S // tm
    tiles = (B, n_t)
    q_dim = A_HEADS * A_HEAD_DIM

    def pos_index(i):
        b, t = _tile_index(i, tiles, lag=1)
        return b, 0, t

    return pl.pallas_call(
        _mlp0_kernel,
        grid=(N_STAGE + B * n_t + 1,),
        in_specs=[
            _row_spec(tm, D, tiles),
            _mod_spec(tiles),
            _mod_spec(tiles, lag=1),
            pl.BlockSpec((1, 1, tm), pos_index),
            _stage_spec(mlp_w_up, 0),
            _stage_spec(mlp_w_down, 0),
            _stage_spec(w_kv, 0),
            _stage_spec(b_w_q, 0),
            _const_spec(head_perm.shape),
            _const_spec((1, D)),
            _const_spec((1, D)),
            _const_spec((ROPE_HALF, 1)),
            _const_spec(expand.shape),
        ],
        out_specs=[_row_spec(tm, D, tiles, lag=1), _row_spec(tm, q_dim, tiles, lag=1),
                   _row_spec(tm, A_KV_DIM, tiles, lag=1), _row_spec(tm, A_KV_DIM, tiles, lag=1)],
        out_shape=[
            jax.ShapeDtypeStruct((B, S, D), F32),
            jax.ShapeDtypeStruct((B, S, q_dim), BF16),
            jax.ShapeDtypeStruct((B, S, A_KV_DIM), BF16),
            jax.ShapeDtypeStruct((B, S, A_KV_DIM), BF16),
        ],
        scratch_shapes=[
            pltpu.VMEM((D, D_FF), BF16),
            pltpu.VMEM((D_FF, D), BF16),
            pltpu.VMEM((D, 2 * A_KV_DIM), BF16),
            pltpu.VMEM((D, q_dim), BF16),
            pltpu.VMEM((tm, D), F32),
        ],
        compiler_params=_compiler_params(),
        name="mlp0",
    )(h, mod, mod, pos, mlp_w_up, mlp_w_down, w_kv, b_w_q, head_perm, ln_g, ln_b, inv_freq, expand)


def _mlp1(h, mod, mlp_w_up, mlp_w_down, ln_g, ln_b):
    B, S, D = h.shape
    tm = ROW_TILE
    n_t = S // tm
    tiles = (B, n_t)
    return pl.pallas_call(
        _mlp1_kernel,
        grid=(N_STAGE + B * n_t,),
        in_specs=[
            _row_spec(tm, D, tiles),
            _mod_spec(tiles),
            _stage_spec(mlp_w_up, 1),
            _stage_spec(mlp_w_down, 1),
            _const_spec((1, D)),
            _const_spec((1, D)),
        ],
        out_specs=_row_spec(tm, D, tiles),
        out_shape=jax.ShapeDtypeStruct((B, S, D), F32),
        scratch_shapes=[pltpu.VMEM((D, D_FF), BF16), pltpu.VMEM((D_FF, D), BF16)],
        compiler_params=_compiler_params(),
        name="mlp1",
    )(h, mod, mlp_w_up, mlp_w_down, ln_g, ln_b)


HEAD_ORDER = tuple(8 * (G // 4) + 4 * half + (G % 4) for G in range(A_HEADS // 2) for half in range(2))
KV_PAIRS = A_KV_HEADS // 2


def _head_perm_matrix():
    p = np.zeros((A_HEADS * A_HEAD_DIM, A_HEADS * A_HEAD_DIM), np.float32)
    for new, old in enumerate(HEAD_ORDER):
        for d in range(A_HEAD_DIM):
            p[old * A_HEAD_DIM + d, new * A_HEAD_DIM + d] = 1.0
    return p


def _attn_pair_operands(k_band, v_band, p, lo_half, ones_bd):
    W = LANES_V7X
    zero = jnp.zeros((), BF16)
    kp = k_band[:, p * W:(p + 1) * W]
    vp = v_band[:, p * W:(p + 1) * W]
    k_bd = jnp.concatenate([jnp.where(lo_half, kp, zero), jnp.where(lo_half, zero, kp)], axis=0)
    v_bd = jnp.concatenate([jnp.where(lo_half, vp, zero), jnp.where(lo_half, zero, vp)], axis=0)
    return k_bd, jnp.concatenate([v_bd, ones_bd], axis=1)


def _attn_softmax_pv(s_all, v_ext, p, start_bias, sink_ref, prev_visible, lo_half_q):
    L = WINDOW
    W = LANES_V7X
    zero = jnp.zeros((), BF16)
    e_rows, sink_rows = [], []
    for j in range(A_GROUP):
        halves, sink_terms = [], []
        for half in range(2):
            c0 = half * 2 * L
            s = jnp.where(prev_visible, s_all[j * L:(j + 1) * L, c0:c0 + L], s_all[j * L:(j + 1) * L, c0 + L:c0 + 2 * L])
            if start_bias is not None:
                s = s + start_bias
            sink = sink_ref[2 * (4 * p + j) + half] * LOG2_E
            mx = jnp.maximum(jnp.max(s, axis=1, keepdims=True), sink)
            e = jnp.exp2(s - mx).astype(BF16)
            halves += [jnp.where(prev_visible, e, zero), jnp.where(prev_visible, zero, e)]
            sink_terms.append(jnp.exp2(sink - mx))
        e_rows.append(jnp.concatenate(halves, axis=1))
        sink_rows.append(jnp.where(lo_half_q, sink_terms[0], sink_terms[1]))
    o_ext = _dot(jnp.concatenate(e_rows, axis=0), v_ext)
    o_all = o_ext[:, 0:W] / (o_ext[:, W:2 * W] + jnp.concatenate(sink_rows, axis=0))
    return [o_all[j * L:(j + 1) * L, :].astype(BF16) for j in range(A_GROUP)]


def _attn_tile(tiles_per_seq, sink_ref, q_ref, k_ref, kprev_ref, v_ref, vprev_ref, h_ref, mod_ref,
               lng_ref, lnb_ref, o_ref, wo_s, att_s):
    L = WINDOW
    W = LANES_V7X
    tq = q_ref.shape[1]
    prev_visible = lax.broadcasted_iota(jnp.int32, (L, L), 1) > lax.broadcasted_iota(jnp.int32, (L, L), 0)
    seq_start = (pl.program_id(0) - N_STAGE) % tiles_per_seq == 0
    start_bias = jnp.where(prev_visible & seq_start, -jnp.inf, 0.0)
    lo_half = lax.broadcasted_iota(jnp.int32, (2 * L, W), 1) < A_HEAD_DIM
    lo_half_q = lax.broadcasted_iota(jnp.int32, (L, W), 1) < A_HEAD_DIM
    ones_bd = jnp.concatenate([jnp.where(lo_half, 1.0, 0.0), jnp.where(lo_half, 0.0, 1.0)], axis=0).astype(BF16)

    for blk in range(tq // L):
        r0 = blk * L
        if blk == 0:
            k_prev, v_prev, blk_bias = kprev_ref[0], vprev_ref[0], start_bias
        else:
            k_prev, v_prev, blk_bias = k_ref[0, r0 - L:r0, :], v_ref[0, r0 - L:r0, :], None
        k_band = jnp.concatenate([k_prev, k_ref[0, r0:r0 + L, :]], axis=0)
        v_band = jnp.concatenate([v_prev, v_ref[0, r0:r0 + L, :]], axis=0)
        for p in range(KV_PAIRS):
            k_bd, v_ext = _attn_pair_operands(k_band, v_band, p, lo_half, ones_bd)
            lhs = jnp.concatenate(
                [q_ref[0, r0:r0 + L, (4 * p + j) * W:(4 * p + j + 1) * W] for j in range(A_GROUP)], axis=0)
            outs = _attn_softmax_pv(_dot_nt(lhs, k_bd), v_ext, p, blk_bias, sink_ref, prev_visible, lo_half_q)
            for j in range(A_GROUP):
                att_s[r0:r0 + L, (4 * p + j) * W:(4 * p + j + 1) * W] = outs[j]

    y = _dot(att_s[...], wo_s[...])
    r = DEEPNORM_ALPHA * h_ref[0] + (1.0 + _mod_slice(mod_ref, N_MOD + 2)) * y
    o_ref[0] = _layer_norm(r, lng_ref[...], lnb_ref[...])


def _attn_kernel(tiles_per_seq, sink_ref, q_ref, k_ref, kprev_ref, v_ref, vprev_ref, h_ref, mod_ref,
                 wo_lo_c, wo_hi_c, lng_ref, lnb_ref, o_ref, wo_s, att_s):
    i = pl.program_id(0)

    @pl.when(i < N_STAGE)
    def _():
        r = pl.multiple_of(i * LANES_V7X, LANES_V7X)
        wo_s[pl.ds(r, A_HEAD_DIM), :] = wo_lo_c[0].astype(BF16)
        wo_s[pl.ds(pl.multiple_of(r + A_HEAD_DIM, A_HEAD_DIM), A_HEAD_DIM), :] = wo_hi_c[0].astype(BF16)

    @pl.when(i >= N_STAGE)
    def _():
        _attn_tile(tiles_per_seq, sink_ref, q_ref, k_ref, kprev_ref, v_ref, vprev_ref, h_ref, mod_ref,
                   lng_ref, lnb_ref, o_ref, wo_s, att_s)


def _attn(sinks, q, k, v, h, mod, b_w_o, ln_g, ln_b):
    B, S, D = h.shape
    tq = ROW_TILE
    n_t = S // tq
    tiles = (B, n_t)
    q_dim = A_HEADS * A_HEAD_DIM
    blocks_per_tile = tq // WINDOW
    assert N_STAGE == A_HEADS // 2

    def prev_index(i):
        b, t = _tile_index(i, tiles)
        return b, jnp.maximum(t * blocks_per_tile - 1, 0), 0

    def wo_index(half):
        def index(i):
            g = jnp.minimum(i, N_STAGE - 1)
            return 0, 8 * (g // 4) + 4 * half + g % 4, 0
        return index

    return pl.pallas_call(
        functools.partial(_attn_kernel, n_t),
        grid=(N_STAGE + B * n_t,),
        in_specs=[
            pl.BlockSpec(memory_space=pltpu.SMEM),
            _row_spec(tq, q_dim, tiles),
            _row_spec(tq, A_KV_DIM, tiles),
            pl.BlockSpec((1, WINDOW, A_KV_DIM), prev_index),
            _row_spec(tq, A_KV_DIM, tiles),
            pl.BlockSpec((1, WINDOW, A_KV_DIM), prev_index),
            _row_spec(tq, D, tiles),
            _mod_spec(tiles),
            pl.BlockSpec((1, A_HEAD_DIM, D), wo_index(0)),
            pl.BlockSpec((1, A_HEAD_DIM, D), wo_index(1)),
            _const_spec((1, D)),
            _const_spec((1, D)),
        ],
        out_specs=_row_spec(tq, D, tiles),
        out_shape=jax.ShapeDtypeStruct((B, S, D), F32),
        scratch_shapes=[pltpu.VMEM((q_dim, D), BF16), pltpu.VMEM((tq, q_dim), BF16)],
        compiler_params=_compiler_params(),
        name="attn",
    )(sinks, q, k, k, v, v, h, mod, b_w_o, b_w_o, ln_g, ln_b)


def _rope_expand_matrix():
    e = np.zeros((2 * ROPE_HALF, 3 * LANES_V7X), np.float32)
    for lane in range(LANES_V7X):
        d = lane % A_HEAD_DIM
        if d < ROPE_HALF:
            e[d, lane] = 1.0
            e[ROPE_HALF + d, 2 * LANES_V7X + lane] = -1.0
        elif d < ROPE_DIM:
            e[d - ROPE_HALF, lane] = 1.0
            e[d, LANES_V7X + lane] = 1.0
    return np.tile(e, (3, 1))


def kernel(x, c, positions, ada_w, ada_b, kv_ada_w, kv_ada_b, a_w_in, a_b_gates, a_norm_w, a_w_out,
           w_kv, b_w_q, b_sinks, b_w_o, mlp_w_up, mlp_w_down, ln_g, ln_b):
    B, S, D = x.shape
    assert D == D_MODEL and S % ROW_TILE == 0 and ROW_TILE % M_CHUNK == 0

    mod = _adaln(c, ada_w, ada_b, kv_ada_w, kv_ada_b).reshape(B, 1, MOD_WIDTH)

    w_in = a_w_in[0, :, :N_MAIN].astype(BF16)
    w_gate = jnp.pad(a_w_in[0, :, N_MAIN:], ((0, 0), (0, GATE_PAD - 2 * M_HEADS))).astype(BF16)
    h = _mix0(x, mod, w_in, a_w_out, w_gate, a_b_gates[0].reshape(2 * M_HEADS, 1),
              jnp.asarray(_gate_expand_matrix(), BF16), a_norm_w[0].reshape(1, D),
              ln_g[0].reshape(1, D), ln_b[0].reshape(1, D))

    inv_freq = (ROPE_THETA ** (-jnp.arange(ROPE_HALF, dtype=F32) / ROPE_HALF)).reshape(ROPE_HALF, 1)
    h, q, k, v = _mlp0(h, mod, positions.reshape(B, 1, S), mlp_w_up, mlp_w_down, w_kv[None], b_w_q,
                       jnp.asarray(_head_perm_matrix(), BF16), ln_g[1].reshape(1, D), ln_b[1].reshape(1, D),
                       inv_freq, jnp.asarray(_rope_expand_matrix(), BF16))

    h = _attn(b_sinks[0][np.asarray(HEAD_ORDER)], q, k, v, h, mod, b_w_o,
              ln_g[2].reshape(1, D), ln_b[2].reshape(1, D))

    return _mlp1(h, mod, mlp_w_up, mlp_w_down, ln_g[3].reshape(1, D), ln_b[3].reshape(1, D))
```

```python
import functools
import math

import numpy as np
import jax
import jax.numpy as jnp
from jax import lax
from jax.experimental import pallas as pl
from jax.experimental.pallas import tpu as pltpu

F32 = jnp.float32
BF16 = jnp.bfloat16

D_MODEL = 1024
DEPTH = 2
M_HEADS = 4
M_V_DIM = D_MODEL // M_HEADS
M_QK_DIM = M_V_DIM // 2
M_CHUNK = 128
GATE_CAP = 15.0
A_HEADS = 16
A_KV_HEADS = 4
A_GROUP = A_HEADS // A_KV_HEADS
A_HEAD_DIM = 64
A_KV_DIM = A_KV_HEADS * A_HEAD_DIM
WINDOW = 128
ROPE_DIM = A_HEAD_DIM // 4
ROPE_HALF = ROPE_DIM // 2
ROPE_THETA = 500000.0
D_FF = 4 * D_MODEL
DEEPNORM_ALPHA = (2 * DEPTH) ** 0.25
LN_EPS = 1e-5
RMS_EPS = 1e-6
LOG2_E = math.log2(math.e)
N_MOD = 6
MOD_KV_BASE = DEPTH * N_MOD * D_MODEL
MOD_WIDTH = MOD_KV_BASE + 2 * D_MODEL

LANES_V7X = 128
SUBLANES_V7X = 8
VMEM_LIMIT_BYTES_V7X = 56 * 1024 * 1024

ROW_TILE = 512
ADALN_COL_TILE = 1024
ADALN_CAST_CHUNKS = 8

def _compiler_params():
    return pltpu.CompilerParams(dimension_semantics=("arbitrary",), vmem_limit_bytes=VMEM_LIMIT_BYTES_V7X)


def _const_spec(shape):
    return pl.BlockSpec(shape, lambda *_: (0,) * len(shape))


def _tile_index(i, tiles, lag=0):
    n_b, n_t = tiles
    j = jnp.clip(i - lag, 0, n_b * n_t - 1)
    return j // n_t, j % n_t


def _row_spec(tm, width, tiles, lag=0):
    def index(i):
        b, t = _tile_index(i, tiles, lag)
        return b, t, 0
    return pl.BlockSpec((1, tm, width), index)


def _mod_spec(tiles, lag=0):
    return pl.BlockSpec((1, 1, MOD_WIDTH), lambda i: (_tile_index(i, tiles, lag)[0], 0, 0))


def _cast_specs(w, layer, n_chunks):
    _, rows, cols = w.shape
    r = rows // n_chunks

    def chunk(i):
        return jnp.minimum(i, n_chunks - 1)

    return (pl.BlockSpec((1, r, cols), lambda i: (layer, chunk(i), 0)),
            pl.BlockSpec((r, cols), lambda i: (chunk(i), 0)))


def _bf16_like(w):
    return jax.ShapeDtypeStruct(w.shape[1:], BF16)


def _layer_norm(r, g, b):
    mu = jnp.mean(r, axis=-1, keepdims=True)
    d = r - mu
    var = jnp.mean(d * d, axis=-1, keepdims=True)
    return d * lax.rsqrt(var + LN_EPS) * g + b


def _mod_slice(mod_ref, idx):
    return mod_ref[0, :, idx * D_MODEL:(idx + 1) * D_MODEL]


def _dot(a, b):
    return jnp.dot(a, b, preferred_element_type=F32)


def _dot_nt(a, b):
    return lax.dot_general(a, b, (((1,), (1,)), ((), ())), preferred_element_type=F32)


def _dot_tn(a, b):
    return lax.dot_general(a, b, (((0,), (0,)), ((), ())), preferred_element_type=F32)


def _split3_bf16(x):
    hi = x.astype(BF16)
    r1 = x - hi.astype(F32)
    mid = r1.astype(BF16)
    lo = (r1 - mid.astype(F32)).astype(BF16)
    return hi, mid, lo


N_ADA_TILES = DEPTH * N_MOD * D_MODEL // ADALN_COL_TILE
N_KV_TILES = 2 * D_MODEL // ADALN_COL_TILE


def _adaln_kernel(c_ref, wa_ref, wk_ref, ba_ref, bk_ref, wout_c, o_ref, wout_b):
    i = pl.program_id(0)
    c = c_ref[...]
    cs = (c * jax.nn.sigmoid(c)).astype(BF16)

    @pl.when(i < ADALN_CAST_CHUNKS)
    def _():
        wout_b[...] = wout_c[0].astype(BF16)

    @pl.when(i < N_ADA_TILES)
    def _():
        o_ref[...] = _dot(cs, wa_ref[0].astype(BF16)) + ba_ref[0]

    @pl.when(i >= N_ADA_TILES)
    def _():
        o_ref[...] = _dot(cs, wk_ref[...].astype(BF16)) + bk_ref[0]


def _adaln(c, ada_w, ada_b, kv_ada_w, kv_ada_b, a_w_out):
    batch = c.shape[0]
    wout_in, wout_out = _cast_specs(a_w_out, 0, ADALN_CAST_CHUNKS)
    tiles_per_layer = N_MOD * D_MODEL // ADALN_COL_TILE
    ba = ada_b.reshape(N_ADA_TILES, 1, ADALN_COL_TILE)
    bk = kv_ada_b.reshape(N_KV_TILES, 1, ADALN_COL_TILE)

    def ada_idx(i):
        return jnp.minimum(i, N_ADA_TILES - 1)

    def kv_idx(i):
        return jnp.maximum(i - N_ADA_TILES, 0)

    return pl.pallas_call(
        _adaln_kernel,
        grid=(N_ADA_TILES + N_KV_TILES,),
        in_specs=[
            _const_spec((batch, D_MODEL)),
            pl.BlockSpec((1, D_MODEL, ADALN_COL_TILE),
                         lambda i: (ada_idx(i) // tiles_per_layer, 0, ada_idx(i) % tiles_per_layer)),
            pl.BlockSpec((D_MODEL, ADALN_COL_TILE), lambda i: (0, kv_idx(i))),
            pl.BlockSpec((1, 1, ADALN_COL_TILE), lambda i: (ada_idx(i), 0, 0)),
            pl.BlockSpec((1, 1, ADALN_COL_TILE), lambda i: (kv_idx(i), 0, 0)),
            wout_in,
        ],
        out_specs=[pl.BlockSpec((batch, ADALN_COL_TILE), lambda i: (0, i)), wout_out],
        out_shape=[jax.ShapeDtypeStruct((batch, MOD_WIDTH), F32), _bf16_like(a_w_out)],
        compiler_params=_compiler_params(),
        name="adaln",
    )(c, ada_w, kv_ada_w, ba, bk, a_w_out)


NQ = M_HEADS * M_QK_DIM
N_MAIN = 2 * NQ + 2 * D_MODEL
GATE_PAD = LANES_V7X
GATE_QUANTS = 3
GATE_PART_ROWS = 32
GATE_EXPAND_COLS = GATE_QUANTS * M_HEADS * LANES_V7X


def _gate_expand_matrix():
    e = np.zeros((GATE_PART_ROWS, GATE_EXPAND_COLS), np.float32)
    for quant in range(GATE_QUANTS):
        for h in range(M_HEADS):
            row = quant * SUBLANES_V7X + M_HEADS + h
            grp = quant * M_HEADS + h
            e[row, grp * LANES_V7X:(grp + 1) * LANES_V7X] = 1.0
    return np.tile(e, (3, 1))


def _segment_scan(x, op, identity):
    pos = lax.broadcasted_iota(jnp.int32, x.shape, 1) % M_CHUNK
    shift = 1
    while shift < M_CHUNK:
        x = op(x, jnp.where(pos >= shift, pltpu.roll(x, shift, axis=1), identity))
        shift *= 2
    return x


def _gate_scan(g_tm, bg_ref):
    L = M_CHUNK
    nc = g_tm.shape[0] // L
    gates_t = jnp.concatenate(
        [g_tm[c * L:(c + 1) * L, :].T[0:2 * M_HEADS, :] for c in range(nc)], axis=1) + bg_ref[...]
    capped = GATE_CAP * jnp.tanh(gates_t / GATE_CAP)
    log_f = jnp.minimum(capped, 0.0) - jnp.log1p(jnp.exp(-jnp.abs(capped)))
    bcum = _segment_scan(log_f * LOG2_E, jnp.add, 0.0)
    a = pltpu.roll(capped * LOG2_E, M_HEADS, axis=0) - bcum
    cmax = _segment_scan(a, jnp.maximum, -jnp.inf)
    stacked = jnp.concatenate([bcum, cmax, a, jnp.zeros_like(a)], axis=0)
    return a, jnp.concatenate(_split3_bf16(stacked), axis=0)


def _gate_expand(parts, gexp_ref):
    L = M_CHUNK
    return [_dot_tn(parts[:, c * L:(c + 1) * L], gexp_ref[...]) for c in range(parts.shape[1] // L)]


def _mlstm_tile(a_rows, cols, qk_s, v_s, h_s, c_s, n_s, m_s):
    L = M_CHUNK
    W = LANES_V7X
    nc = len(cols)
    heads = range(M_HEADS)
    causal = lax.broadcasted_iota(jnp.int32, (L, L), 1) <= lax.broadcasted_iota(jnp.int32, (L, L), 0)
    ones = jnp.ones((L, W), BF16)

    def rows(c):
        return slice(c * L, (c + 1) * L)

    def q_of(h, c):
        return qk_s[rows(c), h * M_QK_DIM:(h + 1) * M_QK_DIM]

    def k_of(h, c):
        return qk_s[rows(c), NQ + h * M_QK_DIM:NQ + (h + 1) * M_QK_DIM]

    def v1_of(h, c):
        return jnp.concatenate([v_s[rows(c), h * M_V_DIM:(h + 1) * M_V_DIM], ones], axis=1)

    def col(c, quant, h):
        g = quant * M_HEADS + h
        return cols[c][:, g * W:(g + 1) * W]

    scores = [[_dot_nt(q_of(h, c), k_of(h, c)) for c in range(nc)] for h in heads]

    m_in = [[None] * nc for _ in heads]
    mx = [[None] * nc for _ in heads]
    for h in heads:
        m_old = m_s[h:h + 1, :]
        for c in range(nc):
            m_in[h][c] = m_old
            mx[h][c] = jnp.maximum(m_old, col(c, 1, h)[L - 1:L, :])
            m_old = col(c, 0, h)[L - 1:L, :] + mx[h][c]
        m_s[h:h + 1, :] = m_old

    deltas = [[_dot_tn((k_of(h, c).astype(F32) * jnp.exp2(col(c, 2, h) - mx[h][c])).astype(BF16), v1_of(h, c))
               for c in range(nc)] for h in heads]

    for h in heads:
        c_t = c_s[h]
        n_bc = n_s[h]
        for c in range(nc):
            m_old = m_in[h][c]
            a_row = a_rows[M_HEADS + h:M_HEADS + h + 1, rows(c)]
            mt = jnp.maximum(col(c, 1, h), m_old)
            w_intra = jnp.exp2(jnp.where(causal, a_row - mt, -jnp.inf))
            w_inter = jnp.exp2(m_old - mt)
            e_neg = jnp.exp2(-col(c, 0, h) - mt)
            s_qk = scores[h][c] * w_intra
            lhs = jnp.concatenate([(q_of(h, c).astype(F32) * w_inter).astype(BF16), s_qk.astype(BF16)], axis=1)
            state = jnp.concatenate([c_t.astype(BF16), n_bc.astype(BF16)], axis=1)
            out = _dot(lhs, jnp.concatenate([state, v1_of(h, c)], axis=0))
            r_den = 1.0 / jnp.maximum(jnp.abs(out[:, M_V_DIM:M_V_DIM + W]), e_neg)
            for half in range(M_V_DIM // W):
                h_s[rows(c), h * M_V_DIM + half * W:h * M_V_DIM + (half + 1) * W] = (
                    out[:, half * W:(half + 1) * W] * r_den)
            decay = jnp.exp2(m_old - mx[h][c])
            c_t = jnp.concatenate([decay] * (M_V_DIM // W), axis=1) * c_t + deltas[h][c][:, 0:M_V_DIM]
            n_bc = decay * n_bc + deltas[h][c][:, M_V_DIM:M_V_DIM + W]
        c_s[h] = c_t
        n_s[h] = n_bc


def _mix0_tile(tiles_per_seq, x_ref, mod_ref, wg_ref, bg_ref, gexp_ref, nw_ref, lng_ref, lnb_ref,
               o_ref, win_s, wout_s, qk_s, v_s, og_s, h_s, c_s, n_s, m_s):
    @pl.when(pl.program_id(0) % tiles_per_seq == 0)
    def _():
        c_s[...] = jnp.zeros_like(c_s)
        n_s[...] = jnp.zeros_like(n_s)
        m_s[...] = jnp.zeros_like(m_s)

    x = x_ref[0]
    u = (x * (1.0 + _mod_slice(mod_ref, 1)) + _mod_slice(mod_ref, 0)).astype(BF16)
    a_rows, parts = _gate_scan(_dot(u, wg_ref[...]), bg_ref)
    qk_s[:, 0:NQ] = _dot(u, win_s[:, 0:NQ]).astype(BF16)
    qk_s[:, NQ:2 * NQ] = (_dot(u, win_s[:, NQ:2 * NQ]) * (1.0 / math.sqrt(M_QK_DIM))).astype(BF16)
    v_s[...] = _dot(u, win_s[:, 2 * NQ:2 * NQ + D_MODEL]).astype(BF16)
    og_s[...] = _dot(u, win_s[:, 2 * NQ + D_MODEL:N_MAIN])

    _mlstm_tile(a_rows, _gate_expand(parts, gexp_ref), qk_s, v_s, h_s, c_s, n_s, m_s)

    normed = []
    for h in range(M_HEADS):
        hh = h_s[:, h * M_V_DIM:(h + 1) * M_V_DIM]
        normed.append(hh * lax.rsqrt(jnp.mean(hh * hh, axis=-1, keepdims=True) + RMS_EPS))
    hn = jnp.concatenate(normed, axis=1)
    gated = hn * nw_ref[...] * jax.nn.sigmoid(og_s[...])
    y = _dot(gated.astype(BF16), wout_s[...])
    r = DEEPNORM_ALPHA * x + (1.0 + _mod_slice(mod_ref, 2)) * y
    o_ref[0] = _layer_norm(r, lng_ref[...], lnb_ref[...])


def _mix0_kernel(tiles_per_seq, x_ref, mod_ref, win_ref, wout_ref, wg_ref, bg_ref, gexp_ref, nw_ref,
                 lng_ref, lnb_ref, up_c, dn_c, kv_c, q_c,
                 o_ref, up_b, dn_b, kv_b, q_b, *scratch):
    up_b[...] = up_c[0].astype(BF16)
    dn_b[...] = dn_c[0].astype(BF16)
    kv_b[...] = kv_c[0].astype(BF16)
    wq = q_c[0]
    q_b[...] = jnp.concatenate(
        [wq[:, hd * A_HEAD_DIM:(hd + 1) * A_HEAD_DIM] for hd in HEAD_ORDER], axis=1).astype(BF16)
    _mix0_tile(tiles_per_seq, x_ref, mod_ref, wg_ref, bg_ref, gexp_ref, nw_ref, lng_ref, lnb_ref,
               o_ref, win_ref, wout_ref, *scratch)


def _mix0(x, mod, w_in, w_out, w_gate, b_gates, gate_expand, norm_w, ln_g, ln_b,
          mlp_w_up, mlp_w_down, w_kv, b_w_q):
    B, S, D = x.shape
    tm = ROW_TILE
    n_t = S // tm
    tiles = (B, n_t)
    casts = [_cast_specs(w, 0, B * n_t) for w in (mlp_w_up, mlp_w_down, w_kv, b_w_q)]
    return pl.pallas_call(
        functools.partial(_mix0_kernel, n_t),
        grid=(B * n_t,),
        in_specs=[
            _row_spec(tm, D, tiles),
            _mod_spec(tiles),
            _const_spec((D, N_MAIN)),
            _const_spec((D, D)),
            _const_spec((D, GATE_PAD)),
            _const_spec((2 * M_HEADS, 1)),
            _const_spec(gate_expand.shape),
            _const_spec((1, D)),
            _const_spec((1, D)),
            _const_spec((1, D)),
            *[c[0] for c in casts],
        ],
        out_specs=[_row_spec(tm, D, tiles), *[c[1] for c in casts]],
        out_shape=[jax.ShapeDtypeStruct((B, S, D), F32),
                   *[_bf16_like(w) for w in (mlp_w_up, mlp_w_down, w_kv, b_w_q)]],
        scratch_shapes=[
            pltpu.VMEM((tm, 2 * NQ), BF16),
            pltpu.VMEM((tm, D_MODEL), BF16),
            pltpu.VMEM((tm, D_MODEL), F32),
            pltpu.VMEM((tm, D_MODEL), F32),
            pltpu.VMEM((M_HEADS, M_QK_DIM, M_V_DIM), F32),
            pltpu.VMEM((M_HEADS, M_QK_DIM, LANES_V7X), F32),
            pltpu.VMEM((SUBLANES_V7X, LANES_V7X), F32),
        ],
        compiler_params=_compiler_params(),
        name="mix0",
    )(x, mod, w_in, w_out, w_gate, b_gates, gate_expand, norm_w, ln_g, ln_b,
      mlp_w_up, mlp_w_down, w_kv, b_w_q)


def _mlp_core(h, mod_ref, layer, wup_s, wdn_s, lng_ref, lnb_ref):
    base = layer * N_MOD
    u = (h * (1.0 + _mod_slice(mod_ref, base + 4)) + _mod_slice(mod_ref, base + 3)).astype(BF16)
    a = jnp.maximum(_dot(u, wup_s[...]), 0.0)
    y = _dot((a * a).astype(BF16), wdn_s[...])
    r = DEEPNORM_ALPHA * h + (1.0 + _mod_slice(mod_ref, base + 5)) * y
    return _layer_norm(r, lng_ref[...], lnb_ref[...])


def _rope_tables(pos_row, invf_ref, expand_ref):
    ang = pos_row.astype(F32) * invf_ref[...]
    trig = jnp.concatenate([jnp.cos(ang), jnp.sin(ang)], axis=0)
    parts = jnp.concatenate(_split3_bf16(trig), axis=0)
    tab = _dot_tn(parts, expand_ref[...])
    lane = lax.broadcasted_iota(jnp.int32, (1, LANES_V7X), 1)
    cos_t = tab[:, 0:LANES_V7X] + jnp.where(lane % A_HEAD_DIM >= ROPE_DIM, 1.0, 0.0)
    return cos_t, tab[:, LANES_V7X:2 * LANES_V7X], tab[:, 2 * LANES_V7X:3 * LANES_V7X]


def _rope(x, tables):
    cos_t, sin_up, sin_dn = tables
    out = []
    for g in range(x.shape[1] // LANES_V7X):
        xg = x[:, g * LANES_V7X:(g + 1) * LANES_V7X]
        from_lo = pltpu.roll(xg, ROPE_HALF, axis=1)
        from_hi = pltpu.roll(xg, LANES_V7X - ROPE_HALF, axis=1)
        out.append(xg * cos_t + from_lo * sin_up + from_hi * sin_dn)
    return jnp.concatenate(out, axis=1)


def _mlp0_kernel(h_ref, mod_ref, modp_ref, posp_ref, wup_ref, wdn_ref, wkv_ref, wq_ref, lng_ref, lnb_ref,
                 invf_ref, expand_ref, up_c, dn_c, wo_c,
                 o_ref, q_ref, k_ref, v_ref, up_b, dn_b, wo_b, r_s):
    @pl.when(pl.program_id(0) == 0)
    def _():
        r_s[...] = jnp.zeros_like(r_s)

    up_b[...] = up_c[0].astype(BF16)
    dn_b[...] = dn_c[0].astype(BF16)
    wo_b[...] = wo_c[0].astype(BF16)

    h = h_ref[0]
    u = (h * (1.0 + _mod_slice(mod_ref, 4)) + _mod_slice(mod_ref, 3)).astype(BF16)
    a = jnp.maximum(_dot(u, wup_ref[...]), 0.0)

    h1 = _layer_norm(r_s[...], lng_ref[...], lnb_ref[...])
    o_ref[0] = h1
    kv_shift = modp_ref[0, :, MOD_KV_BASE:MOD_KV_BASE + D_MODEL]
    kv_scale = modp_ref[0, :, MOD_KV_BASE + D_MODEL:MOD_KV_BASE + 2 * D_MODEL]
    kv = _dot((h1 * (1.0 + kv_scale) + kv_shift).astype(BF16), wkv_ref[...])
    uq = (h1 * (1.0 + _mod_slice(modp_ref, N_MOD + 1)) + _mod_slice(modp_ref, N_MOD)).astype(BF16)
    q = _dot(uq, wq_ref[...])
    tables = _rope_tables(posp_ref[0], invf_ref, expand_ref)
    q_ref[0] = (_rope(q, tables) * (LOG2_E / math.sqrt(A_HEAD_DIM))).astype(BF16)
    k_ref[0] = _rope(kv[:, 0:A_KV_DIM], tables).astype(BF16)
    v_ref[0] = kv[:, A_KV_DIM:2 * A_KV_DIM].astype(BF16)

    y = _dot((a * a).astype(BF16), wdn_ref[...])
    r_s[...] = DEEPNORM_ALPHA * h + (1.0 + _mod_slice(mod_ref, 5)) * y


def _mlp1_kernel(h_ref, mod_ref, wup_ref, wdn_ref, lng_ref, lnb_ref, o_ref):
    o_ref[0] = _mlp_core(h_ref[0], mod_ref, 1, wup_ref, wdn_ref, lng_ref, lnb_ref)


def _mlp0(h, mod, pos, w_up, w_dn, w_kv, w_q, ln_g, ln_b, inv_freq, expand, mlp_w_up, mlp_w_down, b_w_o):
    B, S, D = h.shape
    tm = ROW_TILE
    n_t = S // tm
    tiles = (B, n_t)
    n_tiles = B * n_t
    q_dim = A_HEADS * A_HEAD_DIM

    def pos_index(i):
        b, t = _tile_index(i, tiles, lag=1)
        return b, 0, t

    up_in, up_out = _cast_specs(mlp_w_up, 1, n_tiles)
    dn_in, dn_out = _cast_specs(mlp_w_down, 1, n_tiles)
    _, wo_out = _cast_specs(b_w_o, 0, n_tiles)
    wo_rows = q_dim // n_tiles
    subs = A_HEAD_DIM // wo_rows

    def wo_index(i):
        c = jnp.minimum(i, n_tiles - 1)
        slot, sub = c // subs, c % subs
        g, half = slot // 2, slot % 2
        return 0, (8 * (g // 4) + 4 * half + g % 4) * subs + sub, 0

    return pl.pallas_call(
        _mlp0_kernel,
        grid=(n_tiles + 1,),
        in_specs=[
            _row_spec(tm, D, tiles),
            _mod_spec(tiles),
            _mod_spec(tiles, lag=1),
            pl.BlockSpec((1, 1, tm), pos_index),
            _const_spec((D, D_FF)),
            _const_spec((D_FF, D)),
            _const_spec((D, 2 * A_KV_DIM)),
            _const_spec((D, q_dim)),
            _const_spec((1, D)),
            _const_spec((1, D)),
            _const_spec((ROPE_HALF, 1)),
            _const_spec(expand.shape),
            up_in,
            dn_in,
            pl.BlockSpec((1, wo_rows, D), wo_index),
        ],
        out_specs=[_row_spec(tm, D, tiles, lag=1), _row_spec(tm, q_dim, tiles, lag=1),
                   _row_spec(tm, A_KV_DIM, tiles, lag=1), _row_spec(tm, A_KV_DIM, tiles, lag=1),
                   up_out, dn_out, wo_out],
        out_shape=[
            jax.ShapeDtypeStruct((B, S, D), F32),
            jax.ShapeDtypeStruct((B, S, q_dim), BF16),
            jax.ShapeDtypeStruct((B, S, A_KV_DIM), BF16),
            jax.ShapeDtypeStruct((B, S, A_KV_DIM), BF16),
            _bf16_like(mlp_w_up), _bf16_like(mlp_w_down), _bf16_like(b_w_o),
        ],
        scratch_shapes=[pltpu.VMEM((tm, D), F32)],
        compiler_params=_compiler_params(),
        name="mlp0",
    )(h, mod, mod, pos, w_up, w_dn, w_kv, w_q, ln_g, ln_b, inv_freq, expand, mlp_w_up, mlp_w_down, b_w_o)


def _mlp1(h, mod, w_up, w_dn, ln_g, ln_b):
    B, S, D = h.shape
    tm = ROW_TILE
    n_t = S // tm
    tiles = (B, n_t)
    return pl.pallas_call(
        _mlp1_kernel,
        grid=(B * n_t,),
        in_specs=[
            _row_spec(tm, D, tiles),
            _mod_spec(tiles),
            _const_spec((D, D_FF)),
            _const_spec((D_FF, D)),
            _const_spec((1, D)),
            _const_spec((1, D)),
        ],
        out_specs=_row_spec(tm, D, tiles),
        out_shape=jax.ShapeDtypeStruct((B, S, D), F32),
        compiler_params=_compiler_params(),
        name="mlp1",
    )(h, mod, w_up, w_dn, ln_g, ln_b)


HEAD_ORDER = tuple(8 * (G // 4) + 4 * half + (G % 4) for G in range(A_HEADS // 2) for half in range(2))
KV_PAIRS = A_KV_HEADS // 2


def _attn_pair_operands(k_band, v_band, p, lo_half, ones_bd):
    W = LANES_V7X
    zero = jnp.zeros((), BF16)
    kp = k_band[:, p * W:(p + 1) * W]
    vp = v_band[:, p * W:(p + 1) * W]
    k_bd = jnp.concatenate([jnp.where(lo_half, kp, zero), jnp.where(lo_half, zero, kp)], axis=0)
    v_bd = jnp.concatenate([jnp.where(lo_half, vp, zero), jnp.where(lo_half, zero, vp)], axis=0)
    return k_bd, jnp.concatenate([v_bd, ones_bd], axis=1)


def _attn_softmax_pv(s_all, v_ext, p, start_bias, sink_ref, prev_visible, lo_half_q):
    L = WINDOW
    W = LANES_V7X
    zero = jnp.zeros((), BF16)
    e_rows, sink_rows = [], []
    for j in range(A_GROUP):
        halves, sink_terms = [], []
        for half in range(2):
            c0 = half * 2 * L
            s = jnp.where(prev_visible, s_all[j * L:(j + 1) * L, c0:c0 + L], s_all[j * L:(j + 1) * L, c0 + L:c0 + 2 * L])
            if start_bias is not None:
                s = s + start_bias
            sink = sink_ref[2 * (4 * p + j) + half] * LOG2_E
            mx = jnp.maximum(jnp.max(s, axis=1, keepdims=True), sink)
            e = jnp.exp2(s - mx).astype(BF16)
            halves += [jnp.where(prev_visible, e, zero), jnp.where(prev_visible, zero, e)]
            sink_terms.append(jnp.exp2(sink - mx))
        e_rows.append(jnp.concatenate(halves, axis=1))
        sink_rows.append(jnp.where(lo_half_q, sink_terms[0], sink_terms[1]))
    o_ext = _dot(jnp.concatenate(e_rows, axis=0), v_ext)
    o_all = o_ext[:, 0:W] / (o_ext[:, W:2 * W] + jnp.concatenate(sink_rows, axis=0))
    return [o_all[j * L:(j + 1) * L, :].astype(BF16) for j in range(A_GROUP)]


def _attn_tile(tiles_per_seq, sink_ref, q_ref, k_ref, kprev_ref, v_ref, vprev_ref, h_ref, mod_ref,
               wo_ref, lng_ref, lnb_ref, o_ref, att_s):
    L = WINDOW
    W = LANES_V7X
    tq = q_ref.shape[1]
    prev_visible = lax.broadcasted_iota(jnp.int32, (L, L), 1) > lax.broadcasted_iota(jnp.int32, (L, L), 0)
    seq_start = pl.program_id(0) % tiles_per_seq == 0
    start_bias = jnp.where(prev_visible & seq_start, -jnp.inf, 0.0)
    lo_half = lax.broadcasted_iota(jnp.int32, (2 * L, W), 1) < A_HEAD_DIM
    lo_half_q = lax.broadcasted_iota(jnp.int32, (L, W), 1) < A_HEAD_DIM
    ones_bd = jnp.concatenate([jnp.where(lo_half, 1.0, 0.0), jnp.where(lo_half, 0.0, 1.0)], axis=0).astype(BF16)

    for blk in range(tq // L):
        r0 = blk * L
        if blk == 0:
            k_prev, v_prev, blk_bias = kprev_ref[0], vprev_ref[0], start_bias
        else:
            k_prev, v_prev, blk_bias = k_ref[0, r0 - L:r0, :], v_ref[0, r0 - L:r0, :], None
        k_band = jnp.concatenate([k_prev, k_ref[0, r0:r0 + L, :]], axis=0)
        v_band = jnp.concatenate([v_prev, v_ref[0, r0:r0 + L, :]], axis=0)
        for p in range(KV_PAIRS):
            k_bd, v_ext = _attn_pair_operands(k_band, v_band, p, lo_half, ones_bd)
            lhs = jnp.concatenate(
                [q_ref[0, r0:r0 + L, (4 * p + j) * W:(4 * p + j + 1) * W] for j in range(A_GROUP)], axis=0)
            outs = _attn_softmax_pv(_dot_nt(lhs, k_bd), v_ext, p, blk_bias, sink_ref, prev_visible, lo_half_q)
            for j in range(A_GROUP):
                att_s[r0:r0 + L, (4 * p + j) * W:(4 * p + j + 1) * W] = outs[j]

    y = _dot(att_s[...], wo_ref[...])
    r = DEEPNORM_ALPHA * h_ref[0] + (1.0 + _mod_slice(mod_ref, N_MOD + 2)) * y
    o_ref[0] = _layer_norm(r, lng_ref[...], lnb_ref[...])


def _attn(sinks, q, k, v, h, mod, w_o, ln_g, ln_b):
    B, S, D = h.shape
    tq = ROW_TILE
    n_t = S // tq
    tiles = (B, n_t)
    q_dim = A_HEADS * A_HEAD_DIM
    blocks_per_tile = tq // WINDOW

    def prev_index(i):
        b, t = _tile_index(i, tiles)
        return b, jnp.maximum(t * blocks_per_tile - 1, 0), 0

    return pl.pallas_call(
        functools.partial(_attn_tile, n_t),
        grid=(B * n_t,),
        in_specs=[
            pl.BlockSpec(memory_space=pltpu.SMEM),
            _row_spec(tq, q_dim, tiles),
            _row_spec(tq, A_KV_DIM, tiles),
            pl.BlockSpec((1, WINDOW, A_KV_DIM), prev_index),
            _row_spec(tq, A_KV_DIM, tiles),
            pl.BlockSpec((1, WINDOW, A_KV_DIM), prev_index),
            _row_spec(tq, D, tiles),
            _mod_spec(tiles),
            _const_spec((q_dim, D)),
            _const_spec((1, D)),
            _const_spec((1, D)),
        ],
        out_specs=_row_spec(tq, D, tiles),
        out_shape=jax.ShapeDtypeStruct((B, S, D), F32),
        scratch_shapes=[pltpu.VMEM((tq, q_dim), BF16)],
        compiler_params=_compiler_params(),
        name="attn",
    )(sinks, q, k, k, v, v, h, mod, w_o, ln_g, ln_b)


def _rope_expand_matrix():
    e = np.zeros((2 * ROPE_HALF, 3 * LANES_V7X), np.float32)
    for lane in range(LANES_V7X):
        d = lane % A_HEAD_DIM
        if d < ROPE_HALF:
            e[d, lane] = 1.0
            e[ROPE_HALF + d, 2 * LANES_V7X + lane] = -1.0
        elif d < ROPE_DIM:
            e[d - ROPE_HALF, lane] = 1.0
            e[d, LANES_V7X + lane] = 1.0
    return np.tile(e, (3, 1))


def kernel(x, c, positions, ada_w, ada_b, kv_ada_w, kv_ada_b, a_w_in, a_b_gates, a_norm_w, a_w_out,
           w_kv, b_w_q, b_sinks, b_w_o, mlp_w_up, mlp_w_down, ln_g, ln_b):
    B, S, D = x.shape
    assert D == D_MODEL and S % ROW_TILE == 0 and ROW_TILE % M_CHUNK == 0

    mod, w_out = _adaln(c, ada_w, ada_b, kv_ada_w, kv_ada_b, a_w_out)
    mod = mod.reshape(B, 1, MOD_WIDTH)

    w_in = a_w_in[0, :, :N_MAIN].astype(BF16)
    w_gate = jnp.pad(a_w_in[0, :, N_MAIN:], ((0, 0), (0, GATE_PAD - 2 * M_HEADS))).astype(BF16)
    h, w_up0, w_dn0, w_kv_b, w_q = _mix0(
        x, mod, w_in, w_out, w_gate, a_b_gates[0].reshape(2 * M_HEADS, 1),
        jnp.asarray(_gate_expand_matrix(), BF16), a_norm_w[0].reshape(1, D),
        ln_g[0].reshape(1, D), ln_b[0].reshape(1, D),
        mlp_w_up, mlp_w_down, w_kv[None], b_w_q)

    inv_freq = (ROPE_THETA ** (-jnp.arange(ROPE_HALF, dtype=F32) / ROPE_HALF)).reshape(ROPE_HALF, 1)
    h, q, k, v, w_up1, w_dn1, w_o = _mlp0(
        h, mod, positions.reshape(B, 1, S), w_up0, w_dn0, w_kv_b, w_q, ln_g[1].reshape(1, D),
        ln_b[1].reshape(1, D), inv_freq, jnp.asarray(_rope_expand_matrix(), BF16), mlp_w_up, mlp_w_down, b_w_o)

    h = _attn(b_sinks[0][np.asarray(HEAD_ORDER)], q, k, v, h, mod, w_o, ln_g[2].reshape(1, D), ln_b[2].reshape(1, D))

    return _mlp1(h, mod, w_up1, w_dn1, ln_g[3].reshape(1, D), ln_b[3].reshape(1, D))
```

```python
import functools
import math

import numpy as np
import jax
import jax.numpy as jnp
from jax import lax
from jax.experimental import pallas as pl
from jax.experimental.pallas import tpu as pltpu

F32 = jnp.float32
BF16 = jnp.bfloat16

D_MODEL = 1024
DEPTH = 2
M_HEADS = 4
M_V_DIM = D_MODEL // M_HEADS
M_QK_DIM = M_V_DIM // 2
M_CHUNK = 128
GATE_CAP = 15.0
A_HEADS = 16
A_KV_HEADS = 4
A_GROUP = A_HEADS // A_KV_HEADS
A_HEAD_DIM = 64
A_KV_DIM = A_KV_HEADS * A_HEAD_DIM
WINDOW = 128
ROPE_DIM = A_HEAD_DIM // 4
ROPE_HALF = ROPE_DIM // 2
ROPE_THETA = 500000.0
D_FF = 4 * D_MODEL
DEEPNORM_ALPHA = (2 * DEPTH) ** 0.25
LN_EPS = 1e-5
RMS_EPS = 1e-6
LOG2_E = math.log2(math.e)
N_MOD = 6
MOD_KV_BASE = DEPTH * N_MOD * D_MODEL
MOD_WIDTH = MOD_KV_BASE + 2 * D_MODEL

LANES_V7X = 128
SUBLANES_V7X = 8
VMEM_LIMIT_BYTES_V7X = 56 * 1024 * 1024

ROW_TILE = 512
ADALN_COL_TILE = 1024
ADALN_CAST_CHUNKS = 8

def _compiler_params():
    return pltpu.CompilerParams(dimension_semantics=("arbitrary",), vmem_limit_bytes=VMEM_LIMIT_BYTES_V7X)


def _const_spec(shape):
    return pl.BlockSpec(shape, lambda *_: (0,) * len(shape))


def _tile_index(i, tiles, lag=0):
    n_b, n_t = tiles
    j = jnp.clip(i - lag, 0, n_b * n_t - 1)
    return j // n_t, j % n_t


def _row_spec(tm, width, tiles, lag=0):
    def index(i):
        b, t = _tile_index(i, tiles, lag)
        return b, t, 0
    return pl.BlockSpec((1, tm, width), index)


def _mod_spec(tiles, lag=0):
    return pl.BlockSpec((1, 1, MOD_WIDTH), lambda i: (_tile_index(i, tiles, lag)[0], 0, 0))


def _cast_specs(w, layer, n_chunks):
    _, rows, cols = w.shape
    r = rows // n_chunks

    def chunk(i):
        return jnp.minimum(i, n_chunks - 1)

    return (pl.BlockSpec((1, r, cols), lambda i: (layer, chunk(i), 0)),
            pl.BlockSpec((r, cols), lambda i: (chunk(i), 0)))


def _bf16_like(w):
    return jax.ShapeDtypeStruct(w.shape[1:], BF16)


def _layer_norm(r, g, b):
    mu = jnp.mean(r, axis=-1, keepdims=True)
    d = r - mu
    var = jnp.mean(d * d, axis=-1, keepdims=True)
    return d * lax.rsqrt(var + LN_EPS) * g + b


def _mod_slice(mod_ref, idx):
    return mod_ref[0, :, idx * D_MODEL:(idx + 1) * D_MODEL]


def _dot(a, b):
    return jnp.dot(a, b, preferred_element_type=F32)


def _dot_nt(a, b):
    return lax.dot_general(a, b, (((1,), (1,)), ((), ())), preferred_element_type=F32)


def _dot_tn(a, b):
    return lax.dot_general(a, b, (((0,), (0,)), ((), ())), preferred_element_type=F32)


def _split3_bf16(x):
    hi = x.astype(BF16)
    r1 = x - hi.astype(F32)
    mid = r1.astype(BF16)
    lo = (r1 - mid.astype(F32)).astype(BF16)
    return hi, mid, lo


N_ADA_TILES = DEPTH * N_MOD * D_MODEL // ADALN_COL_TILE
N_KV_TILES = 2 * D_MODEL // ADALN_COL_TILE


def _adaln_kernel(c_ref, wa_ref, wk_ref, ba_ref, bk_ref, wout_c, win_c, wg_c, o_ref, wout_b, win_b, wg_b):
    i = pl.program_id(0)
    c = c_ref[...]
    cs = (c * jax.nn.sigmoid(c)).astype(BF16)

    @pl.when(i < ADALN_CAST_CHUNKS)
    def _():
        wout_b[...] = wout_c[0].astype(BF16)
        win_b[...] = win_c[0].astype(BF16)

    @pl.when(i == 0)
    def _():
        pad = jnp.zeros((GATE_PAD - 2 * M_HEADS, D_MODEL), F32)
        wg_b[...] = jnp.concatenate([wg_c[0], pad], axis=0).astype(BF16)

    @pl.when(i < N_ADA_TILES)
    def _():
        o_ref[...] = _dot(cs, wa_ref[0].astype(BF16)) + ba_ref[0]

    @pl.when(i >= N_ADA_TILES)
    def _():
        o_ref[...] = _dot(cs, wk_ref[...].astype(BF16)) + bk_ref[0]


def _adaln(c, ada_w, ada_b, kv_ada_w, kv_ada_b, a_w_out, w_in_t):
    batch = c.shape[0]
    wout_in, wout_out = _cast_specs(a_w_out, 0, ADALN_CAST_CHUNKS)
    win_rows = N_MAIN // ADALN_CAST_CHUNKS

    def cast_chunk(i):
        return jnp.minimum(i, ADALN_CAST_CHUNKS - 1)
    tiles_per_layer = N_MOD * D_MODEL // ADALN_COL_TILE
    ba = ada_b.reshape(N_ADA_TILES, 1, ADALN_COL_TILE)
    bk = kv_ada_b.reshape(N_KV_TILES, 1, ADALN_COL_TILE)

    def ada_idx(i):
        return jnp.minimum(i, N_ADA_TILES - 1)

    def kv_idx(i):
        return jnp.maximum(i - N_ADA_TILES, 0)

    return pl.pallas_call(
        _adaln_kernel,
        grid=(N_ADA_TILES + N_KV_TILES,),
        in_specs=[
            _const_spec((batch, D_MODEL)),
            pl.BlockSpec((1, D_MODEL, ADALN_COL_TILE),
                         lambda i: (ada_idx(i) // tiles_per_layer, 0, ada_idx(i) % tiles_per_layer)),
            pl.BlockSpec((D_MODEL, ADALN_COL_TILE), lambda i: (0, kv_idx(i))),
            pl.BlockSpec((1, 1, ADALN_COL_TILE), lambda i: (ada_idx(i), 0, 0)),
            pl.BlockSpec((1, 1, ADALN_COL_TILE), lambda i: (kv_idx(i), 0, 0)),
            wout_in,
            pl.BlockSpec((1, win_rows, D_MODEL), lambda i: (0, cast_chunk(i), 0)),
            pl.BlockSpec((1, 2 * M_HEADS, D_MODEL), lambda i: (0, N_MAIN // (2 * M_HEADS), 0)),
        ],
        out_specs=[pl.BlockSpec((batch, ADALN_COL_TILE), lambda i: (0, i)), wout_out,
                   pl.BlockSpec((win_rows, D_MODEL), lambda i: (cast_chunk(i), 0)),
                   _const_spec((GATE_PAD, D_MODEL))],
        out_shape=[jax.ShapeDtypeStruct((batch, MOD_WIDTH), F32), _bf16_like(a_w_out),
                   jax.ShapeDtypeStruct((N_MAIN, D_MODEL), BF16), jax.ShapeDtypeStruct((GATE_PAD, D_MODEL), BF16)],
        compiler_params=_compiler_params(),
        name="adaln",
    )(c, ada_w, kv_ada_w, ba, bk, a_w_out, w_in_t, w_in_t)


NQ = M_HEADS * M_QK_DIM
N_MAIN = 2 * NQ + 2 * D_MODEL
GATE_PAD = LANES_V7X
GATE_QUANTS = 3
GATE_PART_ROWS = 32
GATE_EXPAND_COLS = GATE_QUANTS * M_HEADS * LANES_V7X


def _gate_expand_matrix():
    e = np.zeros((GATE_PART_ROWS, GATE_EXPAND_COLS), np.float32)
    for quant in range(GATE_QUANTS):
        for h in range(M_HEADS):
            row = quant * SUBLANES_V7X + M_HEADS + h
            grp = quant * M_HEADS + h
            e[row, grp * LANES_V7X:(grp + 1) * LANES_V7X] = 1.0
    return np.tile(e, (3, 1))


def _segment_scan(x, op, identity):
    pos = lax.broadcasted_iota(jnp.int32, x.shape, 1) % M_CHUNK
    shift = 1
    while shift < M_CHUNK:
        x = op(x, jnp.where(pos >= shift, pltpu.roll(x, shift, axis=1), identity))
        shift *= 2
    return x


def _gate_scan(g_tm, bg_ref):
    L = M_CHUNK
    nc = g_tm.shape[0] // L
    gates_t = jnp.concatenate(
        [g_tm[c * L:(c + 1) * L, :].T[0:2 * M_HEADS, :] for c in range(nc)], axis=1) + bg_ref[...]
    capped = GATE_CAP * jnp.tanh(gates_t / GATE_CAP)
    log_f = jnp.minimum(capped, 0.0) - jnp.log1p(jnp.exp(-jnp.abs(capped)))
    bcum = _segment_scan(log_f * LOG2_E, jnp.add, 0.0)
    a = pltpu.roll(capped * LOG2_E, M_HEADS, axis=0) - bcum
    cmax = _segment_scan(a, jnp.maximum, -jnp.inf)
    stacked = jnp.concatenate([bcum, cmax, a, jnp.zeros_like(a)], axis=0)
    return a, jnp.concatenate(_split3_bf16(stacked), axis=0)


def _gate_expand(parts, gexp_ref):
    L = M_CHUNK
    return [_dot_tn(parts[:, c * L:(c + 1) * L], gexp_ref[...]) for c in range(parts.shape[1] // L)]


def _mlstm_tile(a_rows, cols, qk_s, v_s, h_s, c_s, n_s, m_s):
    L = M_CHUNK
    W = LANES_V7X
    nc = len(cols)
    heads = range(M_HEADS)
    causal = lax.broadcasted_iota(jnp.int32, (L, L), 1) <= lax.broadcasted_iota(jnp.int32, (L, L), 0)
    ones = jnp.ones((L, W), BF16)

    def rows(c):
        return slice(c * L, (c + 1) * L)

    def q_of(h, c):
        return qk_s[rows(c), h * M_QK_DIM:(h + 1) * M_QK_DIM]

    def k_of(h, c):
        return qk_s[rows(c), NQ + h * M_QK_DIM:NQ + (h + 1) * M_QK_DIM]

    def v1_of(h, c):
        return jnp.concatenate([v_s[rows(c), h * M_V_DIM:(h + 1) * M_V_DIM], ones], axis=1)

    def col(c, quant, h):
        g = quant * M_HEADS + h
        return cols[c][:, g * W:(g + 1) * W]

    scores = [[_dot_nt(q_of(h, c), k_of(h, c)) for c in range(nc)] for h in heads]

    m_in = [[None] * nc for _ in heads]
    mx = [[None] * nc for _ in heads]
    for h in heads:
        m_old = m_s[h:h + 1, :]
        for c in range(nc):
            m_in[h][c] = m_old
            mx[h][c] = jnp.maximum(m_old, col(c, 1, h)[L - 1:L, :])
            m_old = col(c, 0, h)[L - 1:L, :] + mx[h][c]
        m_s[h:h + 1, :] = m_old

    deltas = [[_dot_tn((k_of(h, c).astype(F32) * jnp.exp2(col(c, 2, h) - mx[h][c])).astype(BF16), v1_of(h, c))
               for c in range(nc)] for h in heads]

    for h in heads:
        c_t = c_s[h]
        n_bc = n_s[h]
        for c in range(nc):
            m_old = m_in[h][c]
            a_row = a_rows[M_HEADS + h:M_HEADS + h + 1, rows(c)]
            mt = jnp.maximum(col(c, 1, h), m_old)
            w_intra = jnp.exp2(jnp.where(causal, a_row - mt, -jnp.inf))
            w_inter = jnp.exp2(m_old - mt)
            e_neg = jnp.exp2(-col(c, 0, h) - mt)
            s_qk = scores[h][c] * w_intra
            lhs = jnp.concatenate([(q_of(h, c).astype(F32) * w_inter).astype(BF16), s_qk.astype(BF16)], axis=1)
            state = jnp.concatenate([c_t.astype(BF16), n_bc.astype(BF16)], axis=1)
            out = _dot(lhs, jnp.concatenate([state, v1_of(h, c)], axis=0))
            r_den = 1.0 / jnp.maximum(jnp.abs(out[:, M_V_DIM:M_V_DIM + W]), e_neg)
            for half in range(M_V_DIM // W):
                h_s[rows(c), h * M_V_DIM + half * W:h * M_V_DIM + (half + 1) * W] = (
                    out[:, half * W:(half + 1) * W] * r_den)
            decay = jnp.exp2(m_old - mx[h][c])
            c_t = jnp.concatenate([decay] * (M_V_DIM // W), axis=1) * c_t + deltas[h][c][:, 0:M_V_DIM]
            n_bc = decay * n_bc + deltas[h][c][:, M_V_DIM:M_V_DIM + W]
        c_s[h] = c_t
        n_s[h] = n_bc


def _mix0_tile(tiles_per_seq, x_ref, mod_ref, wg_ref, bg_ref, gexp_ref, nw_ref, lng_ref, lnb_ref,
               o_ref, win_s, wout_s, qk_s, v_s, og_s, h_s, c_s, n_s, m_s):
    @pl.when(pl.program_id(0) % tiles_per_seq == 0)
    def _():
        c_s[...] = jnp.zeros_like(c_s)
        n_s[...] = jnp.zeros_like(n_s)
        m_s[...] = jnp.zeros_like(m_s)

    x = x_ref[0]
    u = (x * (1.0 + _mod_slice(mod_ref, 1)) + _mod_slice(mod_ref, 0)).astype(BF16)
    a_rows, parts = _gate_scan(_dot_nt(u, wg_ref[...]), bg_ref)
    qk_s[:, 0:NQ] = _dot_nt(u, win_s[0:NQ, :]).astype(BF16)
    qk_s[:, NQ:2 * NQ] = (_dot_nt(u, win_s[NQ:2 * NQ, :]) * (1.0 / math.sqrt(M_QK_DIM))).astype(BF16)
    v_s[...] = _dot_nt(u, win_s[2 * NQ:2 * NQ + D_MODEL, :]).astype(BF16)
    og_s[...] = _dot_nt(u, win_s[2 * NQ + D_MODEL:N_MAIN, :])

    _mlstm_tile(a_rows, _gate_expand(parts, gexp_ref), qk_s, v_s, h_s, c_s, n_s, m_s)

    normed = []
    for h in range(M_HEADS):
        hh = h_s[:, h * M_V_DIM:(h + 1) * M_V_DIM]
        normed.append(hh * lax.rsqrt(jnp.mean(hh * hh, axis=-1, keepdims=True) + RMS_EPS))
    hn = jnp.concatenate(normed, axis=1)
    gated = hn * nw_ref[...] * jax.nn.sigmoid(og_s[...])
    y = _dot(gated.astype(BF16), wout_s[...])
    r = DEEPNORM_ALPHA * x + (1.0 + _mod_slice(mod_ref, 2)) * y
    o_ref[0] = _layer_norm(r, lng_ref[...], lnb_ref[...])


def _mix0_kernel(tiles_per_seq, x_ref, mod_ref, win_ref, wout_ref, wg_ref, bg_ref, gexp_ref, nw_ref,
                 lng_ref, lnb_ref, up_c, dn_c, kv_c, q_c,
                 o_ref, up_b, dn_b, kv_b, q_b, *scratch):
    up_b[...] = up_c[0].astype(BF16)
    dn_b[...] = dn_c[0].astype(BF16)
    kv_b[...] = kv_c[0].astype(BF16)
    wq = q_c[0]
    q_b[...] = jnp.concatenate(
        [wq[:, hd * A_HEAD_DIM:(hd + 1) * A_HEAD_DIM] for hd in HEAD_ORDER], axis=1).astype(BF16)
    _mix0_tile(tiles_per_seq, x_ref, mod_ref, wg_ref, bg_ref, gexp_ref, nw_ref, lng_ref, lnb_ref,
               o_ref, win_ref, wout_ref, *scratch)


def _mix0(x, mod, w_in, w_out, w_gate, b_gates, gate_expand, norm_w, ln_g, ln_b,
          mlp_w_up, mlp_w_down, w_kv, b_w_q):
    B, S, D = x.shape
    tm = ROW_TILE
    n_t = S // tm
    tiles = (B, n_t)
    casts = [_cast_specs(w, 0, B * n_t) for w in (mlp_w_up, mlp_w_down, w_kv, b_w_q)]
    return pl.pallas_call(
        functools.partial(_mix0_kernel, n_t),
        grid=(B * n_t,),
        in_specs=[
            _row_spec(tm, D, tiles),
            _mod_spec(tiles),
            _const_spec((N_MAIN, D)),
            _const_spec((D, D)),
            _const_spec((GATE_PAD, D)),
            _const_spec((2 * M_HEADS, 1)),
            _const_spec(gate_expand.shape),
            _const_spec((1, D)),
            _const_spec((1, D)),
            _const_spec((1, D)),
            *[c[0] for c in casts],
        ],
        out_specs=[_row_spec(tm, D, tiles), *[c[1] for c in casts]],
        out_shape=[jax.ShapeDtypeStruct((B, S, D), F32),
                   *[_bf16_like(w) for w in (mlp_w_up, mlp_w_down, w_kv, b_w_q)]],
        scratch_shapes=[
            pltpu.VMEM((tm, 2 * NQ), BF16),
            pltpu.VMEM((tm, D_MODEL), BF16),
            pltpu.VMEM((tm, D_MODEL), F32),
            pltpu.VMEM((tm, D_MODEL), F32),
            pltpu.VMEM((M_HEADS, M_QK_DIM, M_V_DIM), F32),
            pltpu.VMEM((M_HEADS, M_QK_DIM, LANES_V7X), F32),
            pltpu.VMEM((SUBLANES_V7X, LANES_V7X), F32),
        ],
        compiler_params=_compiler_params(),
        name="mix0",
    )(x, mod, w_in, w_out, w_gate, b_gates, gate_expand, norm_w, ln_g, ln_b,
      mlp_w_up, mlp_w_down, w_kv, b_w_q)


def _mlp_core(h, mod_ref, layer, wup_s, wdn_s, lng_ref, lnb_ref):
    base = layer * N_MOD
    u = (h * (1.0 + _mod_slice(mod_ref, base + 4)) + _mod_slice(mod_ref, base + 3)).astype(BF16)
    a = jnp.maximum(_dot(u, wup_s[...]), 0.0)
    y = _dot((a * a).astype(BF16), wdn_s[...])
    r = DEEPNORM_ALPHA * h + (1.0 + _mod_slice(mod_ref, base + 5)) * y
    return _layer_norm(r, lng_ref[...], lnb_ref[...])


def _rope_tables(pos_row, invf_ref, expand_ref):
    ang = pos_row.astype(F32) * invf_ref[...]
    trig = jnp.concatenate([jnp.cos(ang), jnp.sin(ang)], axis=0)
    parts = jnp.concatenate(_split3_bf16(trig), axis=0)
    tab = _dot_tn(parts, expand_ref[...])
    lane = lax.broadcasted_iota(jnp.int32, (1, LANES_V7X), 1)
    cos_t = tab[:, 0:LANES_V7X] + jnp.where(lane % A_HEAD_DIM >= ROPE_DIM, 1.0, 0.0)
    return cos_t, tab[:, LANES_V7X:2 * LANES_V7X], tab[:, 2 * LANES_V7X:3 * LANES_V7X]


def _rope(x, tables):
    cos_t, sin_up, sin_dn = tables
    out = []
    for g in range(x.shape[1] // LANES_V7X):
        xg = x[:, g * LANES_V7X:(g + 1) * LANES_V7X]
        from_lo = pltpu.roll(xg, ROPE_HALF, axis=1)
        from_hi = pltpu.roll(xg, LANES_V7X - ROPE_HALF, axis=1)
        out.append(xg * cos_t + from_lo * sin_up + from_hi * sin_dn)
    return jnp.concatenate(out, axis=1)


def _mlp0_kernel(h_ref, mod_ref, modp_ref, posp_ref, wup_ref, wdn_ref, wkv_ref, wq_ref, lng_ref, lnb_ref,
                 invf_ref, expand_ref, up_c, dn_c, wo_c,
                 o_ref, q_ref, k_ref, v_ref, up_b, dn_b, wo_b, r_s):
    @pl.when(pl.program_id(0) == 0)
    def _():
        r_s[...] = jnp.zeros_like(r_s)

    up_b[...] = up_c[0].astype(BF16)
    dn_b[...] = dn_c[0].astype(BF16)
    wo_b[...] = wo_c[0].astype(BF16)

    h = h_ref[0]
    u = (h * (1.0 + _mod_slice(mod_ref, 4)) + _mod_slice(mod_ref, 3)).astype(BF16)
    a = jnp.maximum(_dot(u, wup_ref[...]), 0.0)

    h1 = _layer_norm(r_s[...], lng_ref[...], lnb_ref[...])
    o_ref[0] = h1
    kv_shift = modp_ref[0, :, MOD_KV_BASE:MOD_KV_BASE + D_MODEL]
    kv_scale = modp_ref[0, :, MOD_KV_BASE + D_MODEL:MOD_KV_BASE + 2 * D_MODEL]
    kv = _dot((h1 * (1.0 + kv_scale) + kv_shift).astype(BF16), wkv_ref[...])
    uq = (h1 * (1.0 + _mod_slice(modp_ref, N_MOD + 1)) + _mod_slice(modp_ref, N_MOD)).astype(BF16)
    q = _dot(uq, wq_ref[...])
    tables = _rope_tables(posp_ref[0], invf_ref, expand_ref)
    q_ref[0] = (_rope(q, tables) * (LOG2_E / math.sqrt(A_HEAD_DIM))).astype(BF16)
    k_ref[0] = _rope(kv[:, 0:A_KV_DIM], tables).astype(BF16)
    v_ref[0] = kv[:, A_KV_DIM:2 * A_KV_DIM].astype(BF16)

    y = _dot((a * a).astype(BF16), wdn_ref[...])
    r_s[...] = DEEPNORM_ALPHA * h + (1.0 + _mod_slice(mod_ref, 5)) * y


def _mlp1_kernel(h_ref, mod_ref, wup_ref, wdn_ref, lng_ref, lnb_ref, o_ref):
    o_ref[0] = _mlp_core(h_ref[0], mod_ref, 1, wup_ref, wdn_ref, lng_ref, lnb_ref)


def _mlp0(h, mod, pos, w_up, w_dn, w_kv, w_q, ln_g, ln_b, inv_freq, expand, mlp_w_up, mlp_w_down, b_w_o):
    B, S, D = h.shape
    tm = ROW_TILE
    n_t = S // tm
    tiles = (B, n_t)
    n_tiles = B * n_t
    q_dim = A_HEADS * A_HEAD_DIM

    def pos_index(i):
        b, t = _tile_index(i, tiles, lag=1)
        return b, 0, t

    up_in, up_out = _cast_specs(mlp_w_up, 1, n_tiles)
    dn_in, dn_out = _cast_specs(mlp_w_down, 1, n_tiles)
    _, wo_out = _cast_specs(b_w_o, 0, n_tiles)
    wo_rows = q_dim // n_tiles
    subs = A_HEAD_DIM // wo_rows

    def wo_index(i):
        c = jnp.minimum(i, n_tiles - 1)
        slot, sub = c // subs, c % subs
        g, half = slot // 2, slot % 2
        return 0, (8 * (g // 4) + 4 * half + g % 4) * subs + sub, 0

    return pl.pallas_call(
        _mlp0_kernel,
        grid=(n_tiles + 1,),
        in_specs=[
            _row_spec(tm, D, tiles),
            _mod_spec(tiles),
            _mod_spec(tiles, lag=1),
            pl.BlockSpec((1, 1, tm), pos_index),
            _const_spec((D, D_FF)),
            _const_spec((D_FF, D)),
            _const_spec((D, 2 * A_KV_DIM)),
            _const_spec((D, q_dim)),
            _const_spec((1, D)),
            _const_spec((1, D)),
            _const_spec((ROPE_HALF, 1)),
            _const_spec(expand.shape),
            up_in,
            dn_in,
            pl.BlockSpec((1, wo_rows, D), wo_index),
        ],
        out_specs=[_row_spec(tm, D, tiles, lag=1), _row_spec(tm, q_dim, tiles, lag=1),
                   _row_spec(tm, A_KV_DIM, tiles, lag=1), _row_spec(tm, A_KV_DIM, tiles, lag=1),
                   up_out, dn_out, wo_out],
        out_shape=[
            jax.ShapeDtypeStruct((B, S, D), F32),
            jax.ShapeDtypeStruct((B, S, q_dim), BF16),
            jax.ShapeDtypeStruct((B, S, A_KV_DIM), BF16),
            jax.ShapeDtypeStruct((B, S, A_KV_DIM), BF16),
            _bf16_like(mlp_w_up), _bf16_like(mlp_w_down), _bf16_like(b_w_o),
        ],
        scratch_shapes=[pltpu.VMEM((tm, D), F32)],
        compiler_params=_compiler_params(),
        name="mlp0",
    )(h, mod, mod, pos, w_up, w_dn, w_kv, w_q, ln_g, ln_b, inv_freq, expand, mlp_w_up, mlp_w_down, b_w_o)


def _mlp1(h, mod, w_up, w_dn, ln_g, ln_b):
    B, S, D = h.shape
    tm = ROW_TILE
    n_t = S // tm
    tiles = (B, n_t)
    return pl.pallas_call(
        _mlp1_kernel,
        grid=(B * n_t,),
        in_specs=[
            _row_spec(tm, D, tiles),
            _mod_spec(tiles),
            _const_spec((D, D_FF)),
            _const_spec((D_FF, D)),
            _const_spec((1, D)),
            _const_spec((1, D)),
        ],
        out_specs=_row_spec(tm, D, tiles),
        out_shape=jax.ShapeDtypeStruct((B, S, D), F32),
        compiler_params=_compiler_params(),
        name="mlp1",
    )(h, mod, w_up, w_dn, ln_g, ln_b)


HEAD_ORDER = tuple(8 * (G // 4) + 4 * half + (G % 4) for G in range(A_HEADS // 2) for half in range(2))
KV_PAIRS = A_KV_HEADS // 2


def _attn_pair_operands(k_band, v_band, p, lo_half, ones_bd):
    W = LANES_V7X
    zero = jnp.zeros((), BF16)
    kp = k_band[:, p * W:(p + 1) * W]
    vp = v_band[:, p * W:(p + 1) * W]
    k_bd = jnp.concatenate([jnp.where(lo_half, kp, zero), jnp.where(lo_half, zero, kp)], axis=0)
    v_bd = jnp.concatenate([jnp.where(lo_half, vp, zero), jnp.where(lo_half, zero, vp)], axis=0)
    return k_bd, jnp.concatenate([v_bd, ones_bd], axis=1)


def _attn_softmax_pv(s_all, v_ext, p, start_bias, sink_ref, prev_visible, lo_half_q):
    L = WINDOW
    W = LANES_V7X
    zero = jnp.zeros((), BF16)
    e_rows, sink_rows = [], []
    for j in range(A_GROUP):
        halves, sink_terms = [], []
        for half in range(2):
            c0 = half * 2 * L
            s = jnp.where(prev_visible, s_all[j * L:(j + 1) * L, c0:c0 + L], s_all[j * L:(j + 1) * L, c0 + L:c0 + 2 * L])
            if start_bias is not None:
                s = s + start_bias
            sink = sink_ref[2 * (4 * p + j) + half] * LOG2_E
            mx = jnp.maximum(jnp.max(s, axis=1, keepdims=True), sink)
            e = jnp.exp2(s - mx).astype(BF16)
            halves += [jnp.where(prev_visible, e, zero), jnp.where(prev_visible, zero, e)]
            sink_terms.append(jnp.exp2(sink - mx))
        e_rows.append(jnp.concatenate(halves, axis=1))
        sink_rows.append(jnp.where(lo_half_q, sink_terms[0], sink_terms[1]))
    o_ext = _dot(jnp.concatenate(e_rows, axis=0), v_ext)
    o_all = o_ext[:, 0:W] / (o_ext[:, W:2 * W] + jnp.concatenate(sink_rows, axis=0))
    return [o_all[j * L:(j + 1) * L, :].astype(BF16) for j in range(A_GROUP)]


def _attn_tile(tiles_per_seq, sink_ref, q_ref, k_ref, kprev_ref, v_ref, vprev_ref, h_ref, mod_ref,
               wo_ref, lng_ref, lnb_ref, o_ref, att_s):
    L = WINDOW
    W = LANES_V7X
    tq = q_ref.shape[1]
    prev_visible = lax.broadcasted_iota(jnp.int32, (L, L), 1) > lax.broadcasted_iota(jnp.int32, (L, L), 0)
    seq_start = pl.program_id(0) % tiles_per_seq == 0
    start_bias = jnp.where(prev_visible & seq_start, -jnp.inf, 0.0)
    lo_half = lax.broadcasted_iota(jnp.int32, (2 * L, W), 1) < A_HEAD_DIM
    lo_half_q = lax.broadcasted_iota(jnp.int32, (L, W), 1) < A_HEAD_DIM
    ones_bd = jnp.concatenate([jnp.where(lo_half, 1.0, 0.0), jnp.where(lo_half, 0.0, 1.0)], axis=0).astype(BF16)

    for blk in range(tq // L):
        r0 = blk * L
        if blk == 0:
            k_prev, v_prev, blk_bias = kprev_ref[0], vprev_ref[0], start_bias
        else:
            k_prev, v_prev, blk_bias = k_ref[0, r0 - L:r0, :], v_ref[0, r0 - L:r0, :], None
        k_band = jnp.concatenate([k_prev, k_ref[0, r0:r0 + L, :]], axis=0)
        v_band = jnp.concatenate([v_prev, v_ref[0, r0:r0 + L, :]], axis=0)
        for p in range(KV_PAIRS):
            k_bd, v_ext = _attn_pair_operands(k_band, v_band, p, lo_half, ones_bd)
            lhs = jnp.concatenate(
                [q_ref[0, r0:r0 + L, (4 * p + j) * W:(4 * p + j + 1) * W] for j in range(A_GROUP)], axis=0)
            outs = _attn_softmax_pv(_dot_nt(lhs, k_bd), v_ext, p, blk_bias, sink_ref, prev_visible, lo_half_q)
            for j in range(A_GROUP):
                att_s[r0:r0 + L, (4 * p + j) * W:(4 * p + j + 1) * W] = outs[j]

    y = _dot(att_s[...], wo_ref[...])
    r = DEEPNORM_ALPHA * h_ref[0] + (1.0 + _mod_slice(mod_ref, N_MOD + 2)) * y
    o_ref[0] = _layer_norm(r, lng_ref[...], lnb_ref[...])


def _attn(sinks, q, k, v, h, mod, w_o, ln_g, ln_b):
    B, S, D = h.shape
    tq = ROW_TILE
    n_t = S // tq
    tiles = (B, n_t)
    q_dim = A_HEADS * A_HEAD_DIM
    blocks_per_tile = tq // WINDOW

    def prev_index(i):
        b, t = _tile_index(i, tiles)
        return b, jnp.maximum(t * blocks_per_tile - 1, 0), 0

    return pl.pallas_call(
        functools.partial(_attn_tile, n_t),
        grid=(B * n_t,),
        in_specs=[
            pl.BlockSpec(memory_space=pltpu.SMEM),
            _row_spec(tq, q_dim, tiles),
            _row_spec(tq, A_KV_DIM, tiles),
            pl.BlockSpec((1, WINDOW, A_KV_DIM), prev_index),
            _row_spec(tq, A_KV_DIM, tiles),
            pl.BlockSpec((1, WINDOW, A_KV_DIM), prev_index),
            _row_spec(tq, D, tiles),
            _mod_spec(tiles),
            _const_spec((q_dim, D)),
            _const_spec((1, D)),
            _const_spec((1, D)),
        ],
        out_specs=_row_spec(tq, D, tiles),
        out_shape=jax.ShapeDtypeStruct((B, S, D), F32),
        scratch_shapes=[pltpu.VMEM((tq, q_dim), BF16)],
        compiler_params=_compiler_params(),
        name="attn",
    )(sinks, q, k, k, v, v, h, mod, w_o, ln_g, ln_b)


def _rope_expand_matrix():
    e = np.zeros((2 * ROPE_HALF, 3 * LANES_V7X), np.float32)
    for lane in range(LANES_V7X):
        d = lane % A_HEAD_DIM
        if d < ROPE_HALF:
            e[d, lane] = 1.0
            e[ROPE_HALF + d, 2 * LANES_V7X + lane] = -1.0
        elif d < ROPE_DIM:
            e[d - ROPE_HALF, lane] = 1.0
            e[d, LANES_V7X + lane] = 1.0
    return np.tile(e, (3, 1))


def kernel(x, c, positions, ada_w, ada_b, kv_ada_w, kv_ada_b, a_w_in, a_b_gates, a_norm_w, a_w_out,
           w_kv, b_w_q, b_sinks, b_w_o, mlp_w_up, mlp_w_down, ln_g, ln_b):
    B, S, D = x.shape
    assert D == D_MODEL and S % ROW_TILE == 0 and ROW_TILE % M_CHUNK == 0

    mod, w_out, w_in, w_gate = _adaln(c, ada_w, ada_b, kv_ada_w, kv_ada_b, a_w_out, jnp.swapaxes(a_w_in, 1, 2))
    mod = mod.reshape(B, 1, MOD_WIDTH)

    h, w_up0, w_dn0, w_kv_b, w_q = _mix0(
        x, mod, w_in, w_out, w_gate, a_b_gates[0].reshape(2 * M_HEADS, 1),
        jnp.asarray(_gate_expand_matrix(), BF16), a_norm_w[0].reshape(1, D),
        ln_g[0].reshape(1, D), ln_b[0].reshape(1, D),
        mlp_w_up, mlp_w_down, w_kv[None], b_w_q)

    inv_freq = (ROPE_THETA ** (-jnp.arange(ROPE_HALF, dtype=F32) / ROPE_HALF)).reshape(ROPE_HALF, 1)
    h, q, k, v, w_up1, w_dn1, w_o = _mlp0(
        h, mod, positions.reshape(B, 1, S), w_up0, w_dn0, w_kv_b, w_q, ln_g[1].reshape(1, D),
        ln_b[1].reshape(1, D), inv_freq, jnp.asarray(_rope_expand_matrix(), BF16), mlp_w_up, mlp_w_down, b_w_o)

    h = _attn(b_sinks[0][np.asarray(HEAD_ORDER)], q, k, v, h, mod, w_o, ln_g[2].reshape(1, D), ln_b[2].reshape(1, D))

    return _mlp1(h, mod, w_up1, w_dn1, ln_g[3].reshape(1, D), ln_b[3].reshape(1, D))
```

```python
import functools
import math

import numpy as np
import jax
import jax.numpy as jnp
from jax import lax
from jax.experimental import pallas as pl
from jax.experimental.pallas import tpu as pltpu

F32 = jnp.float32
BF16 = jnp.bfloat16

D_MODEL = 1024
DEPTH = 2
M_HEADS = 4
M_V_DIM = D_MODEL // M_HEADS
M_QK_DIM = M_V_DIM // 2
M_CHUNK = 128
GATE_CAP = 15.0
A_HEADS = 16
A_KV_HEADS = 4
A_GROUP = A_HEADS // A_KV_HEADS
A_HEAD_DIM = 64
A_KV_DIM = A_KV_HEADS * A_HEAD_DIM
WINDOW = 128
ROPE_DIM = A_HEAD_DIM // 4
ROPE_HALF = ROPE_DIM // 2
ROPE_THETA = 500000.0
D_FF = 4 * D_MODEL
DEEPNORM_ALPHA = (2 * DEPTH) ** 0.25
LN_EPS = 1e-5
RMS_EPS = 1e-6
LOG2_E = math.log2(math.e)
N_MOD = 6
MOD_KV_BASE = DEPTH * N_MOD * D_MODEL
MOD_WIDTH = MOD_KV_BASE + 2 * D_MODEL

LANES_V7X = 128
SUBLANES_V7X = 8
VMEM_LIMIT_BYTES_V7X = 56 * 1024 * 1024

ROW_TILE = 512
WIDE_ROW_TILE = 1024
ADALN_COL_TILE = 1024
ADALN_CAST_CHUNKS = 8

def _compiler_params():
    return pltpu.CompilerParams(dimension_semantics=("arbitrary",), vmem_limit_bytes=VMEM_LIMIT_BYTES_V7X)


def _const_spec(shape):
    return pl.BlockSpec(shape, lambda *_: (0,) * len(shape))


def _tile_index(i, tiles, lag=0):
    n_b, n_t = tiles
    j = jnp.clip(i - lag, 0, n_b * n_t - 1)
    return j // n_t, j % n_t


def _row_spec(tm, width, tiles, lag=0):
    def index(i):
        b, t = _tile_index(i, tiles, lag)
        return b, t, 0
    return pl.BlockSpec((1, tm, width), index)


def _mod_spec(tiles, lag=0):
    return pl.BlockSpec((1, 1, MOD_WIDTH), lambda i: (_tile_index(i, tiles, lag)[0], 0, 0))


def _cast_specs(w, layer, n_chunks):
    _, rows, cols = w.shape
    r = rows // n_chunks

    def chunk(i):
        return jnp.minimum(i, n_chunks - 1)

    return (pl.BlockSpec((1, r, cols), lambda i: (layer, chunk(i), 0)),
            pl.BlockSpec((r, cols), lambda i: (chunk(i), 0)))


def _bf16_like(w):
    return jax.ShapeDtypeStruct(w.shape[1:], BF16)


def _layer_norm(r, g, b):
    mu = jnp.mean(r, axis=-1, keepdims=True)
    d = r - mu
    var = jnp.mean(d * d, axis=-1, keepdims=True)
    return d * lax.rsqrt(var + LN_EPS) * g + b


def _mod_slice(mod_ref, idx):
    return mod_ref[0, :, idx * D_MODEL:(idx + 1) * D_MODEL]


def _dot(a, b):
    return jnp.dot(a, b, preferred_element_type=F32)


def _dot_nt(a, b):
    return lax.dot_general(a, b, (((1,), (1,)), ((), ())), preferred_element_type=F32)


def _dot_tn(a, b):
    return lax.dot_general(a, b, (((0,), (0,)), ((), ())), preferred_element_type=F32)


def _split3_bf16(x):
    hi = x.astype(BF16)
    r1 = x - hi.astype(F32)
    mid = r1.astype(BF16)
    lo = (r1 - mid.astype(F32)).astype(BF16)
    return hi, mid, lo


N_ADA_TILES = DEPTH * N_MOD * D_MODEL // ADALN_COL_TILE
N_KV_TILES = 2 * D_MODEL // ADALN_COL_TILE


def _adaln_kernel(c_ref, wa_ref, wk_ref, ba_ref, bk_ref, wout_c, win_c, wg_c, o_ref, wout_b, win_b, wg_b):
    i = pl.program_id(0)
    c = c_ref[...]
    cs = (c * jax.nn.sigmoid(c)).astype(BF16)

    @pl.when(i < ADALN_CAST_CHUNKS)
    def _():
        wout_b[...] = wout_c[0].astype(BF16)
        win_b[...] = win_c[0].astype(BF16)

    @pl.when(i == 0)
    def _():
        pad = jnp.zeros((GATE_PAD - 2 * M_HEADS, D_MODEL), F32)
        wg_b[...] = jnp.concatenate([wg_c[0], pad], axis=0).astype(BF16)

    @pl.when(i < N_ADA_TILES)
    def _():
        o_ref[...] = _dot(cs, wa_ref[0].astype(BF16)) + ba_ref[0]

    @pl.when(i >= N_ADA_TILES)
    def _():
        o_ref[...] = _dot(cs, wk_ref[...].astype(BF16)) + bk_ref[0]


def _adaln(c, ada_w, ada_b, kv_ada_w, kv_ada_b, a_w_out, w_in_t):
    batch = c.shape[0]
    wout_in, wout_out = _cast_specs(a_w_out, 0, ADALN_CAST_CHUNKS)
    win_rows = N_MAIN // ADALN_CAST_CHUNKS

    def cast_chunk(i):
        return jnp.minimum(i, ADALN_CAST_CHUNKS - 1)
    tiles_per_layer = N_MOD * D_MODEL // ADALN_COL_TILE
    ba = ada_b.reshape(N_ADA_TILES, 1, ADALN_COL_TILE)
    bk = kv_ada_b.reshape(N_KV_TILES, 1, ADALN_COL_TILE)

    def ada_idx(i):
        return jnp.minimum(i, N_ADA_TILES - 1)

    def kv_idx(i):
        return jnp.maximum(i - N_ADA_TILES, 0)

    return pl.pallas_call(
        _adaln_kernel,
        grid=(N_ADA_TILES + N_KV_TILES,),
        in_specs=[
            _const_spec((batch, D_MODEL)),
            pl.BlockSpec((1, D_MODEL, ADALN_COL_TILE),
                         lambda i: (ada_idx(i) // tiles_per_layer, 0, ada_idx(i) % tiles_per_layer)),
            pl.BlockSpec((D_MODEL, ADALN_COL_TILE), lambda i: (0, kv_idx(i))),
            pl.BlockSpec((1, 1, ADALN_COL_TILE), lambda i: (ada_idx(i), 0, 0)),
            pl.BlockSpec((1, 1, ADALN_COL_TILE), lambda i: (kv_idx(i), 0, 0)),
            wout_in,
            pl.BlockSpec((1, win_rows, D_MODEL), lambda i: (0, cast_chunk(i), 0)),
            pl.BlockSpec((1, 2 * M_HEADS, D_MODEL), lambda i: (0, N_MAIN // (2 * M_HEADS), 0)),
        ],
        out_specs=[pl.BlockSpec((batch, ADALN_COL_TILE), lambda i: (0, i)), wout_out,
                   pl.BlockSpec((win_rows, D_MODEL), lambda i: (cast_chunk(i), 0)),
                   _const_spec((GATE_PAD, D_MODEL))],
        out_shape=[jax.ShapeDtypeStruct((batch, MOD_WIDTH), F32), _bf16_like(a_w_out),
                   jax.ShapeDtypeStruct((N_MAIN, D_MODEL), BF16), jax.ShapeDtypeStruct((GATE_PAD, D_MODEL), BF16)],
        compiler_params=_compiler_params(),
        name="adaln",
    )(c, ada_w, kv_ada_w, ba, bk, a_w_out, w_in_t, w_in_t)


NQ = M_HEADS * M_QK_DIM
N_MAIN = 2 * NQ + 2 * D_MODEL
GATE_PAD = LANES_V7X
GATE_QUANTS = 3
GATE_PART_ROWS = 32
GATE_EXPAND_COLS = GATE_QUANTS * M_HEADS * LANES_V7X


def _gate_expand_matrix():
    e = np.zeros((GATE_PART_ROWS, GATE_EXPAND_COLS), np.float32)
    for quant in range(GATE_QUANTS):
        for h in range(M_HEADS):
            row = quant * SUBLANES_V7X + M_HEADS + h
            grp = quant * M_HEADS + h
            e[row, grp * LANES_V7X:(grp + 1) * LANES_V7X] = 1.0
    return np.tile(e, (3, 1))


def _segment_scan(x, op, identity):
    pos = lax.broadcasted_iota(jnp.int32, x.shape, 1) % M_CHUNK
    shift = 1
    while shift < M_CHUNK:
        x = op(x, jnp.where(pos >= shift, pltpu.roll(x, shift, axis=1), identity))
        shift *= 2
    return x


def _gate_scan(g_tm, bg_ref):
    L = M_CHUNK
    nc = g_tm.shape[0] // L
    gates_t = jnp.concatenate(
        [g_tm[c * L:(c + 1) * L, :].T[0:2 * M_HEADS, :] for c in range(nc)], axis=1) + bg_ref[...]
    capped = GATE_CAP * jnp.tanh(gates_t / GATE_CAP)
    log_f = jnp.minimum(capped, 0.0) - jnp.log1p(jnp.exp(-jnp.abs(capped)))
    bcum = _segment_scan(log_f * LOG2_E, jnp.add, 0.0)
    a = pltpu.roll(capped * LOG2_E, M_HEADS, axis=0) - bcum
    cmax = _segment_scan(a, jnp.maximum, -jnp.inf)
    stacked = jnp.concatenate([bcum, cmax, a, jnp.zeros_like(a)], axis=0)
    return a, jnp.concatenate(_split3_bf16(stacked), axis=0)


def _gate_expand(parts, gexp_ref):
    L = M_CHUNK
    return [_dot_tn(parts[:, c * L:(c + 1) * L], gexp_ref[...]) for c in range(parts.shape[1] // L)]


def _mlstm_tile(a_rows, cols, qk_s, v_s, h_s, c_s, n_s, m_s):
    L = M_CHUNK
    W = LANES_V7X
    nc = len(cols)
    heads = range(M_HEADS)
    causal = lax.broadcasted_iota(jnp.int32, (L, L), 1) <= lax.broadcasted_iota(jnp.int32, (L, L), 0)
    ones = jnp.ones((L, W), BF16)

    def rows(c):
        return slice(c * L, (c + 1) * L)

    def q_of(h, c):
        return qk_s[rows(c), h * M_QK_DIM:(h + 1) * M_QK_DIM]

    def k_of(h, c):
        return qk_s[rows(c), NQ + h * M_QK_DIM:NQ + (h + 1) * M_QK_DIM]

    def v1_of(h, c):
        return jnp.concatenate([v_s[rows(c), h * M_V_DIM:(h + 1) * M_V_DIM], ones], axis=1)

    def col(c, quant, h):
        g = quant * M_HEADS + h
        return cols[c][:, g * W:(g + 1) * W]

    scores = [[_dot_nt(q_of(h, c), k_of(h, c)) for c in range(nc)] for h in heads]

    m_in = [[None] * nc for _ in heads]
    mx = [[None] * nc for _ in heads]
    for h in heads:
        m_old = m_s[h:h + 1, :]
        for c in range(nc):
            m_in[h][c] = m_old
            mx[h][c] = jnp.maximum(m_old, col(c, 1, h)[L - 1:L, :])
            m_old = col(c, 0, h)[L - 1:L, :] + mx[h][c]
        m_s[h:h + 1, :] = m_old

    deltas = [[_dot_tn((k_of(h, c).astype(F32) * jnp.exp2(col(c, 2, h) - mx[h][c])).astype(BF16), v1_of(h, c))
               for c in range(nc)] for h in heads]

    for h in heads:
        c_t = c_s[h]
        n_bc = n_s[h]
        for c in range(nc):
            m_old = m_in[h][c]
            a_row = a_rows[M_HEADS + h:M_HEADS + h + 1, rows(c)]
            mt = jnp.maximum(col(c, 1, h), m_old)
            w_intra = jnp.exp2(jnp.where(causal, a_row - mt, -jnp.inf))
            w_inter = jnp.exp2(m_old - mt)
            e_neg = jnp.exp2(-col(c, 0, h) - mt)
            s_qk = scores[h][c] * w_intra
            lhs = jnp.concatenate([(q_of(h, c).astype(F32) * w_inter).astype(BF16), s_qk.astype(BF16)], axis=1)
            state = jnp.concatenate([c_t.astype(BF16), n_bc.astype(BF16)], axis=1)
            out = _dot(lhs, jnp.concatenate([state, v1_of(h, c)], axis=0))
            r_den = 1.0 / jnp.maximum(jnp.abs(out[:, M_V_DIM:M_V_DIM + W]), e_neg)
            for half in range(M_V_DIM // W):
                h_s[rows(c), h * M_V_DIM + half * W:h * M_V_DIM + (half + 1) * W] = (
                    out[:, half * W:(half + 1) * W] * r_den)
            decay = jnp.exp2(m_old - mx[h][c])
            c_t = jnp.concatenate([decay] * (M_V_DIM // W), axis=1) * c_t + deltas[h][c][:, 0:M_V_DIM]
            n_bc = decay * n_bc + deltas[h][c][:, M_V_DIM:M_V_DIM + W]
        c_s[h] = c_t
        n_s[h] = n_bc


def _mix0_tile(tiles_per_seq, x_ref, mod_ref, wg_ref, bg_ref, gexp_ref, nw_ref, lng_ref, lnb_ref,
               o_ref, win_s, wout_s, qk_s, v_s, og_s, h_s, c_s, n_s, m_s):
    @pl.when(pl.program_id(0) % tiles_per_seq == 0)
    def _():
        c_s[...] = jnp.zeros_like(c_s)
        n_s[...] = jnp.zeros_like(n_s)
        m_s[...] = jnp.zeros_like(m_s)

    x = x_ref[0]
    u = (x * (1.0 + _mod_slice(mod_ref, 1)) + _mod_slice(mod_ref, 0)).astype(BF16)
    a_rows, parts = _gate_scan(_dot_nt(u, wg_ref[...]), bg_ref)
    qk_s[:, 0:NQ] = _dot_nt(u, win_s[0:NQ, :]).astype(BF16)
    qk_s[:, NQ:2 * NQ] = (_dot_nt(u, win_s[NQ:2 * NQ, :]) * (1.0 / math.sqrt(M_QK_DIM))).astype(BF16)
    v_s[...] = _dot_nt(u, win_s[2 * NQ:2 * NQ + D_MODEL, :]).astype(BF16)
    og_s[...] = _dot_nt(u, win_s[2 * NQ + D_MODEL:N_MAIN, :])

    _mlstm_tile(a_rows, _gate_expand(parts, gexp_ref), qk_s, v_s, h_s, c_s, n_s, m_s)

    normed = []
    for h in range(M_HEADS):
        hh = h_s[:, h * M_V_DIM:(h + 1) * M_V_DIM]
        normed.append(hh * lax.rsqrt(jnp.mean(hh * hh, axis=-1, keepdims=True) + RMS_EPS))
    hn = jnp.concatenate(normed, axis=1)
    gated = hn * nw_ref[...] * jax.nn.sigmoid(og_s[...])
    y = _dot(gated.astype(BF16), wout_s[...])
    r = DEEPNORM_ALPHA * x + (1.0 + _mod_slice(mod_ref, 2)) * y
    o_ref[0] = _layer_norm(r, lng_ref[...], lnb_ref[...])


def _mix0_kernel(tiles_per_seq, x_ref, mod_ref, win_ref, wout_ref, wg_ref, bg_ref, gexp_ref, nw_ref,
                 lng_ref, lnb_ref, up_c, dn_c, kv_c, q_c,
                 o_ref, up_b, dn_b, kv_b, q_b, *scratch):
    up_b[...] = up_c[0].astype(BF16)
    dn_b[...] = dn_c[0].astype(BF16)
    kv_b[...] = kv_c[0].astype(BF16)
    wq = q_c[0]
    q_b[...] = jnp.concatenate(
        [wq[:, hd * A_HEAD_DIM:(hd + 1) * A_HEAD_DIM] for hd in HEAD_ORDER], axis=1).astype(BF16)
    _mix0_tile(tiles_per_seq, x_ref, mod_ref, wg_ref, bg_ref, gexp_ref, nw_ref, lng_ref, lnb_ref,
               o_ref, win_ref, wout_ref, *scratch)


def _mix0(x, mod, w_in, w_out, w_gate, b_gates, gate_expand, norm_w, ln_g, ln_b,
          mlp_w_up, mlp_w_down, w_kv, b_w_q):
    B, S, D = x.shape
    tm = ROW_TILE
    n_t = S // tm
    tiles = (B, n_t)
    casts = [_cast_specs(w, 0, B * n_t) for w in (mlp_w_up, mlp_w_down, w_kv, b_w_q)]
    return pl.pallas_call(
        functools.partial(_mix0_kernel, n_t),
        grid=(B * n_t,),
        in_specs=[
            _row_spec(tm, D, tiles),
            _mod_spec(tiles),
            _const_spec((N_MAIN, D)),
            _const_spec((D, D)),
            _const_spec((GATE_PAD, D)),
            _const_spec((2 * M_HEADS, 1)),
            _const_spec(gate_expand.shape),
            _const_spec((1, D)),
            _const_spec((1, D)),
            _const_spec((1, D)),
            *[c[0] for c in casts],
        ],
        out_specs=[_row_spec(tm, D, tiles), *[c[1] for c in casts]],
        out_shape=[jax.ShapeDtypeStruct((B, S, D), F32),
                   *[_bf16_like(w) for w in (mlp_w_up, mlp_w_down, w_kv, b_w_q)]],
        scratch_shapes=[
            pltpu.VMEM((tm, 2 * NQ), BF16),
            pltpu.VMEM((tm, D_MODEL), BF16),
            pltpu.VMEM((tm, D_MODEL), F32),
            pltpu.VMEM((tm, D_MODEL), F32),
            pltpu.VMEM((M_HEADS, M_QK_DIM, M_V_DIM), F32),
            pltpu.VMEM((M_HEADS, M_QK_DIM, LANES_V7X), F32),
            pltpu.VMEM((SUBLANES_V7X, LANES_V7X), F32),
        ],
        compiler_params=_compiler_params(),
        name="mix0",
    )(x, mod, w_in, w_out, w_gate, b_gates, gate_expand, norm_w, ln_g, ln_b,
      mlp_w_up, mlp_w_down, w_kv, b_w_q)


def _mlp_core(h, mod_ref, layer, wup_s, wdn_s, lng_ref, lnb_ref):
    base = layer * N_MOD
    u = (h * (1.0 + _mod_slice(mod_ref, base + 4)) + _mod_slice(mod_ref, base + 3)).astype(BF16)
    a = jnp.maximum(_dot(u, wup_s[...]), 0.0)
    y = _dot((a * a).astype(BF16), wdn_s[...])
    r = DEEPNORM_ALPHA * h + (1.0 + _mod_slice(mod_ref, base + 5)) * y
    return _layer_norm(r, lng_ref[...], lnb_ref[...])


def _rope_tables(pos_row, invf_ref, expand_ref):
    ang = pos_row.astype(F32) * invf_ref[...]
    trig = jnp.concatenate([jnp.cos(ang), jnp.sin(ang)], axis=0)
    parts = jnp.concatenate(_split3_bf16(trig), axis=0)
    tab = _dot_tn(parts, expand_ref[...])
    lane = lax.broadcasted_iota(jnp.int32, (1, LANES_V7X), 1)
    cos_t = tab[:, 0:LANES_V7X] + jnp.where(lane % A_HEAD_DIM >= ROPE_DIM, 1.0, 0.0)
    return cos_t, tab[:, LANES_V7X:2 * LANES_V7X], tab[:, 2 * LANES_V7X:3 * LANES_V7X]


def _rope(x, tables):
    cos_t, sin_up, sin_dn = tables
    out = []
    for g in range(x.shape[1] // LANES_V7X):
        xg = x[:, g * LANES_V7X:(g + 1) * LANES_V7X]
        from_lo = pltpu.roll(xg, ROPE_HALF, axis=1)
        from_hi = pltpu.roll(xg, LANES_V7X - ROPE_HALF, axis=1)
        out.append(xg * cos_t + from_lo * sin_up + from_hi * sin_dn)
    return jnp.concatenate(out, axis=1)


def _mlp0_kernel(h_ref, mod_ref, modp_ref, posp_ref, wup_ref, wdn_ref, wkv_ref, wq_ref, lng_ref, lnb_ref,
                 invf_ref, expand_ref, up_c, dn_c, wo_c,
                 o_ref, q_ref, k_ref, v_ref, up_b, dn_b, wo_b, r_s):
    @pl.when(pl.program_id(0) == 0)
    def _():
        r_s[...] = jnp.zeros_like(r_s)

    up_b[...] = up_c[0].astype(BF16)
    dn_b[...] = dn_c[0].astype(BF16)
    wo_b[...] = wo_c[0].astype(BF16)

    h = h_ref[0]
    u = (h * (1.0 + _mod_slice(mod_ref, 4)) + _mod_slice(mod_ref, 3)).astype(BF16)
    a = jnp.maximum(_dot(u, wup_ref[...]), 0.0)

    h1 = _layer_norm(r_s[...], lng_ref[...], lnb_ref[...])
    o_ref[0] = h1
    kv_shift = modp_ref[0, :, MOD_KV_BASE:MOD_KV_BASE + D_MODEL]
    kv_scale = modp_ref[0, :, MOD_KV_BASE + D_MODEL:MOD_KV_BASE + 2 * D_MODEL]
    kv = _dot((h1 * (1.0 + kv_scale) + kv_shift).astype(BF16), wkv_ref[...])
    uq = (h1 * (1.0 + _mod_slice(modp_ref, N_MOD + 1)) + _mod_slice(modp_ref, N_MOD)).astype(BF16)
    q = _dot(uq, wq_ref[...])
    tables = _rope_tables(posp_ref[0], invf_ref, expand_ref)
    q_ref[0] = (_rope(q, tables) * (LOG2_E / math.sqrt(A_HEAD_DIM))).astype(BF16)
    k_ref[0] = _rope(kv[:, 0:A_KV_DIM], tables).astype(BF16)
    v_ref[0] = kv[:, A_KV_DIM:2 * A_KV_DIM].astype(BF16)

    y = _dot((a * a).astype(BF16), wdn_ref[...])
    r_s[...] = DEEPNORM_ALPHA * h + (1.0 + _mod_slice(mod_ref, 5)) * y


def _mlp1_kernel(h_ref, mod_ref, wup_ref, wdn_ref, lng_ref, lnb_ref, o_ref):
    o_ref[0] = _mlp_core(h_ref[0], mod_ref, 1, wup_ref, wdn_ref, lng_ref, lnb_ref)


def _mlp0(h, mod, pos, w_up, w_dn, w_kv, w_q, ln_g, ln_b, inv_freq, expand, mlp_w_up, mlp_w_down, b_w_o):
    B, S, D = h.shape
    tm = ROW_TILE
    n_t = S // tm
    tiles = (B, n_t)
    n_tiles = B * n_t
    q_dim = A_HEADS * A_HEAD_DIM

    def pos_index(i):
        b, t = _tile_index(i, tiles, lag=1)
        return b, 0, t

    up_in, up_out = _cast_specs(mlp_w_up, 1, n_tiles)
    dn_in, dn_out = _cast_specs(mlp_w_down, 1, n_tiles)
    _, wo_out = _cast_specs(b_w_o, 0, n_tiles)
    wo_rows = q_dim // n_tiles
    subs = A_HEAD_DIM // wo_rows

    def wo_index(i):
        c = jnp.minimum(i, n_tiles - 1)
        slot, sub = c // subs, c % subs
        g, half = slot // 2, slot % 2
        return 0, (8 * (g // 4) + 4 * half + g % 4) * subs + sub, 0

    return pl.pallas_call(
        _mlp0_kernel,
        grid=(n_tiles + 1,),
        in_specs=[
            _row_spec(tm, D, tiles),
            _mod_spec(tiles),
            _mod_spec(tiles, lag=1),
            pl.BlockSpec((1, 1, tm), pos_index),
            _const_spec((D, D_FF)),
            _const_spec((D_FF, D)),
            _const_spec((D, 2 * A_KV_DIM)),
            _const_spec((D, q_dim)),
            _const_spec((1, D)),
            _const_spec((1, D)),
            _const_spec((ROPE_HALF, 1)),
            _const_spec(expand.shape),
            up_in,
            dn_in,
            pl.BlockSpec((1, wo_rows, D), wo_index),
        ],
        out_specs=[_row_spec(tm, D, tiles, lag=1), _row_spec(tm, q_dim, tiles, lag=1),
                   _row_spec(tm, A_KV_DIM, tiles, lag=1), _row_spec(tm, A_KV_DIM, tiles, lag=1),
                   up_out, dn_out, wo_out],
        out_shape=[
            jax.ShapeDtypeStruct((B, S, D), F32),
            jax.ShapeDtypeStruct((B, S, q_dim), BF16),
            jax.ShapeDtypeStruct((B, S, A_KV_DIM), BF16),
            jax.ShapeDtypeStruct((B, S, A_KV_DIM), BF16),
            _bf16_like(mlp_w_up), _bf16_like(mlp_w_down), _bf16_like(b_w_o),
        ],
        scratch_shapes=[pltpu.VMEM((tm, D), F32)],
        compiler_params=_compiler_params(),
        name="mlp0",
    )(h, mod, mod, pos, w_up, w_dn, w_kv, w_q, ln_g, ln_b, inv_freq, expand, mlp_w_up, mlp_w_down, b_w_o)


def _mlp1(h, mod, w_up, w_dn, ln_g, ln_b):
    B, S, D = h.shape
    tm = WIDE_ROW_TILE
    n_t = S // tm
    tiles = (B, n_t)
    return pl.pallas_call(
        _mlp1_kernel,
        grid=(B * n_t,),
        in_specs=[
            _row_spec(tm, D, tiles),
            _mod_spec(tiles),
            _const_spec((D, D_FF)),
            _const_spec((D_FF, D)),
            _const_spec((1, D)),
            _const_spec((1, D)),
        ],
        out_specs=_row_spec(tm, D, tiles),
        out_shape=jax.ShapeDtypeStruct((B, S, D), F32),
        compiler_params=_compiler_params(),
        name="mlp1",
    )(h, mod, w_up, w_dn, ln_g, ln_b)


HEAD_ORDER = tuple(8 * (G // 4) + 4 * half + (G % 4) for G in range(A_HEADS // 2) for half in range(2))
KV_PAIRS = A_KV_HEADS // 2


def _attn_pair_operands(k_band, v_band, p, lo_half, ones_bd):
    W = LANES_V7X
    zero = jnp.zeros((), BF16)
    kp = k_band[:, p * W:(p + 1) * W]
    vp = v_band[:, p * W:(p + 1) * W]
    k_bd = jnp.concatenate([jnp.where(lo_half, kp, zero), jnp.where(lo_half, zero, kp)], axis=0)
    v_bd = jnp.concatenate([jnp.where(lo_half, vp, zero), jnp.where(lo_half, zero, vp)], axis=0)
    return k_bd, jnp.concatenate([v_bd, ones_bd], axis=1)


def _attn_softmax_pv(s_all, v_ext, p, start_bias, sink_ref, prev_visible, lo_half_q):
    L = WINDOW
    W = LANES_V7X
    zero = jnp.zeros((), BF16)
    e_rows, sink_rows = [], []
    for j in range(A_GROUP):
        halves, sink_terms = [], []
        for half in range(2):
            c0 = half * 2 * L
            s = jnp.where(prev_visible, s_all[j * L:(j + 1) * L, c0:c0 + L], s_all[j * L:(j + 1) * L, c0 + L:c0 + 2 * L])
            if start_bias is not None:
                s = s + start_bias
            sink = sink_ref[2 * (4 * p + j) + half] * LOG2_E
            mx = jnp.maximum(jnp.max(s, axis=1, keepdims=True), sink)
            e = jnp.exp2(s - mx).astype(BF16)
            halves += [jnp.where(prev_visible, e, zero), jnp.where(prev_visible, zero, e)]
            sink_terms.append(jnp.exp2(sink - mx))
        e_rows.append(jnp.concatenate(halves, axis=1))
        sink_rows.append(jnp.where(lo_half_q, sink_terms[0], sink_terms[1]))
    o_ext = _dot(jnp.concatenate(e_rows, axis=0), v_ext)
    o_all = o_ext[:, 0:W] / (o_ext[:, W:2 * W] + jnp.concatenate(sink_rows, axis=0))
    return [o_all[j * L:(j + 1) * L, :].astype(BF16) for j in range(A_GROUP)]


def _attn_tile(tiles_per_seq, sink_ref, q_ref, k_ref, kprev_ref, v_ref, vprev_ref, h_ref, mod_ref,
               wo_ref, lng_ref, lnb_ref, o_ref, att_s):
    L = WINDOW
    W = LANES_V7X
    tq = q_ref.shape[1]
    prev_visible = lax.broadcasted_iota(jnp.int32, (L, L), 1) > lax.broadcasted_iota(jnp.int32, (L, L), 0)
    seq_start = pl.program_id(0) % tiles_per_seq == 0
    start_bias = jnp.where(prev_visible & seq_start, -jnp.inf, 0.0)
    lo_half = lax.broadcasted_iota(jnp.int32, (2 * L, W), 1) < A_HEAD_DIM
    lo_half_q = lax.broadcasted_iota(jnp.int32, (L, W), 1) < A_HEAD_DIM
    ones_bd = jnp.concatenate([jnp.where(lo_half, 1.0, 0.0), jnp.where(lo_half, 0.0, 1.0)], axis=0).astype(BF16)

    for blk in range(tq // L):
        r0 = blk * L
        if blk == 0:
            k_prev, v_prev, blk_bias = kprev_ref[0], vprev_ref[0], start_bias
        else:
            k_prev, v_prev, blk_bias = k_ref[0, r0 - L:r0, :], v_ref[0, r0 - L:r0, :], None
        k_band = jnp.concatenate([k_prev, k_ref[0, r0:r0 + L, :]], axis=0)
        v_band = jnp.concatenate([v_prev, v_ref[0, r0:r0 + L, :]], axis=0)
        for p in range(KV_PAIRS):
            k_bd, v_ext = _attn_pair_operands(k_band, v_band, p, lo_half, ones_bd)
            lhs = jnp.concatenate(
                [q_ref[0, r0:r0 + L, (4 * p + j) * W:(4 * p + j + 1) * W] for j in range(A_GROUP)], axis=0)
            outs = _attn_softmax_pv(_dot_nt(lhs, k_bd), v_ext, p, blk_bias, sink_ref, prev_visible, lo_half_q)
            for j in range(A_GROUP):
                att_s[r0:r0 + L, (4 * p + j) * W:(4 * p + j + 1) * W] = outs[j]

    y = _dot(att_s[...], wo_ref[...])
    r = DEEPNORM_ALPHA * h_ref[0] + (1.0 + _mod_slice(mod_ref, N_MOD + 2)) * y
    o_ref[0] = _layer_norm(r, lng_ref[...], lnb_ref[...])


def _attn(sinks, q, k, v, h, mod, w_o, ln_g, ln_b):
    B, S, D = h.shape
    tq = WIDE_ROW_TILE
    n_t = S // tq
    tiles = (B, n_t)
    q_dim = A_HEADS * A_HEAD_DIM
    blocks_per_tile = tq // WINDOW

    def prev_index(i):
        b, t = _tile_index(i, tiles)
        return b, jnp.maximum(t * blocks_per_tile - 1, 0), 0

    return pl.pallas_call(
        functools.partial(_attn_tile, n_t),
        grid=(B * n_t,),
        in_specs=[
            pl.BlockSpec(memory_space=pltpu.SMEM),
            _row_spec(tq, q_dim, tiles),
            _row_spec(tq, A_KV_DIM, tiles),
            pl.BlockSpec((1, WINDOW, A_KV_DIM), prev_index),
            _row_spec(tq, A_KV_DIM, tiles),
            pl.BlockSpec((1, WINDOW, A_KV_DIM), prev_index),
            _row_spec(tq, D, tiles),
            _mod_spec(tiles),
            _const_spec((q_dim, D)),
            _const_spec((1, D)),
            _const_spec((1, D)),
        ],
        out_specs=_row_spec(tq, D, tiles),
        out_shape=jax.ShapeDtypeStruct((B, S, D), F32),
        scratch_shapes=[pltpu.VMEM((tq, q_dim), BF16)],
        compiler_params=_compiler_params(),
        name="attn",
    )(sinks, q, k, k, v, v, h, mod, w_o, ln_g, ln_b)


def _rope_expand_matrix():
    e = np.zeros((2 * ROPE_HALF, 3 * LANES_V7X), np.float32)
    for lane in range(LANES_V7X):
        d = lane % A_HEAD_DIM
        if d < ROPE_HALF:
            e[d, lane] = 1.0
            e[ROPE_HALF + d, 2 * LANES_V7X + lane] = -1.0
        elif d < ROPE_DIM:
            e[d - ROPE_HALF, lane] = 1.0
            e[d, LANES_V7X + lane] = 1.0
    return np.tile(e, (3, 1))


def kernel(x, c, positions, ada_w, ada_b, kv_ada_w, kv_ada_b, a_w_in, a_b_gates, a_norm_w, a_w_out,
           w_kv, b_w_q, b_sinks, b_w_o, mlp_w_up, mlp_w_down, ln_g, ln_b):
    B, S, D = x.shape
    assert D == D_MODEL and S % WIDE_ROW_TILE == 0 and WIDE_ROW_TILE % ROW_TILE == 0 and ROW_TILE % M_CHUNK == 0

    mod, w_out, w_in, w_gate = _adaln(c, ada_w, ada_b, kv_ada_w, kv_ada_b, a_w_out, jnp.swapaxes(a_w_in, 1, 2))
    mod = mod.reshape(B, 1, MOD_WIDTH)

    h, w_up0, w_dn0, w_kv_b, w_q = _mix0(
        x, mod, w_in, w_out, w_gate, a_b_gates[0].reshape(2 * M_HEADS, 1),
        jnp.asarray(_gate_expand_matrix(), BF16), a_norm_w[0].reshape(1, D),
        ln_g[0].reshape(1, D), ln_b[0].reshape(1, D),
        mlp_w_up, mlp_w_down, w_kv[None], b_w_q)

    inv_freq = (ROPE_THETA ** (-jnp.arange(ROPE_HALF, dtype=F32) / ROPE_HALF)).reshape(ROPE_HALF, 1)
    h, q, k, v, w_up1, w_dn1, w_o = _mlp0(
        h, mod, positions.reshape(B, 1, S), w_up0, w_dn0, w_kv_b, w_q, ln_g[1].reshape(1, D),
        ln_b[1].reshape(1, D), inv_freq, jnp.asarray(_rope_expand_matrix(), BF16), mlp_w_up, mlp_w_down, b_w_o)

    h = _attn(b_sinks[0][np.asarray(HEAD_ORDER)], q, k, v, h, mod, w_o, ln_g[2].reshape(1, D), ln_b[2].reshape(1, D))

    return _mlp1(h, mod, w_up1, w_dn1, ln_g[3].reshape(1, D), ln_b[3].reshape(1, D))
```

```python
import functools
import math

import numpy as np
import jax
import jax.numpy as jnp
from jax import lax
from jax.experimental import pallas as pl
from jax.experimental.pallas import tpu as pltpu

F32 = jnp.float32
BF16 = jnp.bfloat16

D_MODEL = 1024
DEPTH = 2
M_HEADS = 4
M_V_DIM = D_MODEL // M_HEADS
M_QK_DIM = M_V_DIM // 2
M_CHUNK = 128
GATE_CAP = 15.0
A_HEADS = 16
A_KV_HEADS = 4
A_GROUP = A_HEADS // A_KV_HEADS
A_HEAD_DIM = 64
A_KV_DIM = A_KV_HEADS * A_HEAD_DIM
WINDOW = 128
ROPE_DIM = A_HEAD_DIM // 4
ROPE_HALF = ROPE_DIM // 2
ROPE_THETA = 500000.0
D_FF = 4 * D_MODEL
DEEPNORM_ALPHA = (2 * DEPTH) ** 0.25
LN_EPS = 1e-5
RMS_EPS = 1e-6
LOG2_E = math.log2(math.e)
N_MOD = 6
MOD_KV_BASE = DEPTH * N_MOD * D_MODEL
MOD_WIDTH = MOD_KV_BASE + 2 * D_MODEL

LANES_V7X = 128
SUBLANES_V7X = 8
VMEM_LIMIT_BYTES_V7X = 56 * 1024 * 1024

ROW_TILE = 512
WIDE_ROW_TILE = 1024
ADALN_COL_TILE = 1024
ADALN_CAST_CHUNKS = 8

def _compiler_params():
    return pltpu.CompilerParams(dimension_semantics=("arbitrary",), vmem_limit_bytes=VMEM_LIMIT_BYTES_V7X)


def _const_spec(shape):
    return pl.BlockSpec(shape, lambda *_: (0,) * len(shape))


def _tile_index(i, tiles, lag=0):
    n_b, n_t = tiles
    j = jnp.clip(i - lag, 0, n_b * n_t - 1)
    return j // n_t, j % n_t


def _row_spec(tm, width, tiles, lag=0):
    def index(i):
        b, t = _tile_index(i, tiles, lag)
        return b, t, 0
    return pl.BlockSpec((1, tm, width), index)


def _mod_spec(tiles, lag=0):
    return pl.BlockSpec((1, 1, MOD_WIDTH), lambda i: (_tile_index(i, tiles, lag)[0], 0, 0))


def _cast_specs(w, layer, n_chunks):
    _, rows, cols = w.shape
    r = rows // n_chunks

    def chunk(i):
        return jnp.minimum(i, n_chunks - 1)

    return (pl.BlockSpec((1, r, cols), lambda i: (layer, chunk(i), 0)),
            pl.BlockSpec((r, cols), lambda i: (chunk(i), 0)))


def _bf16_like(w):
    return jax.ShapeDtypeStruct(w.shape[1:], BF16)


def _layer_norm(r, g, b):
    mu = jnp.mean(r, axis=-1, keepdims=True)
    d = r - mu
    var = jnp.mean(d * d, axis=-1, keepdims=True)
    return d * lax.rsqrt(var + LN_EPS) * g + b


def _mod_slice(mod_ref, idx):
    return mod_ref[0, :, idx * D_MODEL:(idx + 1) * D_MODEL]


def _dot(a, b):
    return jnp.dot(a, b, preferred_element_type=F32)


def _dot_nt(a, b):
    return lax.dot_general(a, b, (((1,), (1,)), ((), ())), preferred_element_type=F32)


def _dot_tn(a, b):
    return lax.dot_general(a, b, (((0,), (0,)), ((), ())), preferred_element_type=F32)


def _split3_bf16(x):
    hi = x.astype(BF16)
    r1 = x - hi.astype(F32)
    mid = r1.astype(BF16)
    lo = (r1 - mid.astype(F32)).astype(BF16)
    return hi, mid, lo


N_ADA_TILES = DEPTH * N_MOD * D_MODEL // ADALN_COL_TILE
N_KV_TILES = 2 * D_MODEL // ADALN_COL_TILE


def _adaln_kernel(c_ref, wa_ref, wk_ref, ba_ref, bk_ref, wout_c, win_c, wg_c, o_ref, wout_b, win_b, wg_b):
    i = pl.program_id(0)
    c = c_ref[...]
    cs = (c * jax.nn.sigmoid(c)).astype(BF16)

    @pl.when(i < ADALN_CAST_CHUNKS)
    def _():
        wout_b[...] = wout_c[0].astype(BF16)
        win_b[...] = win_c[0].astype(BF16)

    @pl.when(i == 0)
    def _():
        pad = jnp.zeros((GATE_PAD - 2 * M_HEADS, D_MODEL), F32)
        wg_b[...] = jnp.concatenate([wg_c[0], pad], axis=0).astype(BF16)

    @pl.when(i < N_ADA_TILES)
    def _():
        o_ref[...] = _dot(cs, wa_ref[0].astype(BF16)) + ba_ref[0]

    @pl.when(i >= N_ADA_TILES)
    def _():
        o_ref[...] = _dot(cs, wk_ref[...].astype(BF16)) + bk_ref[0]


def _adaln(c, ada_w, ada_b, kv_ada_w, kv_ada_b, a_w_out, w_in_t):
    batch = c.shape[0]
    wout_in, wout_out = _cast_specs(a_w_out, 0, ADALN_CAST_CHUNKS)
    win_rows = N_MAIN // ADALN_CAST_CHUNKS

    def cast_chunk(i):
        return jnp.minimum(i, ADALN_CAST_CHUNKS - 1)
    tiles_per_layer = N_MOD * D_MODEL // ADALN_COL_TILE
    ba = ada_b.reshape(N_ADA_TILES, 1, ADALN_COL_TILE)
    bk = kv_ada_b.reshape(N_KV_TILES, 1, ADALN_COL_TILE)

    def ada_idx(i):
        return jnp.minimum(i, N_ADA_TILES - 1)

    def kv_idx(i):
        return jnp.maximum(i - N_ADA_TILES, 0)

    return pl.pallas_call(
        _adaln_kernel,
        grid=(N_ADA_TILES + N_KV_TILES,),
        in_specs=[
            _const_spec((batch, D_MODEL)),
            pl.BlockSpec((1, D_MODEL, ADALN_COL_TILE),
                         lambda i: (ada_idx(i) // tiles_per_layer, 0, ada_idx(i) % tiles_per_layer)),
            pl.BlockSpec((D_MODEL, ADALN_COL_TILE), lambda i: (0, kv_idx(i))),
            pl.BlockSpec((1, 1, ADALN_COL_TILE), lambda i: (ada_idx(i), 0, 0)),
            pl.BlockSpec((1, 1, ADALN_COL_TILE), lambda i: (kv_idx(i), 0, 0)),
            wout_in,
            pl.BlockSpec((1, win_rows, D_MODEL), lambda i: (0, cast_chunk(i), 0)),
            pl.BlockSpec((1, 2 * M_HEADS, D_MODEL), lambda i: (0, N_MAIN // (2 * M_HEADS), 0)),
        ],
        out_specs=[pl.BlockSpec((batch, ADALN_COL_TILE), lambda i: (0, i)), wout_out,
                   pl.BlockSpec((win_rows, D_MODEL), lambda i: (cast_chunk(i), 0)),
                   _const_spec((GATE_PAD, D_MODEL))],
        out_shape=[jax.ShapeDtypeStruct((batch, MOD_WIDTH), F32), _bf16_like(a_w_out),
                   jax.ShapeDtypeStruct((N_MAIN, D_MODEL), BF16), jax.ShapeDtypeStruct((GATE_PAD, D_MODEL), BF16)],
        compiler_params=_compiler_params(),
        name="adaln",
    )(c, ada_w, kv_ada_w, ba, bk, a_w_out, w_in_t, w_in_t)


NQ = M_HEADS * M_QK_DIM
N_MAIN = 2 * NQ + 2 * D_MODEL
GATE_PAD = LANES_V7X
GATE_QUANTS = 3
GATE_PART_ROWS = 32
GATE_EXPAND_COLS = GATE_QUANTS * M_HEADS * LANES_V7X


def _gate_expand_matrix():
    e = np.zeros((GATE_PART_ROWS, GATE_EXPAND_COLS), np.float32)
    for quant in range(GATE_QUANTS):
        for h in range(M_HEADS):
            row = quant * SUBLANES_V7X + M_HEADS + h
            grp = quant * M_HEADS + h
            e[row, grp * LANES_V7X:(grp + 1) * LANES_V7X] = 1.0
    return np.tile(e, (3, 1))


def _segment_scan(x, op, identity):
    pos = lax.broadcasted_iota(jnp.int32, x.shape, 1) % M_CHUNK
    shift = 1
    while shift < M_CHUNK:
        x = op(x, jnp.where(pos >= shift, pltpu.roll(x, shift, axis=1), identity))
        shift *= 2
    return x


def _gate_scan(g_tm, bg_ref):
    L = M_CHUNK
    nc = g_tm.shape[0] // L
    gates_t = jnp.concatenate(
        [g_tm[c * L:(c + 1) * L, :].T[0:2 * M_HEADS, :] for c in range(nc)], axis=1) + bg_ref[...]
    capped = GATE_CAP * jnp.tanh(gates_t / GATE_CAP)
    log_f = jnp.minimum(capped, 0.0) - jnp.log1p(jnp.exp(-jnp.abs(capped)))
    bcum = _segment_scan(log_f * LOG2_E, jnp.add, 0.0)
    a = pltpu.roll(capped * LOG2_E, M_HEADS, axis=0) - bcum
    cmax = _segment_scan(a, jnp.maximum, -jnp.inf)
    stacked = jnp.concatenate([bcum, cmax, a, jnp.zeros_like(a)], axis=0)
    return a, jnp.concatenate(_split3_bf16(stacked), axis=0)


def _gate_expand(parts, gexp_ref):
    L = M_CHUNK
    return [_dot_tn(parts[:, c * L:(c + 1) * L], gexp_ref[...]) for c in range(parts.shape[1] // L)]


def _mlstm_tile(a_rows, cols, qk_s, v_s, h_s, c_s, n_s, m_s):
    L = M_CHUNK
    W = LANES_V7X
    nc = len(cols)
    heads = range(M_HEADS)
    causal = lax.broadcasted_iota(jnp.int32, (L, L), 1) <= lax.broadcasted_iota(jnp.int32, (L, L), 0)
    ones = jnp.ones((L, W), BF16)

    def rows(c):
        return slice(c * L, (c + 1) * L)

    def q_of(h, c):
        return qk_s[rows(c), h * M_QK_DIM:(h + 1) * M_QK_DIM]

    def k_of(h, c):
        return qk_s[rows(c), NQ + h * M_QK_DIM:NQ + (h + 1) * M_QK_DIM]

    def v1_of(h, c):
        return jnp.concatenate([v_s[rows(c), h * M_V_DIM:(h + 1) * M_V_DIM], ones], axis=1)

    def col(c, quant, h):
        g = quant * M_HEADS + h
        return cols[c][:, g * W:(g + 1) * W]

    scores = [[_dot_nt(q_of(h, c), k_of(h, c)) for c in range(nc)] for h in heads]

    m_in = [[None] * nc for _ in heads]
    mx = [[None] * nc for _ in heads]
    for h in heads:
        m_old = m_s[h:h + 1, :]
        for c in range(nc):
            m_in[h][c] = m_old
            mx[h][c] = jnp.maximum(m_old, col(c, 1, h)[L - 1:L, :])
            m_old = col(c, 0, h)[L - 1:L, :] + mx[h][c]
        m_s[h:h + 1, :] = m_old

    deltas = [[_dot_tn((k_of(h, c).astype(F32) * jnp.exp2(col(c, 2, h) - mx[h][c])).astype(BF16), v1_of(h, c))
               for c in range(nc)] for h in heads]

    for h in heads:
        c_t = c_s[h]
        n_bc = n_s[h]
        for c in range(nc):
            m_old = m_in[h][c]
            a_row = a_rows[M_HEADS + h:M_HEADS + h + 1, rows(c)]
            mt = jnp.maximum(col(c, 1, h), m_old)
            w_intra = jnp.exp2(jnp.where(causal, a_row - mt, -jnp.inf))
            w_inter = jnp.exp2(m_old - mt)
            e_neg = jnp.exp2(-col(c, 0, h) - mt)
            s_qk = scores[h][c] * w_intra
            lhs = jnp.concatenate([(q_of(h, c).astype(F32) * w_inter).astype(BF16), s_qk.astype(BF16)], axis=1)
            state = jnp.concatenate([c_t.astype(BF16), n_bc.astype(BF16)], axis=1)
            out = _dot(lhs, jnp.concatenate([state, v1_of(h, c)], axis=0))
            r_den = 1.0 / jnp.maximum(jnp.abs(out[:, M_V_DIM:M_V_DIM + W]), e_neg)
            for half in range(M_V_DIM // W):
                h_s[rows(c), h * M_V_DIM + half * W:h * M_V_DIM + (half + 1) * W] = (
                    out[:, half * W:(half + 1) * W] * r_den)
            decay = jnp.exp2(m_old - mx[h][c])
            c_t = jnp.concatenate([decay] * (M_V_DIM // W), axis=1) * c_t + deltas[h][c][:, 0:M_V_DIM]
            n_bc = decay * n_bc + deltas[h][c][:, M_V_DIM:M_V_DIM + W]
        c_s[h] = c_t
        n_s[h] = n_bc


def _mix0_tile(tiles_per_seq, x_ref, mod_ref, wg_ref, bg_ref, gexp_ref, nw_ref, lng_ref, lnb_ref,
               o_ref, win_s, wout_s, qk_s, v_s, og_s, h_s, c_s, n_s, m_s):
    @pl.when(pl.program_id(0) % tiles_per_seq == 0)
    def _():
        c_s[...] = jnp.zeros_like(c_s)
        n_s[...] = jnp.zeros_like(n_s)
        m_s[...] = jnp.zeros_like(m_s)

    x = x_ref[0]
    u = (x * (1.0 + _mod_slice(mod_ref, 1)) + _mod_slice(mod_ref, 0)).astype(BF16)
    a_rows, parts = _gate_scan(_dot_nt(u, wg_ref[...]), bg_ref)
    qk_s[:, 0:NQ] = _dot_nt(u, win_s[0:NQ, :]).astype(BF16)
    qk_s[:, NQ:2 * NQ] = (_dot_nt(u, win_s[NQ:2 * NQ, :]) * (1.0 / math.sqrt(M_QK_DIM))).astype(BF16)
    v_s[...] = _dot_nt(u, win_s[2 * NQ:2 * NQ + D_MODEL, :]).astype(BF16)
    og_s[...] = _dot_nt(u, win_s[2 * NQ + D_MODEL:N_MAIN, :])

    _mlstm_tile(a_rows, _gate_expand(parts, gexp_ref), qk_s, v_s, h_s, c_s, n_s, m_s)

    normed = []
    for h in range(M_HEADS):
        hh = h_s[:, h * M_V_DIM:(h + 1) * M_V_DIM]
        normed.append(hh * lax.rsqrt(jnp.mean(hh * hh, axis=-1, keepdims=True) + RMS_EPS))
    hn = jnp.concatenate(normed, axis=1)
    gated = hn * nw_ref[...] * jax.nn.sigmoid(og_s[...])
    y = _dot(gated.astype(BF16), wout_s[...])
    r = DEEPNORM_ALPHA * x + (1.0 + _mod_slice(mod_ref, 2)) * y
    o_ref[0] = _layer_norm(r, lng_ref[...], lnb_ref[...])


def _mix0_kernel(tiles_per_seq, x_ref, mod_ref, win_ref, wout_ref, wg_ref, bg_ref, gexp_ref, nw_ref,
                 lng_ref, lnb_ref, up_c, dn_c, kv_c, q_c,
                 o_ref, up_b, dn_b, kv_b, q_b, *scratch):
    up_b[...] = up_c[0].astype(BF16)
    dn_b[...] = dn_c[0].astype(BF16)
    kv_b[...] = kv_c[0].astype(BF16)
    wq = q_c[0]
    q_b[...] = jnp.concatenate(
        [wq[:, hd * A_HEAD_DIM:(hd + 1) * A_HEAD_DIM] for hd in HEAD_ORDER], axis=1).astype(BF16)
    _mix0_tile(tiles_per_seq, x_ref, mod_ref, wg_ref, bg_ref, gexp_ref, nw_ref, lng_ref, lnb_ref,
               o_ref, win_ref, wout_ref, *scratch)


def _mix0(x, mod, w_in, w_out, w_gate, b_gates, gate_expand, norm_w, ln_g, ln_b,
          mlp_w_up, mlp_w_down, w_kv, b_w_q):
    B, S, D = x.shape
    tm = ROW_TILE
    n_t = S // tm
    tiles = (B, n_t)
    casts = [_cast_specs(w, 0, B * n_t) for w in (mlp_w_up, mlp_w_down, w_kv, b_w_q)]
    return pl.pallas_call(
        functools.partial(_mix0_kernel, n_t),
        grid=(B * n_t,),
        in_specs=[
            _row_spec(tm, D, tiles),
            _mod_spec(tiles),
            _const_spec((N_MAIN, D)),
            _const_spec((D, D)),
            _const_spec((GATE_PAD, D)),
            _const_spec((2 * M_HEADS, 1)),
            _const_spec(gate_expand.shape),
            _const_spec((1, D)),
            _const_spec((1, D)),
            _const_spec((1, D)),
            *[c[0] for c in casts],
        ],
        out_specs=[_row_spec(tm, D, tiles), *[c[1] for c in casts]],
        out_shape=[jax.ShapeDtypeStruct((B, S, D), F32),
                   *[_bf16_like(w) for w in (mlp_w_up, mlp_w_down, w_kv, b_w_q)]],
        scratch_shapes=[
            pltpu.VMEM((tm, 2 * NQ), BF16),
            pltpu.VMEM((tm, D_MODEL), BF16),
            pltpu.VMEM((tm, D_MODEL), F32),
            pltpu.VMEM((tm, D_MODEL), F32),
            pltpu.VMEM((M_HEADS, M_QK_DIM, M_V_DIM), F32),
            pltpu.VMEM((M_HEADS, M_QK_DIM, LANES_V7X), F32),
            pltpu.VMEM((SUBLANES_V7X, LANES_V7X), F32),
        ],
        compiler_params=_compiler_params(),
        name="mix0",
    )(x, mod, w_in, w_out, w_gate, b_gates, gate_expand, norm_w, ln_g, ln_b,
      mlp_w_up, mlp_w_down, w_kv, b_w_q)


def _mlp_core(h, mod_ref, layer, wup_s, wdn_s, lng_ref, lnb_ref):
    base = layer * N_MOD
    u = (h * (1.0 + _mod_slice(mod_ref, base + 4)) + _mod_slice(mod_ref, base + 3)).astype(BF16)
    a = jnp.maximum(_dot(u, wup_s[...]), 0.0)
    y = _dot((a * a).astype(BF16), wdn_s[...])
    r = DEEPNORM_ALPHA * h + (1.0 + _mod_slice(mod_ref, base + 5)) * y
    return _layer_norm(r, lng_ref[...], lnb_ref[...])


def _rope_tables(pos_row, invf_ref, expand_ref):
    ang = pos_row.astype(F32) * invf_ref[...]
    trig = jnp.concatenate([jnp.cos(ang), jnp.sin(ang)], axis=0)
    parts = jnp.concatenate(_split3_bf16(trig), axis=0)
    tab = _dot_tn(parts, expand_ref[...])
    lane = lax.broadcasted_iota(jnp.int32, (1, LANES_V7X), 1)
    cos_t = tab[:, 0:LANES_V7X] + jnp.where(lane % A_HEAD_DIM >= ROPE_DIM, 1.0, 0.0)
    return cos_t, tab[:, LANES_V7X:2 * LANES_V7X], tab[:, 2 * LANES_V7X:3 * LANES_V7X]


def _rope(x, tables):
    cos_t, sin_up, sin_dn = tables
    out = []
    for g in range(x.shape[1] // LANES_V7X):
        xg = x[:, g * LANES_V7X:(g + 1) * LANES_V7X]
        from_lo = pltpu.roll(xg, ROPE_HALF, axis=1)
        from_hi = pltpu.roll(xg, LANES_V7X - ROPE_HALF, axis=1)
        out.append(xg * cos_t + from_lo * sin_up + from_hi * sin_dn)
    return jnp.concatenate(out, axis=1)


def _mlp0_kernel(n_tiles, h_ref, mod_ref, modp_ref, posp_ref, wup_ref, wdn_ref, wkv_ref, wq_ref, lng_ref,
                 lnb_ref, invf_ref, expand_ref, up_c, dn_c, wo_c,
                 o_ref, q_ref, k_ref, v_ref, up_b, dn_b, wo_b, r_s):
    i = pl.program_id(0)

    def close_previous_tile():
        h1 = _layer_norm(r_s[...], lng_ref[...], lnb_ref[...])
        o_ref[0] = h1
        kv_shift = modp_ref[0, :, MOD_KV_BASE:MOD_KV_BASE + D_MODEL]
        kv_scale = modp_ref[0, :, MOD_KV_BASE + D_MODEL:MOD_KV_BASE + 2 * D_MODEL]
        kv = _dot((h1 * (1.0 + kv_scale) + kv_shift).astype(BF16), wkv_ref[...])
        uq = (h1 * (1.0 + _mod_slice(modp_ref, N_MOD + 1)) + _mod_slice(modp_ref, N_MOD)).astype(BF16)
        q = _dot(uq, wq_ref[...])
        tables = _rope_tables(posp_ref[0], invf_ref, expand_ref)
        q_ref[0] = (_rope(q, tables) * (LOG2_E / math.sqrt(A_HEAD_DIM))).astype(BF16)
        k_ref[0] = _rope(kv[:, 0:A_KV_DIM], tables).astype(BF16)
        v_ref[0] = kv[:, A_KV_DIM:2 * A_KV_DIM].astype(BF16)

    @pl.when(i == 0)
    def _():
        r_s[...] = jnp.zeros_like(r_s)

    @pl.when(i < n_tiles)
    def _():
        up_b[...] = up_c[0].astype(BF16)
        dn_b[...] = dn_c[0].astype(BF16)
        wo_b[...] = wo_c[0].astype(BF16)

        h = h_ref[0]
        u = (h * (1.0 + _mod_slice(mod_ref, 4)) + _mod_slice(mod_ref, 3)).astype(BF16)
        a = jnp.maximum(_dot(u, wup_ref[...]), 0.0)
        close_previous_tile()
        y = _dot((a * a).astype(BF16), wdn_ref[...])
        r_s[...] = DEEPNORM_ALPHA * h + (1.0 + _mod_slice(mod_ref, 5)) * y

    @pl.when(i == n_tiles)
    def _():
        close_previous_tile()


def _mlp1_kernel(h_ref, mod_ref, wup_ref, wdn_ref, lng_ref, lnb_ref, o_ref):
    o_ref[0] = _mlp_core(h_ref[0], mod_ref, 1, wup_ref, wdn_ref, lng_ref, lnb_ref)


def _mlp0(h, mod, pos, w_up, w_dn, w_kv, w_q, ln_g, ln_b, inv_freq, expand, mlp_w_up, mlp_w_down, b_w_o):
    B, S, D = h.shape
    tm = ROW_TILE
    n_t = S // tm
    tiles = (B, n_t)
    n_tiles = B * n_t
    q_dim = A_HEADS * A_HEAD_DIM

    def pos_index(i):
        b, t = _tile_index(i, tiles, lag=1)
        return b, 0, t

    up_in, up_out = _cast_specs(mlp_w_up, 1, n_tiles)
    dn_in, dn_out = _cast_specs(mlp_w_down, 1, n_tiles)
    _, wo_out = _cast_specs(b_w_o, 0, n_tiles)
    wo_rows = q_dim // n_tiles
    subs = A_HEAD_DIM // wo_rows

    def wo_index(i):
        c = jnp.minimum(i, n_tiles - 1)
        slot, sub = c // subs, c % subs
        g, half = slot // 2, slot % 2
        return 0, (8 * (g // 4) + 4 * half + g % 4) * subs + sub, 0

    return pl.pallas_call(
        functools.partial(_mlp0_kernel, n_tiles),
        grid=(n_tiles + 1,),
        in_specs=[
            _row_spec(tm, D, tiles),
            _mod_spec(tiles),
            _mod_spec(tiles, lag=1),
            pl.BlockSpec((1, 1, tm), pos_index),
            _const_spec((D, D_FF)),
            _const_spec((D_FF, D)),
            _const_spec((D, 2 * A_KV_DIM)),
            _const_spec((D, q_dim)),
            _const_spec((1, D)),
            _const_spec((1, D)),
            _const_spec((ROPE_HALF, 1)),
            _const_spec(expand.shape),
            up_in,
            dn_in,
            pl.BlockSpec((1, wo_rows, D), wo_index),
        ],
        out_specs=[_row_spec(tm, D, tiles, lag=1), _row_spec(tm, q_dim, tiles, lag=1),
                   _row_spec(tm, A_KV_DIM, tiles, lag=1), _row_spec(tm, A_KV_DIM, tiles, lag=1),
                   up_out, dn_out, wo_out],
        out_shape=[
            jax.ShapeDtypeStruct((B, S, D), F32),
            jax.ShapeDtypeStruct((B, S, q_dim), BF16),
            jax.ShapeDtypeStruct((B, S, A_KV_DIM), BF16),
            jax.ShapeDtypeStruct((B, S, A_KV_DIM), BF16),
            _bf16_like(mlp_w_up), _bf16_like(mlp_w_down), _bf16_like(b_w_o),
        ],
        scratch_shapes=[pltpu.VMEM((tm, D), F32)],
        compiler_params=_compiler_params(),
        name="mlp0",
    )(h, mod, mod, pos, w_up, w_dn, w_kv, w_q, ln_g, ln_b, inv_freq, expand, mlp_w_up, mlp_w_down, b_w_o)


def _mlp1(h, mod, w_up, w_dn, ln_g, ln_b):
    B, S, D = h.shape
    tm = WIDE_ROW_TILE
    n_t = S // tm
    tiles = (B, n_t)
    return pl.pallas_call(
        _mlp1_kernel,
        grid=(B * n_t,),
        in_specs=[
            _row_spec(tm, D, tiles),
            _mod_spec(tiles),
            _const_spec((D, D_FF)),
            _const_spec((D_FF, D)),
            _const_spec((1, D)),
            _const_spec((1, D)),
        ],
        out_specs=_row_spec(tm, D, tiles),
        out_shape=jax.ShapeDtypeStruct((B, S, D), F32),
        compiler_params=_compiler_params(),
        name="mlp1",
    )(h, mod, w_up, w_dn, ln_g, ln_b)


HEAD_ORDER = tuple(8 * (G // 4) + 4 * half + (G % 4) for G in range(A_HEADS // 2) for half in range(2))
KV_PAIRS = A_KV_HEADS // 2


def _attn_pair_operands(k_band, v_band, p, lo_half, ones_bd):
    W = LANES_V7X
    zero = jnp.zeros((), BF16)
    kp = k_band[:, p * W:(p + 1) * W]
    vp = v_band[:, p * W:(p + 1) * W]
    k_bd = jnp.concatenate([jnp.where(lo_half, kp, zero), jnp.where(lo_half, zero, kp)], axis=0)
    v_bd = jnp.concatenate([jnp.where(lo_half, vp, zero), jnp.where(lo_half, zero, vp)], axis=0)
    return k_bd, jnp.concatenate([v_bd, ones_bd], axis=1)


def _attn_softmax_pv(s_all, v_ext, p, start_bias, sink_ref, prev_visible, lo_half_q):
    L = WINDOW
    W = LANES_V7X
    zero = jnp.zeros((), BF16)
    e_rows, sink_rows = [], []
    for j in range(A_GROUP):
        halves, sink_terms = [], []
        for half in range(2):
            c0 = half * 2 * L
            s = jnp.where(prev_visible, s_all[j * L:(j + 1) * L, c0:c0 + L], s_all[j * L:(j + 1) * L, c0 + L:c0 + 2 * L])
            if start_bias is not None:
                s = s + start_bias
            sink = sink_ref[2 * (4 * p + j) + half] * LOG2_E
            mx = jnp.maximum(jnp.max(s, axis=1, keepdims=True), sink)
            e = jnp.exp2(s - mx).astype(BF16)
            halves += [jnp.where(prev_visible, e, zero), jnp.where(prev_visible, zero, e)]
            sink_terms.append(jnp.exp2(sink - mx))
        e_rows.append(jnp.concatenate(halves, axis=1))
        sink_rows.append(jnp.where(lo_half_q, sink_terms[0], sink_terms[1]))
    o_ext = _dot(jnp.concatenate(e_rows, axis=0), v_ext)
    o_all = o_ext[:, 0:W] / (o_ext[:, W:2 * W] + jnp.concatenate(sink_rows, axis=0))
    return [o_all[j * L:(j + 1) * L, :].astype(BF16) for j in range(A_GROUP)]


def _attn_tile(tiles_per_seq, sink_ref, q_ref, k_ref, kprev_ref, v_ref, vprev_ref, h_ref, mod_ref,
               wo_ref, lng_ref, lnb_ref, o_ref, att_s):
    L = WINDOW
    W = LANES_V7X
    tq = q_ref.shape[1]
    prev_visible = lax.broadcasted_iota(jnp.int32, (L, L), 1) > lax.broadcasted_iota(jnp.int32, (L, L), 0)
    seq_start = pl.program_id(0) % tiles_per_seq == 0
    start_bias = jnp.where(prev_visible & seq_start, -jnp.inf, 0.0)
    lo_half = lax.broadcasted_iota(jnp.int32, (2 * L, W), 1) < A_HEAD_DIM
    lo_half_q = lax.broadcasted_iota(jnp.int32, (L, W), 1) < A_HEAD_DIM
    ones_bd = jnp.concatenate([jnp.where(lo_half, 1.0, 0.0), jnp.where(lo_half, 0.0, 1.0)], axis=0).astype(BF16)

    for blk in range(tq // L):
        r0 = blk * L
        if blk == 0:
            k_prev, v_prev, blk_bias = kprev_ref[0], vprev_ref[0], start_bias
        else:
            k_prev, v_prev, blk_bias = k_ref[0, r0 - L:r0, :], v_ref[0, r0 - L:r0, :], None
        k_band = jnp.concatenate([k_prev, k_ref[0, r0:r0 + L, :]], axis=0)
        v_band = jnp.concatenate([v_prev, v_ref[0, r0:r0 + L, :]], axis=0)
        for p in range(KV_PAIRS):
            k_bd, v_ext = _attn_pair_operands(k_band, v_band, p, lo_half, ones_bd)
            lhs = jnp.concatenate(
                [q_ref[0, r0:r0 + L, (4 * p + j) * W:(4 * p + j + 1) * W] for j in range(A_GROUP)], axis=0)
            outs = _attn_softmax_pv(_dot_nt(lhs, k_bd), v_ext, p, blk_bias, sink_ref, prev_visible, lo_half_q)
            for j in range(A_GROUP):
                att_s[r0:r0 + L, (4 * p + j) * W:(4 * p + j + 1) * W] = outs[j]

    y = _dot(att_s[...], wo_ref[...])
    r = DEEPNORM_ALPHA * h_ref[0] + (1.0 + _mod_slice(mod_ref, N_MOD + 2)) * y
    o_ref[0] = _layer_norm(r, lng_ref[...], lnb_ref[...])


def _attn(sinks, q, k, v, h, mod, w_o, ln_g, ln_b):
    B, S, D = h.shape
    tq = WIDE_ROW_TILE
    n_t = S // tq
    tiles = (B, n_t)
    q_dim = A_HEADS * A_HEAD_DIM
    blocks_per_tile = tq // WINDOW

    def prev_index(i):
        b, t = _tile_index(i, tiles)
        return b, jnp.maximum(t * blocks_per_tile - 1, 0), 0

    return pl.pallas_call(
        functools.partial(_attn_tile, n_t),
        grid=(B * n_t,),
        in_specs=[
            pl.BlockSpec(memory_space=pltpu.SMEM),
            _row_spec(tq, q_dim, tiles),
            _row_spec(tq, A_KV_DIM, tiles),
            pl.BlockSpec((1, WINDOW, A_KV_DIM), prev_index),
            _row_spec(tq, A_KV_DIM, tiles),
            pl.BlockSpec((1, WINDOW, A_KV_DIM), prev_index),
            _row_spec(tq, D, tiles),
            _mod_spec(tiles),
            _const_spec((q_dim, D)),
            _const_spec((1, D)),
            _const_spec((1, D)),
        ],
        out_specs=_row_spec(tq, D, tiles),
        out_shape=jax.ShapeDtypeStruct((B, S, D), F32),
        scratch_shapes=[pltpu.VMEM((tq, q_dim), BF16)],
        compiler_params=_compiler_params(),
        name="attn",
    )(sinks, q, k, k, v, v, h, mod, w_o, ln_g, ln_b)


def _rope_expand_matrix():
    e = np.zeros((2 * ROPE_HALF, 3 * LANES_V7X), np.float32)
    for lane in range(LANES_V7X):
        d = lane % A_HEAD_DIM
        if d < ROPE_HALF:
            e[d, lane] = 1.0
            e[ROPE_HALF + d, 2 * LANES_V7X + lane] = -1.0
        elif d < ROPE_DIM:
            e[d - ROPE_HALF, lane] = 1.0
            e[d, LANES_V7X + lane] = 1.0
    return np.tile(e, (3, 1))


def kernel(x, c, positions, ada_w, ada_b, kv_ada_w, kv_ada_b, a_w_in, a_b_gates, a_norm_w, a_w_out,
           w_kv, b_w_q, b_sinks, b_w_o, mlp_w_up, mlp_w_down, ln_g, ln_b):
    B, S, D = x.shape
    assert D == D_MODEL and S % WIDE_ROW_TILE == 0 and WIDE_ROW_TILE % ROW_TILE == 0 and ROW_TILE % M_CHUNK == 0

    mod, w_out, w_in, w_gate = _adaln(c, ada_w, ada_b, kv_ada_w, kv_ada_b, a_w_out, jnp.swapaxes(a_w_in, 1, 2))
    mod = mod.reshape(B, 1, MOD_WIDTH)

    h, w_up0, w_dn0, w_kv_b, w_q = _mix0(
        x, mod, w_in, w_out, w_gate, a_b_gates[0].reshape(2 * M_HEADS, 1),
        jnp.asarray(_gate_expand_matrix(), BF16), a_norm_w[0].reshape(1, D),
        ln_g[0].reshape(1, D), ln_b[0].reshape(1, D),
        mlp_w_up, mlp_w_down, w_kv[None], b_w_q)

    inv_freq = (ROPE_THETA ** (-jnp.arange(ROPE_HALF, dtype=F32) / ROPE_HALF)).reshape(ROPE_HALF, 1)
    h, q, k, v, w_up1, w_dn1, w_o = _mlp0(
        h, mod, positions.reshape(B, 1, S), w_up0, w_dn0, w_kv_b, w_q, ln_g[1].reshape(1, D),
        ln_b[1].reshape(1, D), inv_freq, jnp.asarray(_rope_expand_matrix(), BF16), mlp_w_up, mlp_w_down, b_w_o)

    h = _attn(b_sinks[0][np.asarray(HEAD_ORDER)], q, k, v, h, mod, w_o, ln_g[2].reshape(1, D), ln_b[2].reshape(1, D))

    return _mlp1(h, mod, w_up1, w_dn1, ln_g[3].reshape(1, D), ln_b[3].reshape(1, D))
```

```python
import functools
import math

import numpy as np
import jax
import jax.numpy as jnp
from jax import lax
from jax.experimental import pallas as pl
from jax.experimental.pallas import tpu as pltpu

F32 = jnp.float32
BF16 = jnp.bfloat16

D_MODEL = 1024
DEPTH = 2
M_HEADS = 4
M_V_DIM = D_MODEL // M_HEADS
M_QK_DIM = M_V_DIM // 2
M_CHUNK = 128
GATE_CAP = 15.0
A_HEADS = 16
A_KV_HEADS = 4
A_GROUP = A_HEADS // A_KV_HEADS
A_HEAD_DIM = 64
A_KV_DIM = A_KV_HEADS * A_HEAD_DIM
WINDOW = 128
ROPE_DIM = A_HEAD_DIM // 4
ROPE_HALF = ROPE_DIM // 2
ROPE_THETA = 500000.0
D_FF = 4 * D_MODEL
DEEPNORM_ALPHA = (2 * DEPTH) ** 0.25
LN_EPS = 1e-5
RMS_EPS = 1e-6
LOG2_E = math.log2(math.e)
N_MOD = 6
MOD_KV_BASE = DEPTH * N_MOD * D_MODEL
MOD_WIDTH = MOD_KV_BASE + 2 * D_MODEL

LANES_V7X = 128
SUBLANES_V7X = 8
VMEM_LIMIT_BYTES_V7X = 56 * 1024 * 1024

ROW_TILE = 512
WIDE_ROW_TILE = 1024
ADALN_COL_TILE = 1024
ADALN_CAST_CHUNKS = 8

def _compiler_params():
    return pltpu.CompilerParams(dimension_semantics=("arbitrary",), vmem_limit_bytes=VMEM_LIMIT_BYTES_V7X)


def _const_spec(shape):
    return pl.BlockSpec(shape, lambda *_: (0,) * len(shape))


def _tile_index(i, tiles, lag=0):
    n_b, n_t = tiles
    j = jnp.clip(i - lag, 0, n_b * n_t - 1)
    return j // n_t, j % n_t


def _row_spec(tm, width, tiles, lag=0):
    def index(i):
        b, t = _tile_index(i, tiles, lag)
        return b, t, 0
    return pl.BlockSpec((1, tm, width), index)


def _mod_spec(tiles, lag=0):
    return pl.BlockSpec((1, 1, MOD_WIDTH), lambda i: (_tile_index(i, tiles, lag)[0], 0, 0))


def _cast_specs(w, layer, n_chunks):
    _, rows, cols = w.shape
    r = rows // n_chunks

    def chunk(i):
        return jnp.minimum(i, n_chunks - 1)

    return (pl.BlockSpec((1, r, cols), lambda i: (layer, chunk(i), 0)),
            pl.BlockSpec((r, cols), lambda i: (chunk(i), 0)))


def _bf16_like(w):
    return jax.ShapeDtypeStruct(w.shape[1:], BF16)


def _layer_norm(r, g, b):
    mu = jnp.mean(r, axis=-1, keepdims=True)
    d = r - mu
    var = jnp.mean(d * d, axis=-1, keepdims=True)
    return d * lax.rsqrt(var + LN_EPS) * g + b


def _mod_slice(mod_ref, idx):
    return mod_ref[0, :, idx * D_MODEL:(idx + 1) * D_MODEL]


def _dot(a, b):
    return jnp.dot(a, b, preferred_element_type=F32)


def _dot_nt(a, b):
    return lax.dot_general(a, b, (((1,), (1,)), ((), ())), preferred_element_type=F32)


def _dot_tn(a, b):
    return lax.dot_general(a, b, (((0,), (0,)), ((), ())), preferred_element_type=F32)


def _split3_bf16(x):
    hi = x.astype(BF16)
    r1 = x - hi.astype(F32)
    mid = r1.astype(BF16)
    lo = (r1 - mid.astype(F32)).astype(BF16)
    return hi, mid, lo


N_ADA_TILES = DEPTH * N_MOD * D_MODEL // ADALN_COL_TILE
N_KV_TILES = 2 * D_MODEL // ADALN_COL_TILE


def _adaln_kernel(c_ref, wa_ref, wk_ref, ba_ref, bk_ref, wout_c, win_c, wg_c, o_ref, wout_b, win_b, wg_b):
    i = pl.program_id(0)
    c = c_ref[...]
    cs = (c * jax.nn.sigmoid(c)).astype(BF16)

    @pl.when(i < ADALN_CAST_CHUNKS)
    def _():
        wout_b[...] = wout_c[0].astype(BF16)
        win_b[...] = win_c[0].astype(BF16)

    @pl.when(i == 0)
    def _():
        pad = jnp.zeros((GATE_PAD - 2 * M_HEADS, D_MODEL), F32)
        wg_b[...] = jnp.concatenate([wg_c[0], pad], axis=0).astype(BF16)

    @pl.when(i < N_ADA_TILES)
    def _():
        o_ref[:, 0, :] = _dot(cs, wa_ref[0].astype(BF16)) + ba_ref[0]

    @pl.when(i >= N_ADA_TILES)
    def _():
        o_ref[:, 0, :] = _dot(cs, wk_ref[...].astype(BF16)) + bk_ref[0]


def _adaln(c, ada_w, ada_b, kv_ada_w, kv_ada_b, a_w_out, w_in_t):
    batch = c.shape[0]
    wout_in, wout_out = _cast_specs(a_w_out, 0, ADALN_CAST_CHUNKS)
    win_rows = N_MAIN // ADALN_CAST_CHUNKS

    def cast_chunk(i):
        return jnp.minimum(i, ADALN_CAST_CHUNKS - 1)
    tiles_per_layer = N_MOD * D_MODEL // ADALN_COL_TILE
    ba = ada_b.reshape(N_ADA_TILES, 1, ADALN_COL_TILE)
    bk = kv_ada_b.reshape(N_KV_TILES, 1, ADALN_COL_TILE)

    def ada_idx(i):
        return jnp.minimum(i, N_ADA_TILES - 1)

    def kv_idx(i):
        return jnp.maximum(i - N_ADA_TILES, 0)

    return pl.pallas_call(
        _adaln_kernel,
        grid=(N_ADA_TILES + N_KV_TILES,),
        in_specs=[
            _const_spec((batch, D_MODEL)),
            pl.BlockSpec((1, D_MODEL, ADALN_COL_TILE),
                         lambda i: (ada_idx(i) // tiles_per_layer, 0, ada_idx(i) % tiles_per_layer)),
            pl.BlockSpec((D_MODEL, ADALN_COL_TILE), lambda i: (0, kv_idx(i))),
            pl.BlockSpec((1, 1, ADALN_COL_TILE), lambda i: (ada_idx(i), 0, 0)),
            pl.BlockSpec((1, 1, ADALN_COL_TILE), lambda i: (kv_idx(i), 0, 0)),
            wout_in,
            pl.BlockSpec((1, win_rows, D_MODEL), lambda i: (0, cast_chunk(i), 0)),
            pl.BlockSpec((1, 2 * M_HEADS, D_MODEL), lambda i: (0, N_MAIN // (2 * M_HEADS), 0)),
        ],
        out_specs=[pl.BlockSpec((batch, 1, ADALN_COL_TILE), lambda i: (0, 0, i)), wout_out,
                   pl.BlockSpec((win_rows, D_MODEL), lambda i: (cast_chunk(i), 0)),
                   _const_spec((GATE_PAD, D_MODEL))],
        out_shape=[jax.ShapeDtypeStruct((batch, 1, MOD_WIDTH), F32), _bf16_like(a_w_out),
                   jax.ShapeDtypeStruct((N_MAIN, D_MODEL), BF16), jax.ShapeDtypeStruct((GATE_PAD, D_MODEL), BF16)],
        compiler_params=_compiler_params(),
        name="adaln",
    )(c, ada_w, kv_ada_w, ba, bk, a_w_out, w_in_t, w_in_t)


NQ = M_HEADS * M_QK_DIM
N_MAIN = 2 * NQ + 2 * D_MODEL
GATE_PAD = LANES_V7X
GATE_QUANTS = 3
GATE_PART_ROWS = 32
GATE_EXPAND_COLS = GATE_QUANTS * M_HEADS * LANES_V7X


def _gate_expand_matrix():
    e = np.zeros((GATE_PART_ROWS, GATE_EXPAND_COLS), np.float32)
    for quant in range(GATE_QUANTS):
        for h in range(M_HEADS):
            row = quant * SUBLANES_V7X + M_HEADS + h
            grp = quant * M_HEADS + h
            e[row, grp * LANES_V7X:(grp + 1) * LANES_V7X] = 1.0
    return np.tile(e, (3, 1))


def _segment_scan(x, op, identity):
    pos = lax.broadcasted_iota(jnp.int32, x.shape, 1) % M_CHUNK
    shift = 1
    while shift < M_CHUNK:
        x = op(x, jnp.where(pos >= shift, pltpu.roll(x, shift, axis=1), identity))
        shift *= 2
    return x


def _gate_scan(g_tm):
    L = M_CHUNK
    nc = g_tm.shape[0] // L
    gates_t = jnp.concatenate(
        [g_tm[c * L:(c + 1) * L, :].T[0:2 * M_HEADS, :] for c in range(nc)], axis=1)
    capped = GATE_CAP * jnp.tanh(gates_t / GATE_CAP)
    log_f = jnp.minimum(capped, 0.0) - jnp.log1p(jnp.exp(-jnp.abs(capped)))
    bcum = _segment_scan(log_f * LOG2_E, jnp.add, 0.0)
    a = pltpu.roll(capped * LOG2_E, M_HEADS, axis=0) - bcum
    cmax = _segment_scan(a, jnp.maximum, -jnp.inf)
    stacked = jnp.concatenate([bcum, cmax, a, jnp.zeros_like(a)], axis=0)
    return a, jnp.concatenate(_split3_bf16(stacked), axis=0)


def _gate_expand(parts, gexp_ref):
    L = M_CHUNK
    return [_dot_tn(parts[:, c * L:(c + 1) * L], gexp_ref[...]) for c in range(parts.shape[1] // L)]


def _mlstm_tile(a_rows, cols, qk_s, v_s, h_s, c_s, n_s, m_s):
    L = M_CHUNK
    W = LANES_V7X
    nc = len(cols)
    heads = range(M_HEADS)
    causal = lax.broadcasted_iota(jnp.int32, (L, L), 1) <= lax.broadcasted_iota(jnp.int32, (L, L), 0)
    ones = jnp.ones((L, W), BF16)

    def rows(c):
        return slice(c * L, (c + 1) * L)

    def q_of(h, c):
        return qk_s[rows(c), h * M_QK_DIM:(h + 1) * M_QK_DIM]

    def k_of(h, c):
        return qk_s[rows(c), NQ + h * M_QK_DIM:NQ + (h + 1) * M_QK_DIM]

    def v1_of(h, c):
        return jnp.concatenate([v_s[rows(c), h * M_V_DIM:(h + 1) * M_V_DIM], ones], axis=1)

    def col(c, quant, h):
        g = quant * M_HEADS + h
        return cols[c][:, g * W:(g + 1) * W]

    scores = [[_dot_nt(q_of(h, c), k_of(h, c)) for c in range(nc)] for h in heads]

    m_in = [[None] * nc for _ in heads]
    mx = [[None] * nc for _ in heads]
    for h in heads:
        m_old = m_s[h:h + 1, :]
        for c in range(nc):
            m_in[h][c] = m_old
            mx[h][c] = jnp.maximum(m_old, col(c, 1, h)[L - 1:L, :])
            m_old = col(c, 0, h)[L - 1:L, :] + mx[h][c]
        m_s[h:h + 1, :] = m_old

    deltas = [[_dot_tn((k_of(h, c).astype(F32) * jnp.exp2(col(c, 2, h) - mx[h][c])).astype(BF16), v1_of(h, c))
               for c in range(nc)] for h in heads]

    for h in heads:
        c_t = c_s[h]
        n_bc = n_s[h]
        for c in range(nc):
            m_old = m_in[h][c]
            a_row = a_rows[M_HEADS + h:M_HEADS + h + 1, rows(c)]
            mt = jnp.maximum(col(c, 1, h), m_old)
            w_intra = jnp.exp2(jnp.where(causal, a_row - mt, -jnp.inf))
            w_inter = jnp.exp2(m_old - mt)
            e_neg = jnp.exp2(-col(c, 0, h) - mt)
            s_qk = scores[h][c] * w_intra
            lhs = jnp.concatenate([(q_of(h, c).astype(F32) * w_inter).astype(BF16), s_qk.astype(BF16)], axis=1)
            state = jnp.concatenate([c_t.astype(BF16), n_bc.astype(BF16)], axis=1)
            out = _dot(lhs, jnp.concatenate([state, v1_of(h, c)], axis=0))
            r_den = 1.0 / jnp.maximum(jnp.abs(out[:, M_V_DIM:M_V_DIM + W]), e_neg)
            for half in range(M_V_DIM // W):
                h_s[rows(c), h * M_V_DIM + half * W:h * M_V_DIM + (half + 1) * W] = (
                    out[:, half * W:(half + 1) * W] * r_den)
            decay = jnp.exp2(m_old - mx[h][c])
            c_t = jnp.concatenate([decay] * (M_V_DIM // W), axis=1) * c_t + deltas[h][c][:, 0:M_V_DIM]
            n_bc = decay * n_bc + deltas[h][c][:, M_V_DIM:M_V_DIM + W]
        c_s[h] = c_t
        n_s[h] = n_bc


def _mix0_tile(tiles_per_seq, x_ref, mod_ref, wg_ref, bg_ref, gexp_ref, nw_ref, lng_ref, lnb_ref,
               o_ref, win_s, wout_s, qk_s, v_s, og_s, h_s, c_s, n_s, m_s):
    @pl.when(pl.program_id(0) % tiles_per_seq == 0)
    def _():
        c_s[...] = jnp.zeros_like(c_s)
        n_s[...] = jnp.zeros_like(n_s)
        m_s[...] = jnp.zeros_like(m_s)

    x = x_ref[0]
    u = (x * (1.0 + _mod_slice(mod_ref, 1)) + _mod_slice(mod_ref, 0)).astype(BF16)
    gate_bias = jnp.concatenate([bg_ref[...], jnp.zeros((1, GATE_PAD - 2 * M_HEADS), F32)], axis=1)
    a_rows, parts = _gate_scan(_dot_nt(u, wg_ref[...]) + gate_bias)
    qk_s[:, 0:NQ] = _dot_nt(u, win_s[0:NQ, :]).astype(BF16)
    qk_s[:, NQ:2 * NQ] = (_dot_nt(u, win_s[NQ:2 * NQ, :]) * (1.0 / math.sqrt(M_QK_DIM))).astype(BF16)
    v_s[...] = _dot_nt(u, win_s[2 * NQ:2 * NQ + D_MODEL, :]).astype(BF16)
    og_s[...] = _dot_nt(u, win_s[2 * NQ + D_MODEL:N_MAIN, :])

    _mlstm_tile(a_rows, _gate_expand(parts, gexp_ref), qk_s, v_s, h_s, c_s, n_s, m_s)

    normed = []
    for h in range(M_HEADS):
        hh = h_s[:, h * M_V_DIM:(h + 1) * M_V_DIM]
        normed.append(hh * lax.rsqrt(jnp.mean(hh * hh, axis=-1, keepdims=True) + RMS_EPS))
    hn = jnp.concatenate(normed, axis=1)
    gated = hn * nw_ref[...] * jax.nn.sigmoid(og_s[...])
    y = _dot(gated.astype(BF16), wout_s[...])
    r = DEEPNORM_ALPHA * x + (1.0 + _mod_slice(mod_ref, 2)) * y
    o_ref[0] = _layer_norm(r, lng_ref[...], lnb_ref[...])


def _mix0_kernel(tiles_per_seq, x_ref, mod_ref, win_ref, wout_ref, wg_ref, bg_ref, gexp_ref, nw_ref,
                 lng_ref, lnb_ref, up_c, dn_c, kv_c, q_c,
                 o_ref, up_b, dn_b, kv_b, q_b, *scratch):
    up_b[...] = up_c[0].astype(BF16)
    dn_b[...] = dn_c[0].astype(BF16)
    kv_b[...] = kv_c[0].astype(BF16)
    wq = q_c[0]
    q_b[...] = jnp.concatenate(
        [wq[:, hd * A_HEAD_DIM:(hd + 1) * A_HEAD_DIM] for hd in HEAD_ORDER], axis=1).astype(BF16)
    _mix0_tile(tiles_per_seq, x_ref, mod_ref, wg_ref, bg_ref, gexp_ref, nw_ref, lng_ref, lnb_ref,
               o_ref, win_ref, wout_ref, *scratch)


def _mix0(x, mod, w_in, w_out, w_gate, b_gates, gate_expand, norm_w, ln_g, ln_b,
          mlp_w_up, mlp_w_down, w_kv, b_w_q):
    B, S, D = x.shape
    tm = ROW_TILE
    n_t = S // tm
    tiles = (B, n_t)
    casts = [_cast_specs(w, 0, B * n_t) for w in (mlp_w_up, mlp_w_down, w_kv, b_w_q)]
    return pl.pallas_call(
        functools.partial(_mix0_kernel, n_t),
        grid=(B * n_t,),
        in_specs=[
            _row_spec(tm, D, tiles),
            _mod_spec(tiles),
            _const_spec((N_MAIN, D)),
            _const_spec((D, D)),
            _const_spec((GATE_PAD, D)),
            _const_spec((1, 2 * M_HEADS)),
            _const_spec(gate_expand.shape),
            _const_spec((1, D)),
            _const_spec((1, D)),
            _const_spec((1, D)),
            *[c[0] for c in casts],
        ],
        out_specs=[_row_spec(tm, D, tiles), *[c[1] for c in casts]],
        out_shape=[jax.ShapeDtypeStruct((B, S, D), F32),
                   *[_bf16_like(w) for w in (mlp_w_up, mlp_w_down, w_kv, b_w_q)]],
        scratch_shapes=[
            pltpu.VMEM((tm, 2 * NQ), BF16),
            pltpu.VMEM((tm, D_MODEL), BF16),
            pltpu.VMEM((tm, D_MODEL), F32),
            pltpu.VMEM((tm, D_MODEL), F32),
            pltpu.VMEM((M_HEADS, M_QK_DIM, M_V_DIM), F32),
            pltpu.VMEM((M_HEADS, M_QK_DIM, LANES_V7X), F32),
            pltpu.VMEM((SUBLANES_V7X, LANES_V7X), F32),
        ],
        compiler_params=_compiler_params(),
        name="mix0",
    )(x, mod, w_in, w_out, w_gate, b_gates, gate_expand, norm_w, ln_g, ln_b,
      mlp_w_up, mlp_w_down, w_kv, b_w_q)


def _mlp_core(h, mod_ref, layer, wup_s, wdn_s, lng_ref, lnb_ref):
    base = layer * N_MOD
    u = (h * (1.0 + _mod_slice(mod_ref, base + 4)) + _mod_slice(mod_ref, base + 3)).astype(BF16)
    a = jnp.maximum(_dot(u, wup_s[...]), 0.0)
    y = _dot((a * a).astype(BF16), wdn_s[...])
    r = DEEPNORM_ALPHA * h + (1.0 + _mod_slice(mod_ref, base + 5)) * y
    return _layer_norm(r, lng_ref[...], lnb_ref[...])


def _rope_tables(pos_row, invf_ref, expand_ref):
    ang = pos_row.astype(F32) * invf_ref[...]
    trig = jnp.concatenate([jnp.cos(ang), jnp.sin(ang)], axis=0)
    parts = jnp.concatenate(_split3_bf16(trig), axis=0)
    tab = _dot_tn(parts, expand_ref[...])
    lane = lax.broadcasted_iota(jnp.int32, (1, LANES_V7X), 1)
    cos_t = tab[:, 0:LANES_V7X] + jnp.where(lane % A_HEAD_DIM >= ROPE_DIM, 1.0, 0.0)
    return cos_t, tab[:, LANES_V7X:2 * LANES_V7X], tab[:, 2 * LANES_V7X:3 * LANES_V7X]


def _rope(x, tables):
    cos_t, sin_up, sin_dn = tables
    out = []
    for g in range(x.shape[1] // LANES_V7X):
        xg = x[:, g * LANES_V7X:(g + 1) * LANES_V7X]
        from_lo = pltpu.roll(xg, ROPE_HALF, axis=1)
        from_hi = pltpu.roll(xg, LANES_V7X - ROPE_HALF, axis=1)
        out.append(xg * cos_t + from_lo * sin_up + from_hi * sin_dn)
    return jnp.concatenate(out, axis=1)


def _mlp0_kernel(n_tiles, tiles_per_seq, h_ref, mod_ref, modp_ref, posp_ref, wup_ref, wdn_ref, wkv_ref, wq_ref, lng_ref,
                 lnb_ref, invf_ref, expand_ref, up_c, dn_c, wo_c,
                 o_ref, q_ref, k_ref, v_ref, up_b, dn_b, wo_b, r_s):
    i = pl.program_id(0)

    def close_previous_tile():
        h1 = _layer_norm(r_s[...], lng_ref[...], lnb_ref[...])
        o_ref[0] = h1
        kv_shift = modp_ref[0, :, MOD_KV_BASE:MOD_KV_BASE + D_MODEL]
        kv_scale = modp_ref[0, :, MOD_KV_BASE + D_MODEL:MOD_KV_BASE + 2 * D_MODEL]
        kv = _dot((h1 * (1.0 + kv_scale) + kv_shift).astype(BF16), wkv_ref[...])
        uq = (h1 * (1.0 + _mod_slice(modp_ref, N_MOD + 1)) + _mod_slice(modp_ref, N_MOD)).astype(BF16)
        q = _dot(uq, wq_ref[...])
        batch = jnp.clip(i - 1, 0, n_tiles - 1) // tiles_per_seq
        tables = _rope_tables(posp_ref[pl.ds(batch, 1), :], invf_ref, expand_ref)
        q_ref[0] = (_rope(q, tables) * (LOG2_E / math.sqrt(A_HEAD_DIM))).astype(BF16)
        k_ref[0] = _rope(kv[:, 0:A_KV_DIM], tables).astype(BF16)
        v_ref[0] = kv[:, A_KV_DIM:2 * A_KV_DIM].astype(BF16)

    @pl.when(i == 0)
    def _():
        r_s[...] = jnp.zeros_like(r_s)

    @pl.when(i < n_tiles)
    def _():
        up_b[...] = up_c[0].astype(BF16)
        dn_b[...] = dn_c[0].astype(BF16)
        wo_b[...] = wo_c[0].astype(BF16)

        h = h_ref[0]
        u = (h * (1.0 + _mod_slice(mod_ref, 4)) + _mod_slice(mod_ref, 3)).astype(BF16)
        a = jnp.maximum(_dot(u, wup_ref[...]), 0.0)
        close_previous_tile()
        y = _dot((a * a).astype(BF16), wdn_ref[...])
        r_s[...] = DEEPNORM_ALPHA * h + (1.0 + _mod_slice(mod_ref, 5)) * y

    @pl.when(i == n_tiles)
    def _():
        close_previous_tile()


def _mlp1_kernel(h_ref, mod_ref, wup_ref, wdn_ref, lng_ref, lnb_ref, o_ref):
    o_ref[0] = _mlp_core(h_ref[0], mod_ref, 1, wup_ref, wdn_ref, lng_ref, lnb_ref)


def _mlp0(h, mod, pos, w_up, w_dn, w_kv, w_q, ln_g, ln_b, inv_freq, expand, mlp_w_up, mlp_w_down, b_w_o):
    B, S, D = h.shape
    tm = ROW_TILE
    n_t = S // tm
    tiles = (B, n_t)
    n_tiles = B * n_t
    q_dim = A_HEADS * A_HEAD_DIM

    def pos_index(i):
        return 0, _tile_index(i, tiles, lag=1)[1]

    up_in, up_out = _cast_specs(mlp_w_up, 1, n_tiles)
    dn_in, dn_out = _cast_specs(mlp_w_down, 1, n_tiles)
    _, wo_out = _cast_specs(b_w_o, 0, n_tiles)
    wo_rows = q_dim // n_tiles
    subs = A_HEAD_DIM // wo_rows

    def wo_index(i):
        c = jnp.minimum(i, n_tiles - 1)
        slot, sub = c // subs, c % subs
        g, half = slot // 2, slot % 2
        return 0, (8 * (g // 4) + 4 * half + g % 4) * subs + sub, 0

    return pl.pallas_call(
        functools.partial(_mlp0_kernel, n_tiles, n_t),
        grid=(n_tiles + 1,),
        in_specs=[
            _row_spec(tm, D, tiles),
            _mod_spec(tiles),
            _mod_spec(tiles, lag=1),
            pl.BlockSpec((B, tm), pos_index),
            _const_spec((D, D_FF)),
            _const_spec((D_FF, D)),
            _const_spec((D, 2 * A_KV_DIM)),
            _const_spec((D, q_dim)),
            _const_spec((1, D)),
            _const_spec((1, D)),
            _const_spec((ROPE_HALF, 1)),
            _const_spec(expand.shape),
            up_in,
            dn_in,
            pl.BlockSpec((1, wo_rows, D), wo_index),
        ],
        out_specs=[_row_spec(tm, D, tiles, lag=1), _row_spec(tm, q_dim, tiles, lag=1),
                   _row_spec(tm, A_KV_DIM, tiles, lag=1), _row_spec(tm, A_KV_DIM, tiles, lag=1),
                   up_out, dn_out, wo_out],
        out_shape=[
            jax.ShapeDtypeStruct((B, S, D), F32),
            jax.ShapeDtypeStruct((B, S, q_dim), BF16),
            jax.ShapeDtypeStruct((B, S, A_KV_DIM), BF16),
            jax.ShapeDtypeStruct((B, S, A_KV_DIM), BF16),
            _bf16_like(mlp_w_up), _bf16_like(mlp_w_down), _bf16_like(b_w_o),
        ],
        scratch_shapes=[pltpu.VMEM((tm, D), F32)],
        compiler_params=_compiler_params(),
        name="mlp0",
    )(h, mod, mod, pos, w_up, w_dn, w_kv, w_q, ln_g, ln_b, inv_freq, expand, mlp_w_up, mlp_w_down, b_w_o)


def _mlp1(h, mod, w_up, w_dn, ln_g, ln_b):
    B, S, D = h.shape
    tm = WIDE_ROW_TILE
    n_t = S // tm
    tiles = (B, n_t)
    return pl.pallas_call(
        _mlp1_kernel,
        grid=(B * n_t,),
        in_specs=[
            _row_spec(tm, D, tiles),
            _mod_spec(tiles),
            _const_spec((D, D_FF)),
            _const_spec((D_FF, D)),
            _const_spec((1, D)),
            _const_spec((1, D)),
        ],
        out_specs=_row_spec(tm, D, tiles),
        out_shape=jax.ShapeDtypeStruct((B, S, D), F32),
        compiler_params=_compiler_params(),
        name="mlp1",
    )(h, mod, w_up, w_dn, ln_g, ln_b)


HEAD_ORDER = tuple(8 * (G // 4) + 4 * half + (G % 4) for G in range(A_HEADS // 2) for half in range(2))
KV_PAIRS = A_KV_HEADS // 2


def _attn_pair_operands(k_band, v_band, p, lo_half, ones_bd):
    W = LANES_V7X
    zero = jnp.zeros((), BF16)
    kp = k_band[:, p * W:(p + 1) * W]
    vp = v_band[:, p * W:(p + 1) * W]
    k_bd = jnp.concatenate([jnp.where(lo_half, kp, zero), jnp.where(lo_half, zero, kp)], axis=0)
    v_bd = jnp.concatenate([jnp.where(lo_half, vp, zero), jnp.where(lo_half, zero, vp)], axis=0)
    return k_bd, jnp.concatenate([v_bd, ones_bd], axis=1)


def _attn_softmax_pv(s_all, v_ext, p, start_bias, sink_ref, prev_visible, lo_half_q):
    L = WINDOW
    W = LANES_V7X
    zero = jnp.zeros((), BF16)
    e_rows, sink_rows = [], []
    for j in range(A_GROUP):
        halves, sink_terms = [], []
        for half in range(2):
            c0 = half * 2 * L
            s = jnp.where(prev_visible, s_all[j * L:(j + 1) * L, c0:c0 + L], s_all[j * L:(j + 1) * L, c0 + L:c0 + 2 * L])
            if start_bias is not None:
                s = s + start_bias
            sink = sink_ref[0, HEAD_ORDER[2 * (4 * p + j) + half]] * LOG2_E
            mx = jnp.maximum(jnp.max(s, axis=1, keepdims=True), sink)
            e = jnp.exp2(s - mx).astype(BF16)
            halves += [jnp.where(prev_visible, e, zero), jnp.where(prev_visible, zero, e)]
            sink_terms.append(jnp.exp2(sink - mx))
        e_rows.append(jnp.concatenate(halves, axis=1))
        sink_rows.append(jnp.where(lo_half_q, sink_terms[0], sink_terms[1]))
    o_ext = _dot(jnp.concatenate(e_rows, axis=0), v_ext)
    o_all = o_ext[:, 0:W] / (o_ext[:, W:2 * W] + jnp.concatenate(sink_rows, axis=0))
    return [o_all[j * L:(j + 1) * L, :].astype(BF16) for j in range(A_GROUP)]


def _attn_tile(tiles_per_seq, sink_ref, q_ref, k_ref, kprev_ref, v_ref, vprev_ref, h_ref, mod_ref,
               wo_ref, lng_ref, lnb_ref, o_ref, att_s):
    L = WINDOW
    W = LANES_V7X
    tq = q_ref.shape[1]
    prev_visible = lax.broadcasted_iota(jnp.int32, (L, L), 1) > lax.broadcasted_iota(jnp.int32, (L, L), 0)
    seq_start = pl.program_id(0) % tiles_per_seq == 0
    start_bias = jnp.where(prev_visible & seq_start, -jnp.inf, 0.0)
    lo_half = lax.broadcasted_iota(jnp.int32, (2 * L, W), 1) < A_HEAD_DIM
    lo_half_q = lax.broadcasted_iota(jnp.int32, (L, W), 1) < A_HEAD_DIM
    ones_bd = jnp.concatenate([jnp.where(lo_half, 1.0, 0.0), jnp.where(lo_half, 0.0, 1.0)], axis=0).astype(BF16)

    for blk in range(tq // L):
        r0 = blk * L
        if blk == 0:
            k_prev, v_prev, blk_bias = kprev_ref[0], vprev_ref[0], start_bias
        else:
            k_prev, v_prev, blk_bias = k_ref[0, r0 - L:r0, :], v_ref[0, r0 - L:r0, :], None
        k_band = jnp.concatenate([k_prev, k_ref[0, r0:r0 + L, :]], axis=0)
        v_band = jnp.concatenate([v_prev, v_ref[0, r0:r0 + L, :]], axis=0)
        for p in range(KV_PAIRS):
            k_bd, v_ext = _attn_pair_operands(k_band, v_band, p, lo_half, ones_bd)
            lhs = jnp.concatenate(
                [q_ref[0, r0:r0 + L, (4 * p + j) * W:(4 * p + j + 1) * W] for j in range(A_GROUP)], axis=0)
            outs = _attn_softmax_pv(_dot_nt(lhs, k_bd), v_ext, p, blk_bias, sink_ref, prev_visible, lo_half_q)
            for j in range(A_GROUP):
                att_s[r0:r0 + L, (4 * p + j) * W:(4 * p + j + 1) * W] = outs[j]

    y = _dot(att_s[...], wo_ref[...])
    r = DEEPNORM_ALPHA * h_ref[0] + (1.0 + _mod_slice(mod_ref, N_MOD + 2)) * y
    o_ref[0] = _layer_norm(r, lng_ref[...], lnb_ref[...])


def _attn(sinks, q, k, v, h, mod, w_o, ln_g, ln_b):
    B, S, D = h.shape
    tq = WIDE_ROW_TILE
    n_t = S // tq
    tiles = (B, n_t)
    q_dim = A_HEADS * A_HEAD_DIM
    blocks_per_tile = tq // WINDOW

    def prev_index(i):
        b, t = _tile_index(i, tiles)
        return b, jnp.maximum(t * blocks_per_tile - 1, 0), 0

    return pl.pallas_call(
        functools.partial(_attn_tile, n_t),
        grid=(B * n_t,),
        in_specs=[
            pl.BlockSpec(memory_space=pltpu.SMEM),
            _row_spec(tq, q_dim, tiles),
            _row_spec(tq, A_KV_DIM, tiles),
            pl.BlockSpec((1, WINDOW, A_KV_DIM), prev_index),
            _row_spec(tq, A_KV_DIM, tiles),
            pl.BlockSpec((1, WINDOW, A_KV_DIM), prev_index),
            _row_spec(tq, D, tiles),
            _mod_spec(tiles),
            _const_spec((q_dim, D)),
            _const_spec((1, D)),
            _const_spec((1, D)),
        ],
        out_specs=_row_spec(tq, D, tiles),
        out_shape=jax.ShapeDtypeStruct((B, S, D), F32),
        scratch_shapes=[pltpu.VMEM((tq, q_dim), BF16)],
        compiler_params=_compiler_params(),
        name="attn",
    )(sinks, q, k, k, v, v, h, mod, w_o, ln_g, ln_b)


def _rope_expand_matrix():
    e = np.zeros((2 * ROPE_HALF, 3 * LANES_V7X), np.float32)
    for lane in range(LANES_V7X):
        d = lane % A_HEAD_DIM
        if d < ROPE_HALF:
            e[d, lane] = 1.0
            e[ROPE_HALF + d, 2 * LANES_V7X + lane] = -1.0
        elif d < ROPE_DIM:
            e[d - ROPE_HALF, lane] = 1.0
            e[d, LANES_V7X + lane] = 1.0
    return np.tile(e, (3, 1))


def kernel(x, c, positions, ada_w, ada_b, kv_ada_w, kv_ada_b, a_w_in, a_b_gates, a_norm_w, a_w_out,
           w_kv, b_w_q, b_sinks, b_w_o, mlp_w_up, mlp_w_down, ln_g, ln_b):
    B, S, D = x.shape
    assert D == D_MODEL and S % WIDE_ROW_TILE == 0 and WIDE_ROW_TILE % ROW_TILE == 0 and ROW_TILE % M_CHUNK == 0

    mod, w_out, w_in, w_gate = _adaln(c, ada_w, ada_b, kv_ada_w, kv_ada_b, a_w_out, jnp.swapaxes(a_w_in, 1, 2))

    h, w_up0, w_dn0, w_kv_b, w_q = _mix0(
        x, mod, w_in, w_out, w_gate, a_b_gates,
        jnp.asarray(_gate_expand_matrix(), BF16), a_norm_w[0].reshape(1, D),
        ln_g[0].reshape(1, D), ln_b[0].reshape(1, D),
        mlp_w_up, mlp_w_down, w_kv[None], b_w_q)

    inv_freq = (ROPE_THETA ** (-jnp.arange(ROPE_HALF, dtype=F32) / ROPE_HALF)).reshape(ROPE_HALF, 1)
    h, q, k, v, w_up1, w_dn1, w_o = _mlp0(
        h, mod, positions, w_up0, w_dn0, w_kv_b, w_q, ln_g[1].reshape(1, D),
        ln_b[1].reshape(1, D), inv_freq, jnp.asarray(_rope_expand_matrix(), BF16), mlp_w_up, mlp_w_down, b_w_o)

    h = _attn(b_sinks, q, k, v, h, mod, w_o, ln_g[2].reshape(1, D), ln_b[2].reshape(1, D))

    return _mlp1(h, mod, w_up1, w_dn1, ln_g[3].reshape(1, D), ln_b[3].reshape(1, D))
```

```python
import functools
import math

import numpy as np
import jax
import jax.numpy as jnp
from jax import lax
from jax.experimental import pallas as pl
from jax.experimental.pallas import tpu as pltpu

F32 = jnp.float32
BF16 = jnp.bfloat16

D_MODEL = 1024
DEPTH = 2
M_HEADS = 4
M_V_DIM = D_MODEL // M_HEADS
M_QK_DIM = M_V_DIM // 2
M_CHUNK = 128
GATE_CAP = 15.0
A_HEADS = 16
A_KV_HEADS = 4
A_GROUP = A_HEADS // A_KV_HEADS
A_HEAD_DIM = 64
A_KV_DIM = A_KV_HEADS * A_HEAD_DIM
WINDOW = 128
ROPE_DIM = A_HEAD_DIM // 4
ROPE_HALF = ROPE_DIM // 2
ROPE_THETA = 500000.0
D_FF = 4 * D_MODEL
DEEPNORM_ALPHA = (2 * DEPTH) ** 0.25
LN_EPS = 1e-5
RMS_EPS = 1e-6
LOG2_E = math.log2(math.e)
N_MOD = 6
MOD_KV_BASE = DEPTH * N_MOD * D_MODEL
MOD_WIDTH = MOD_KV_BASE + 2 * D_MODEL

LANES_V7X = 128
SUBLANES_V7X = 8
VMEM_LIMIT_BYTES_V7X = 56 * 1024 * 1024

ROW_TILE = 512
WIDE_ROW_TILE = 1024
ADALN_COL_TILE = 1024
ADALN_CAST_CHUNKS = 8

def _compiler_params():
    return pltpu.CompilerParams(dimension_semantics=("arbitrary",), vmem_limit_bytes=VMEM_LIMIT_BYTES_V7X)


def _const_spec(shape):
    return pl.BlockSpec(shape, lambda *_: (0,) * len(shape))


def _tile_index(i, tiles, lag=0):
    n_b, n_t = tiles
    j = jnp.clip(i - lag, 0, n_b * n_t - 1)
    return j // n_t, j % n_t


def _row_spec(tm, width, tiles, lag=0):
    def index(i):
        b, t = _tile_index(i, tiles, lag)
        return b, t, 0
    return pl.BlockSpec((1, tm, width), index)


def _mod_spec(tiles, lag=0):
    return pl.BlockSpec((1, 1, MOD_WIDTH), lambda i: (_tile_index(i, tiles, lag)[0], 0, 0))


def _cast_specs(w, layer, n_chunks):
    _, rows, cols = w.shape
    r = rows // n_chunks

    def chunk(i):
        return jnp.minimum(i, n_chunks - 1)

    return (pl.BlockSpec((1, r, cols), lambda i: (layer, chunk(i), 0)),
            pl.BlockSpec((r, cols), lambda i: (chunk(i), 0)))


def _bf16_like(w):
    return jax.ShapeDtypeStruct(w.shape[1:], BF16)


def _layer_norm(r, g, b):
    mu = jnp.mean(r, axis=-1, keepdims=True)
    d = r - mu
    var = jnp.mean(d * d, axis=-1, keepdims=True)
    return d * lax.rsqrt(var + LN_EPS) * g + b


def _mod_slice(mod_ref, idx):
    return mod_ref[0, :, idx * D_MODEL:(idx + 1) * D_MODEL]


def _dot(a, b):
    return jnp.dot(a, b, preferred_element_type=F32)


def _dot_nt(a, b):
    return lax.dot_general(a, b, (((1,), (1,)), ((), ())), preferred_element_type=F32)


def _dot_tn(a, b):
    return lax.dot_general(a, b, (((0,), (0,)), ((), ())), preferred_element_type=F32)


def _split3_bf16(x):
    hi = x.astype(BF16)
    r1 = x - hi.astype(F32)
    mid = r1.astype(BF16)
    lo = (r1 - mid.astype(F32)).astype(BF16)
    return hi, mid, lo


N_ADA_TILES = DEPTH * N_MOD * D_MODEL // ADALN_COL_TILE
N_KV_TILES = 2 * D_MODEL // ADALN_COL_TILE


def _adaln_kernel(c_ref, wa_ref, wk_ref, ba_ref, bk_ref, wout_c, win_c, wg_c, o_ref, wout_b, win_b, wg_b):
    i = pl.program_id(0)
    c = c_ref[...]
    cs = (c * jax.nn.sigmoid(c)).astype(BF16)

    @pl.when(i < ADALN_CAST_CHUNKS)
    def _():
        wout_b[...] = wout_c[0].astype(BF16)
        win_b[...] = win_c[0].astype(BF16)

    @pl.when(i == 0)
    def _():
        pad = jnp.zeros((GATE_PAD - 2 * M_HEADS, D_MODEL), F32)
        wg_b[...] = jnp.concatenate([wg_c[0], pad], axis=0).astype(BF16)

    @pl.when(i < N_ADA_TILES)
    def _():
        o_ref[:, 0, :] = _dot(cs, wa_ref[0].astype(BF16)) + ba_ref[0]

    @pl.when(i >= N_ADA_TILES)
    def _():
        o_ref[:, 0, :] = _dot(cs, wk_ref[...].astype(BF16)) + bk_ref[0]


def _adaln(c, ada_w, ada_b, kv_ada_w, kv_ada_b, a_w_out, w_in_t):
    batch = c.shape[0]
    wout_in, wout_out = _cast_specs(a_w_out, 0, ADALN_CAST_CHUNKS)
    win_rows = N_MAIN // ADALN_CAST_CHUNKS

    def cast_chunk(i):
        return jnp.minimum(i, ADALN_CAST_CHUNKS - 1)
    tiles_per_layer = N_MOD * D_MODEL // ADALN_COL_TILE
    ba = ada_b.reshape(N_ADA_TILES, 1, ADALN_COL_TILE)
    bk = kv_ada_b.reshape(N_KV_TILES, 1, ADALN_COL_TILE)

    def ada_idx(i):
        return jnp.minimum(i, N_ADA_TILES - 1)

    def kv_idx(i):
        return jnp.maximum(i - N_ADA_TILES, 0)

    return pl.pallas_call(
        _adaln_kernel,
        grid=(N_ADA_TILES + N_KV_TILES,),
        in_specs=[
            _const_spec((batch, D_MODEL)),
            pl.BlockSpec((1, D_MODEL, ADALN_COL_TILE),
                         lambda i: (ada_idx(i) // tiles_per_layer, 0, ada_idx(i) % tiles_per_layer)),
            pl.BlockSpec((D_MODEL, ADALN_COL_TILE), lambda i: (0, kv_idx(i))),
            pl.BlockSpec((1, 1, ADALN_COL_TILE), lambda i: (ada_idx(i), 0, 0)),
            pl.BlockSpec((1, 1, ADALN_COL_TILE), lambda i: (kv_idx(i), 0, 0)),
            wout_in,
            pl.BlockSpec((1, win_rows, D_MODEL), lambda i: (0, cast_chunk(i), 0)),
            pl.BlockSpec((1, 2 * M_HEADS, D_MODEL), lambda i: (0, N_MAIN // (2 * M_HEADS), 0)),
        ],
        out_specs=[pl.BlockSpec((batch, 1, ADALN_COL_TILE), lambda i: (0, 0, i)), wout_out,
                   pl.BlockSpec((win_rows, D_MODEL), lambda i: (cast_chunk(i), 0)),
                   _const_spec((GATE_PAD, D_MODEL))],
        out_shape=[jax.ShapeDtypeStruct((batch, 1, MOD_WIDTH), F32), _bf16_like(a_w_out),
                   jax.ShapeDtypeStruct((N_MAIN, D_MODEL), BF16), jax.ShapeDtypeStruct((GATE_PAD, D_MODEL), BF16)],
        compiler_params=_compiler_params(),
        name="adaln",
    )(c, ada_w, kv_ada_w, ba, bk, a_w_out, w_in_t, w_in_t)


NQ = M_HEADS * M_QK_DIM
N_MAIN = 2 * NQ + 2 * D_MODEL
GATE_PAD = LANES_V7X
GATE_QUANTS = 3
GATE_PART_ROWS = 32
GATE_EXPAND_COLS = GATE_QUANTS * M_HEADS * LANES_V7X


def _gate_expand_matrix():
    e = np.zeros((GATE_PART_ROWS, GATE_EXPAND_COLS), np.float32)
    for quant in range(GATE_QUANTS):
        for h in range(M_HEADS):
            row = quant * SUBLANES_V7X + M_HEADS + h
            grp = quant * M_HEADS + h
            e[row, grp * LANES_V7X:(grp + 1) * LANES_V7X] = 1.0
    return np.tile(e, (3, 1))


def _segment_scan(x, op, identity):
    pos = lax.broadcasted_iota(jnp.int32, x.shape, 1) % M_CHUNK
    shift = 1
    while shift < M_CHUNK:
        x = op(x, jnp.where(pos >= shift, pltpu.roll(x, shift, axis=1), identity))
        shift *= 2
    return x


def _gate_scan(g_tm):
    L = M_CHUNK
    nc = g_tm.shape[0] // L
    gates_t = jnp.concatenate(
        [g_tm[c * L:(c + 1) * L, :].T[0:2 * M_HEADS, :] for c in range(nc)], axis=1)
    capped = GATE_CAP * jnp.tanh(gates_t / GATE_CAP)
    log_f = jnp.minimum(capped, 0.0) - jnp.log1p(jnp.exp(-jnp.abs(capped)))
    bcum = _segment_scan(log_f * LOG2_E, jnp.add, 0.0)
    a = pltpu.roll(capped * LOG2_E, M_HEADS, axis=0) - bcum
    cmax = _segment_scan(a, jnp.maximum, -jnp.inf)
    stacked = jnp.concatenate([bcum, cmax, a, jnp.zeros_like(a)], axis=0)
    return a, jnp.concatenate(_split3_bf16(stacked), axis=0)


def _gate_expand(parts, gexp_ref):
    L = M_CHUNK
    return [_dot_tn(parts[:, c * L:(c + 1) * L], gexp_ref[...]) for c in range(parts.shape[1] // L)]


def _mlstm_tile(a_rows, cols, qk_s, v_s, h_s, c_s, n_s, m_s):
    L = M_CHUNK
    W = LANES_V7X
    nc = len(cols)
    heads = range(M_HEADS)
    causal = lax.broadcasted_iota(jnp.int32, (L, L), 1) <= lax.broadcasted_iota(jnp.int32, (L, L), 0)
    ones = jnp.ones((L, W), BF16)

    def rows(c):
        return slice(c * L, (c + 1) * L)

    def q_of(h, c):
        return qk_s[rows(c), h * M_QK_DIM:(h + 1) * M_QK_DIM]

    def k_of(h, c):
        return qk_s[rows(c), NQ + h * M_QK_DIM:NQ + (h + 1) * M_QK_DIM]

    def v1_of(h, c):
        return jnp.concatenate([v_s[rows(c), h * M_V_DIM:(h + 1) * M_V_DIM], ones], axis=1)

    def col(c, quant, h):
        g = quant * M_HEADS + h
        return cols[c][:, g * W:(g + 1) * W]

    scores = [[_dot_nt(q_of(h, c), k_of(h, c)) for c in range(nc)] for h in heads]

    m_in = [[None] * nc for _ in heads]
    mx = [[None] * nc for _ in heads]
    for h in heads:
        m_old = m_s[h:h + 1, :]
        for c in range(nc):
            m_in[h][c] = m_old
            mx[h][c] = jnp.maximum(m_old, col(c, 1, h)[L - 1:L, :])
            m_old = col(c, 0, h)[L - 1:L, :] + mx[h][c]
        m_s[h:h + 1, :] = m_old

    deltas = [[_dot_tn((k_of(h, c).astype(F32) * jnp.exp2(col(c, 2, h) - mx[h][c])).astype(BF16), v1_of(h, c))
               for c in range(nc)] for h in heads]

    for h in heads:
        c_t = c_s[h]
        n_bc = n_s[h]
        for c in range(nc):
            m_old = m_in[h][c]
            a_row = a_rows[M_HEADS + h:M_HEADS + h + 1, rows(c)]
            mt = jnp.maximum(col(c, 1, h), m_old)
            w_intra = jnp.exp2(jnp.where(causal, a_row - mt, -jnp.inf))
            w_inter = jnp.exp2(m_old - mt)
            e_neg = jnp.exp2(-col(c, 0, h) - mt)
            s_qk = scores[h][c] * w_intra
            lhs = jnp.concatenate([(q_of(h, c).astype(F32) * w_inter).astype(BF16), s_qk.astype(BF16)], axis=1)
            state = jnp.concatenate([c_t.astype(BF16), n_bc.astype(BF16)], axis=1)
            out = _dot(lhs, jnp.concatenate([state, v1_of(h, c)], axis=0))
            r_den = 1.0 / jnp.maximum(jnp.abs(out[:, M_V_DIM:M_V_DIM + W]), e_neg)
            for half in range(M_V_DIM // W):
                h_s[rows(c), h * M_V_DIM + half * W:h * M_V_DIM + (half + 1) * W] = (
                    out[:, half * W:(half + 1) * W] * r_den)
            decay = jnp.exp2(m_old - mx[h][c])
            c_t = jnp.concatenate([decay] * (M_V_DIM // W), axis=1) * c_t + deltas[h][c][:, 0:M_V_DIM]
            n_bc = decay * n_bc + deltas[h][c][:, M_V_DIM:M_V_DIM + W]
        c_s[h] = c_t
        n_s[h] = n_bc


def _mix0_tile(tiles_per_seq, x_ref, mod_ref, wg_ref, bg_ref, gexp_ref, nw_ref, lng_ref, lnb_ref,
               o_ref, win_s, wout_s, qk_s, v_s, og_s, h_s, c_s, n_s, m_s):
    @pl.when(pl.program_id(0) % tiles_per_seq == 0)
    def _():
        c_s[...] = jnp.zeros_like(c_s)
        n_s[...] = jnp.zeros_like(n_s)
        m_s[...] = jnp.zeros_like(m_s)

    x = x_ref[0]
    u = (x * (1.0 + _mod_slice(mod_ref, 1)) + _mod_slice(mod_ref, 0)).astype(BF16)
    gate_bias = jnp.concatenate([bg_ref[...], jnp.zeros((1, GATE_PAD - 2 * M_HEADS), F32)], axis=1)
    a_rows, parts = _gate_scan(_dot_nt(u, wg_ref[...]) + gate_bias)
    qk_s[:, 0:NQ] = _dot_nt(u, win_s[0:NQ, :]).astype(BF16)
    qk_s[:, NQ:2 * NQ] = (_dot_nt(u, win_s[NQ:2 * NQ, :]) * (1.0 / math.sqrt(M_QK_DIM))).astype(BF16)
    v_s[...] = _dot_nt(u, win_s[2 * NQ:2 * NQ + D_MODEL, :]).astype(BF16)
    og_s[...] = _dot_nt(u, win_s[2 * NQ + D_MODEL:N_MAIN, :])

    _mlstm_tile(a_rows, _gate_expand(parts, gexp_ref), qk_s, v_s, h_s, c_s, n_s, m_s)

    normed = []
    for h in range(M_HEADS):
        hh = h_s[:, h * M_V_DIM:(h + 1) * M_V_DIM]
        normed.append(hh * lax.rsqrt(jnp.mean(hh * hh, axis=-1, keepdims=True) + RMS_EPS))
    hn = jnp.concatenate(normed, axis=1)
    gated = hn * nw_ref[...] * jax.nn.sigmoid(og_s[...])
    y = _dot(gated.astype(BF16), wout_s[...])
    r = DEEPNORM_ALPHA * x + (1.0 + _mod_slice(mod_ref, 2)) * y
    o_ref[0] = _layer_norm(r, lng_ref[...], lnb_ref[...])


def _mix0_kernel(tiles_per_seq, x_ref, mod_ref, win_ref, wout_ref, wg_ref, bg_ref, gexp_ref, nw_ref,
                 lng_ref, lnb_ref, up_c, dn_c, kv_c, q_c,
                 o_ref, up_b, dn_b, kv_b, q_b, *scratch):
    up_b[...] = up_c[0].astype(BF16)
    dn_b[...] = dn_c[0].astype(BF16)
    kv_b[...] = kv_c[0].astype(BF16)
    wq = q_c[0]
    q_b[...] = jnp.concatenate(
        [wq[:, hd * A_HEAD_DIM:(hd + 1) * A_HEAD_DIM] for hd in HEAD_ORDER], axis=1).astype(BF16)
    _mix0_tile(tiles_per_seq, x_ref, mod_ref, wg_ref, bg_ref, gexp_ref, nw_ref, lng_ref, lnb_ref,
               o_ref, win_ref, wout_ref, *scratch)


def _mix0(x, mod, w_in, w_out, w_gate, b_gates, gate_expand, norm_w, ln_g, ln_b,
          mlp_w_up, mlp_w_down, w_kv, b_w_q):
    B, S, D = x.shape
    tm = WIDE_ROW_TILE
    n_t = S // tm
    tiles = (B, n_t)
    casts = [_cast_specs(w, 0, B * n_t) for w in (mlp_w_up, mlp_w_down, w_kv, b_w_q)]
    return pl.pallas_call(
        functools.partial(_mix0_kernel, n_t),
        grid=(B * n_t,),
        in_specs=[
            _row_spec(tm, D, tiles),
            _mod_spec(tiles),
            _const_spec((N_MAIN, D)),
            _const_spec((D, D)),
            _const_spec((GATE_PAD, D)),
            _const_spec((1, 2 * M_HEADS)),
            _const_spec(gate_expand.shape),
            _const_spec((1, D)),
            _const_spec((1, D)),
            _const_spec((1, D)),
            *[c[0] for c in casts],
        ],
        out_specs=[_row_spec(tm, D, tiles), *[c[1] for c in casts]],
        out_shape=[jax.ShapeDtypeStruct((B, S, D), F32),
                   *[_bf16_like(w) for w in (mlp_w_up, mlp_w_down, w_kv, b_w_q)]],
        scratch_shapes=[
            pltpu.VMEM((tm, 2 * NQ), BF16),
            pltpu.VMEM((tm, D_MODEL), BF16),
            pltpu.VMEM((tm, D_MODEL), F32),
            pltpu.VMEM((tm, D_MODEL), F32),
            pltpu.VMEM((M_HEADS, M_QK_DIM, M_V_DIM), F32),
            pltpu.VMEM((M_HEADS, M_QK_DIM, LANES_V7X), F32),
            pltpu.VMEM((SUBLANES_V7X, LANES_V7X), F32),
        ],
        compiler_params=_compiler_params(),
        name="mix0",
    )(x, mod, w_in, w_out, w_gate, b_gates, gate_expand, norm_w, ln_g, ln_b,
      mlp_w_up, mlp_w_down, w_kv, b_w_q)


def _mlp_core(h, mod_ref, layer, wup_s, wdn_s, lng_ref, lnb_ref):
    base = layer * N_MOD
    u = (h * (1.0 + _mod_slice(mod_ref, base + 4)) + _mod_slice(mod_ref, base + 3)).astype(BF16)
    a = jnp.maximum(_dot(u, wup_s[...]), 0.0)
    y = _dot((a * a).astype(BF16), wdn_s[...])
    r = DEEPNORM_ALPHA * h + (1.0 + _mod_slice(mod_ref, base + 5)) * y
    return _layer_norm(r, lng_ref[...], lnb_ref[...])


def _rope_tables(pos_row, invf_ref, expand_ref):
    ang = pos_row.astype(F32) * invf_ref[...]
    trig = jnp.concatenate([jnp.cos(ang), jnp.sin(ang)], axis=0)
    parts = jnp.concatenate(_split3_bf16(trig), axis=0)
    tab = _dot_tn(parts, expand_ref[...])
    lane = lax.broadcasted_iota(jnp.int32, (1, LANES_V7X), 1)
    cos_t = tab[:, 0:LANES_V7X] + jnp.where(lane % A_HEAD_DIM >= ROPE_DIM, 1.0, 0.0)
    return cos_t, tab[:, LANES_V7X:2 * LANES_V7X], tab[:, 2 * LANES_V7X:3 * LANES_V7X]


def _rope(x, tables):
    cos_t, sin_up, sin_dn = tables
    out = []
    for g in range(x.shape[1] // LANES_V7X):
        xg = x[:, g * LANES_V7X:(g + 1) * LANES_V7X]
        from_lo = pltpu.roll(xg, ROPE_HALF, axis=1)
        from_hi = pltpu.roll(xg, LANES_V7X - ROPE_HALF, axis=1)
        out.append(xg * cos_t + from_lo * sin_up + from_hi * sin_dn)
    return jnp.concatenate(out, axis=1)


def _mlp0_kernel(n_tiles, tiles_per_seq, h_ref, mod_ref, modp_ref, posp_ref, wup_ref, wdn_ref, wkv_ref, wq_ref, lng_ref,
                 lnb_ref, invf_ref, expand_ref, up_c, dn_c, wo_c,
                 o_ref, q_ref, k_ref, v_ref, up_b, dn_b, wo_b, r_s):
    i = pl.program_id(0)

    def close_previous_tile():
        h1 = _layer_norm(r_s[...], lng_ref[...], lnb_ref[...])
        o_ref[0] = h1
        kv_shift = modp_ref[0, :, MOD_KV_BASE:MOD_KV_BASE + D_MODEL]
        kv_scale = modp_ref[0, :, MOD_KV_BASE + D_MODEL:MOD_KV_BASE + 2 * D_MODEL]
        kv = _dot((h1 * (1.0 + kv_scale) + kv_shift).astype(BF16), wkv_ref[...])
        uq = (h1 * (1.0 + _mod_slice(modp_ref, N_MOD + 1)) + _mod_slice(modp_ref, N_MOD)).astype(BF16)
        q = _dot(uq, wq_ref[...])
        batch = jnp.clip(i - 1, 0, n_tiles - 1) // tiles_per_seq
        tables = _rope_tables(posp_ref[pl.ds(batch, 1), :], invf_ref, expand_ref)
        q_ref[0] = (_rope(q, tables) * (LOG2_E / math.sqrt(A_HEAD_DIM))).astype(BF16)
        k_ref[0] = _rope(kv[:, 0:A_KV_DIM], tables).astype(BF16)
        v_ref[0] = kv[:, A_KV_DIM:2 * A_KV_DIM].astype(BF16)

    @pl.when(i == 0)
    def _():
        r_s[...] = jnp.zeros_like(r_s)

    @pl.when(i < n_tiles)
    def _():
        up_b[...] = up_c[0].astype(BF16)
        dn_b[...] = dn_c[0].astype(BF16)
        wo_b[...] = wo_c[0].astype(BF16)

        h = h_ref[0]
        u = (h * (1.0 + _mod_slice(mod_ref, 4)) + _mod_slice(mod_ref, 3)).astype(BF16)
        a = jnp.maximum(_dot(u, wup_ref[...]), 0.0)
        close_previous_tile()
        y = _dot((a * a).astype(BF16), wdn_ref[...])
        r_s[...] = DEEPNORM_ALPHA * h + (1.0 + _mod_slice(mod_ref, 5)) * y

    @pl.when(i == n_tiles)
    def _():
        close_previous_tile()


def _mlp1_kernel(h_ref, mod_ref, wup_ref, wdn_ref, lng_ref, lnb_ref, o_ref):
    o_ref[0] = _mlp_core(h_ref[0], mod_ref, 1, wup_ref, wdn_ref, lng_ref, lnb_ref)


def _mlp0(h, mod, pos, w_up, w_dn, w_kv, w_q, ln_g, ln_b, inv_freq, expand, mlp_w_up, mlp_w_down, b_w_o):
    B, S, D = h.shape
    tm = ROW_TILE
    n_t = S // tm
    tiles = (B, n_t)
    n_tiles = B * n_t
    q_dim = A_HEADS * A_HEAD_DIM

    def pos_index(i):
        return 0, _tile_index(i, tiles, lag=1)[1]

    up_in, up_out = _cast_specs(mlp_w_up, 1, n_tiles)
    dn_in, dn_out = _cast_specs(mlp_w_down, 1, n_tiles)
    _, wo_out = _cast_specs(b_w_o, 0, n_tiles)
    wo_rows = q_dim // n_tiles
    subs = A_HEAD_DIM // wo_rows

    def wo_index(i):
        c = jnp.minimum(i, n_tiles - 1)
        slot, sub = c // subs, c % subs
        g, half = slot // 2, slot % 2
        return 0, (8 * (g // 4) + 4 * half + g % 4) * subs + sub, 0

    return pl.pallas_call(
        functools.partial(_mlp0_kernel, n_tiles, n_t),
        grid=(n_tiles + 1,),
        in_specs=[
            _row_spec(tm, D, tiles),
            _mod_spec(tiles),
            _mod_spec(tiles, lag=1),
            pl.BlockSpec((B, tm), pos_index),
            _const_spec((D, D_FF)),
            _const_spec((D_FF, D)),
            _const_spec((D, 2 * A_KV_DIM)),
            _const_spec((D, q_dim)),
            _const_spec((1, D)),
            _const_spec((1, D)),
            _const_spec((ROPE_HALF, 1)),
            _const_spec(expand.shape),
            up_in,
            dn_in,
            pl.BlockSpec((1, wo_rows, D), wo_index),
        ],
        out_specs=[_row_spec(tm, D, tiles, lag=1), _row_spec(tm, q_dim, tiles, lag=1),
                   _row_spec(tm, A_KV_DIM, tiles, lag=1), _row_spec(tm, A_KV_DIM, tiles, lag=1),
                   up_out, dn_out, wo_out],
        out_shape=[
            jax.ShapeDtypeStruct((B, S, D), F32),
            jax.ShapeDtypeStruct((B, S, q_dim), BF16),
            jax.ShapeDtypeStruct((B, S, A_KV_DIM), BF16),
            jax.ShapeDtypeStruct((B, S, A_KV_DIM), BF16),
            _bf16_like(mlp_w_up), _bf16_like(mlp_w_down), _bf16_like(b_w_o),
        ],
        scratch_shapes=[pltpu.VMEM((tm, D), F32)],
        compiler_params=_compiler_params(),
        name="mlp0",
    )(h, mod, mod, pos, w_up, w_dn, w_kv, w_q, ln_g, ln_b, inv_freq, expand, mlp_w_up, mlp_w_down, b_w_o)


def _mlp1(h, mod, w_up, w_dn, ln_g, ln_b):
    B, S, D = h.shape
    tm = WIDE_ROW_TILE
    n_t = S // tm
    tiles = (B, n_t)
    return pl.pallas_call(
        _mlp1_kernel,
        grid=(B * n_t,),
        in_specs=[
            _row_spec(tm, D, tiles),
            _mod_spec(tiles),
            _const_spec((D, D_FF)),
            _const_spec((D_FF, D)),
            _const_spec((1, D)),
            _const_spec((1, D)),
        ],
        out_specs=_row_spec(tm, D, tiles),
        out_shape=jax.ShapeDtypeStruct((B, S, D), F32),
        compiler_params=_compiler_params(),
        name="mlp1",
    )(h, mod, w_up, w_dn, ln_g, ln_b)


HEAD_ORDER = tuple(8 * (G // 4) + 4 * half + (G % 4) for G in range(A_HEADS // 2) for half in range(2))
KV_PAIRS = A_KV_HEADS // 2


def _attn_pair_operands(k_band, v_band, p, lo_half, ones_bd):
    W = LANES_V7X
    zero = jnp.zeros((), BF16)
    kp = k_band[:, p * W:(p + 1) * W]
    vp = v_band[:, p * W:(p + 1) * W]
    k_bd = jnp.concatenate([jnp.where(lo_half, kp, zero), jnp.where(lo_half, zero, kp)], axis=0)
    v_bd = jnp.concatenate([jnp.where(lo_half, vp, zero), jnp.where(lo_half, zero, vp)], axis=0)
    return k_bd, jnp.concatenate([v_bd, ones_bd], axis=1)


def _attn_softmax_pv(s_all, v_ext, p, start_bias, sink_ref, prev_visible, lo_half_q):
    L = WINDOW
    W = LANES_V7X
    zero = jnp.zeros((), BF16)
    e_rows, sink_rows = [], []
    for j in range(A_GROUP):
        halves, sink_terms = [], []
        for half in range(2):
            c0 = half * 2 * L
            s = jnp.where(prev_visible, s_all[j * L:(j + 1) * L, c0:c0 + L], s_all[j * L:(j + 1) * L, c0 + L:c0 + 2 * L])
            if start_bias is not None:
                s = s + start_bias
            sink = sink_ref[0, HEAD_ORDER[2 * (4 * p + j) + half]] * LOG2_E
            mx = jnp.maximum(jnp.max(s, axis=1, keepdims=True), sink)
            e = jnp.exp2(s - mx).astype(BF16)
            halves += [jnp.where(prev_visible, e, zero), jnp.where(prev_visible, zero, e)]
            sink_terms.append(jnp.exp2(sink - mx))
        e_rows.append(jnp.concatenate(halves, axis=1))
        sink_rows.append(jnp.where(lo_half_q, sink_terms[0], sink_terms[1]))
    o_ext = _dot(jnp.concatenate(e_rows, axis=0), v_ext)
    o_all = o_ext[:, 0:W] / (o_ext[:, W:2 * W] + jnp.concatenate(sink_rows, axis=0))
    return [o_all[j * L:(j + 1) * L, :].astype(BF16) for j in range(A_GROUP)]


def _attn_tile(tiles_per_seq, sink_ref, q_ref, k_ref, kprev_ref, v_ref, vprev_ref, h_ref, mod_ref,
               wo_ref, lng_ref, lnb_ref, o_ref, att_s):
    L = WINDOW
    W = LANES_V7X
    tq = q_ref.shape[1]
    prev_visible = lax.broadcasted_iota(jnp.int32, (L, L), 1) > lax.broadcasted_iota(jnp.int32, (L, L), 0)
    seq_start = pl.program_id(0) % tiles_per_seq == 0
    start_bias = jnp.where(prev_visible & seq_start, -jnp.inf, 0.0)
    lo_half = lax.broadcasted_iota(jnp.int32, (2 * L, W), 1) < A_HEAD_DIM
    lo_half_q = lax.broadcasted_iota(jnp.int32, (L, W), 1) < A_HEAD_DIM
    ones_bd = jnp.concatenate([jnp.where(lo_half, 1.0, 0.0), jnp.where(lo_half, 0.0, 1.0)], axis=0).astype(BF16)

    for blk in range(tq // L):
        r0 = blk * L
        if blk == 0:
            k_prev, v_prev, blk_bias = kprev_ref[0], vprev_ref[0], start_bias
        else:
            k_prev, v_prev, blk_bias = k_ref[0, r0 - L:r0, :], v_ref[0, r0 - L:r0, :], None
        k_band = jnp.concatenate([k_prev, k_ref[0, r0:r0 + L, :]], axis=0)
        v_band = jnp.concatenate([v_prev, v_ref[0, r0:r0 + L, :]], axis=0)
        for p in range(KV_PAIRS):
            k_bd, v_ext = _attn_pair_operands(k_band, v_band, p, lo_half, ones_bd)
            lhs = jnp.concatenate(
                [q_ref[0, r0:r0 + L, (4 * p + j) * W:(4 * p + j + 1) * W] for j in range(A_GROUP)], axis=0)
            outs = _attn_softmax_pv(_dot_nt(lhs, k_bd), v_ext, p, blk_bias, sink_ref, prev_visible, lo_half_q)
            for j in range(A_GROUP):
                att_s[r0:r0 + L, (4 * p + j) * W:(4 * p + j + 1) * W] = outs[j]

    y = _dot(att_s[...], wo_ref[...])
    r = DEEPNORM_ALPHA * h_ref[0] + (1.0 + _mod_slice(mod_ref, N_MOD + 2)) * y
    o_ref[0] = _layer_norm(r, lng_ref[...], lnb_ref[...])


def _attn(sinks, q, k, v, h, mod, w_o, ln_g, ln_b):
    B, S, D = h.shape
    tq = WIDE_ROW_TILE
    n_t = S // tq
    tiles = (B, n_t)
    q_dim = A_HEADS * A_HEAD_DIM
    blocks_per_tile = tq // WINDOW

    def prev_index(i):
        b, t = _tile_index(i, tiles)
        return b, jnp.maximum(t * blocks_per_tile - 1, 0), 0

    return pl.pallas_call(
        functools.partial(_attn_tile, n_t),
        grid=(B * n_t,),
        in_specs=[
            pl.BlockSpec(memory_space=pltpu.SMEM),
            _row_spec(tq, q_dim, tiles),
            _row_spec(tq, A_KV_DIM, tiles),
            pl.BlockSpec((1, WINDOW, A_KV_DIM), prev_index),
            _row_spec(tq, A_KV_DIM, tiles),
            pl.BlockSpec((1, WINDOW, A_KV_DIM), prev_index),
            _row_spec(tq, D, tiles),
            _mod_spec(tiles),
            _const_spec((q_dim, D)),
            _const_spec((1, D)),
            _const_spec((1, D)),
        ],
        out_specs=_row_spec(tq, D, tiles),
        out_shape=jax.ShapeDtypeStruct((B, S, D), F32),
        scratch_shapes=[pltpu.VMEM((tq, q_dim), BF16)],
        compiler_params=_compiler_params(),
        name="attn",
    )(sinks, q, k, k, v, v, h, mod, w_o, ln_g, ln_b)


def _rope_expand_matrix():
    e = np.zeros((2 * ROPE_HALF, 3 * LANES_V7X), np.float32)
    for lane in range(LANES_V7X):
        d = lane % A_HEAD_DIM
        if d < ROPE_HALF:
            e[d, lane] = 1.0
            e[ROPE_HALF + d, 2 * LANES_V7X + lane] = -1.0
        elif d < ROPE_DIM:
            e[d - ROPE_HALF, lane] = 1.0
            e[d, LANES_V7X + lane] = 1.0
    return np.tile(e, (3, 1))


def kernel(x, c, positions, ada_w, ada_b, kv_ada_w, kv_ada_b, a_w_in, a_b_gates, a_norm_w, a_w_out,
           w_kv, b_w_q, b_sinks, b_w_o, mlp_w_up, mlp_w_down, ln_g, ln_b):
    B, S, D = x.shape
    assert D == D_MODEL and S % WIDE_ROW_TILE == 0 and WIDE_ROW_TILE % ROW_TILE == 0 and ROW_TILE % M_CHUNK == 0

    mod, w_out, w_in, w_gate = _adaln(c, ada_w, ada_b, kv_ada_w, kv_ada_b, a_w_out, jnp.swapaxes(a_w_in, 1, 2))

    h, w_up0, w_dn0, w_kv_b, w_q = _mix0(
        x, mod, w_in, w_out, w_gate, a_b_gates,
        jnp.asarray(_gate_expand_matrix(), BF16), a_norm_w[0].reshape(1, D),
        ln_g[0].reshape(1, D), ln_b[0].reshape(1, D),
        mlp_w_up, mlp_w_down, w_kv[None], b_w_q)

    inv_freq = (ROPE_THETA ** (-jnp.arange(ROPE_HALF, dtype=F32) / ROPE_HALF)).reshape(ROPE_HALF, 1)
    h, q, k, v, w_up1, w_dn1, w_o = _mlp0(
        h, mod, positions, w_up0, w_dn0, w_kv_b, w_q, ln_g[1].reshape(1, D),
        ln_b[1].reshape(1, D), inv_freq, jnp.asarray(_rope_expand_matrix(), BF16), mlp_w_up, mlp_w_down, b_w_o)

    h = _attn(b_sinks, q, k, v, h, mod, w_o, ln_g[2].reshape(1, D), ln_b[2].reshape(1, D))

    return _mlp1(h, mod, w_up1, w_dn1, ln_g[3].reshape(1, D), ln_b[3].reshape(1, D))
```

```python
import functools
import math

import numpy as np
import jax
import jax.numpy as jnp
from jax import lax
from jax.experimental import pallas as pl
from jax.experimental.pallas import tpu as pltpu

F32 = jnp.float32
BF16 = jnp.bfloat16

D_MODEL = 1024
DEPTH = 2
M_HEADS = 4
M_V_DIM = D_MODEL // M_HEADS
M_QK_DIM = M_V_DIM // 2
M_CHUNK = 128
GATE_CAP = 15.0
A_HEADS = 16
A_KV_HEADS = 4
A_GROUP = A_HEADS // A_KV_HEADS
A_HEAD_DIM = 64
A_KV_DIM = A_KV_HEADS * A_HEAD_DIM
WINDOW = 128
ROPE_DIM = A_HEAD_DIM // 4
ROPE_HALF = ROPE_DIM // 2
ROPE_THETA = 500000.0
D_FF = 4 * D_MODEL
DEEPNORM_ALPHA = (2 * DEPTH) ** 0.25
LN_EPS = 1e-5
RMS_EPS = 1e-6
LOG2_E = math.log2(math.e)
N_MOD = 6
MOD_KV_BASE = DEPTH * N_MOD * D_MODEL
MOD_WIDTH = MOD_KV_BASE + 2 * D_MODEL

LANES_V7X = 128
SUBLANES_V7X = 8
VMEM_LIMIT_BYTES_V7X = 56 * 1024 * 1024

ROW_TILE = 512
WIDE_ROW_TILE = 1024
ADALN_COL_TILE = 2048
ADALN_CAST_CHUNKS = 4

def _compiler_params():
    return pltpu.CompilerParams(dimension_semantics=("arbitrary",), vmem_limit_bytes=VMEM_LIMIT_BYTES_V7X)


def _const_spec(shape):
    return pl.BlockSpec(shape, lambda *_: (0,) * len(shape))


def _tile_index(i, tiles, lag=0):
    n_b, n_t = tiles
    j = jnp.clip(i - lag, 0, n_b * n_t - 1)
    return j // n_t, j % n_t


def _row_spec(tm, width, tiles, lag=0):
    def index(i):
        b, t = _tile_index(i, tiles, lag)
        return b, t, 0
    return pl.BlockSpec((1, tm, width), index)


def _mod_spec(tiles, lag=0):
    return pl.BlockSpec((1, 1, MOD_WIDTH), lambda i: (_tile_index(i, tiles, lag)[0], 0, 0))


def _cast_specs(w, layer, n_chunks):
    rows, cols = w.shape[-2:]
    r = rows // n_chunks

    def chunk(i):
        return jnp.minimum(i, n_chunks - 1)

    out_spec = pl.BlockSpec((r, cols), lambda i: (chunk(i), 0))
    if w.ndim == 2:
        return out_spec, out_spec
    return pl.BlockSpec((1, r, cols), lambda i: (layer, chunk(i), 0)), out_spec


def _bf16_like(w):
    return jax.ShapeDtypeStruct(w.shape[-2:], BF16)


def _layer_norm(r, g, b):
    mu = jnp.mean(r, axis=-1, keepdims=True)
    d = r - mu
    var = jnp.mean(d * d, axis=-1, keepdims=True)
    return d * lax.rsqrt(var + LN_EPS) * g + b


def _mod_slice(mod_ref, idx):
    return mod_ref[0, :, idx * D_MODEL:(idx + 1) * D_MODEL]


def _dot(a, b):
    return jnp.dot(a, b, preferred_element_type=F32)


def _dot_nt(a, b):
    return lax.dot_general(a, b, (((1,), (1,)), ((), ())), preferred_element_type=F32)


def _dot_tn(a, b):
    return lax.dot_general(a, b, (((0,), (0,)), ((), ())), preferred_element_type=F32)


def _split3_bf16(x):
    hi = x.astype(BF16)
    r1 = x - hi.astype(F32)
    mid = r1.astype(BF16)
    lo = (r1 - mid.astype(F32)).astype(BF16)
    return hi, mid, lo


N_ADA_TILES = DEPTH * N_MOD * D_MODEL // ADALN_COL_TILE
N_KV_TILES = 2 * D_MODEL // ADALN_COL_TILE


def _adaln_kernel(c_ref, wa_ref, wk_ref, ba_ref, bk_ref, wout_c, win_c, wg_c, o_ref, wout_b, win_b, wg_b):
    i = pl.program_id(0)
    c = c_ref[...]
    cs = (c * jax.nn.sigmoid(c)).astype(BF16)

    @pl.when(i < ADALN_CAST_CHUNKS)
    def _():
        wout_b[...] = wout_c[0].astype(BF16)
        win_b[...] = win_c[0].astype(BF16)

    @pl.when(i == 0)
    def _():
        pad = jnp.zeros((GATE_PAD - 2 * M_HEADS, D_MODEL), F32)
        wg_b[...] = jnp.concatenate([wg_c[0], pad], axis=0).astype(BF16)

    @pl.when(i < N_ADA_TILES)
    def _():
        layer = i // (N_ADA_TILES // DEPTH)
        o_ref[:, 0, :] = _dot(cs, wa_ref[0].astype(BF16)) + ba_ref[pl.ds(layer, 1), :]

    @pl.when(i >= N_ADA_TILES)
    def _():
        o_ref[:, 0, :] = _dot(cs, wk_ref[...].astype(BF16)) + bk_ref[0]


def _adaln(c, ada_w, ada_b, kv_ada_w, kv_ada_b, a_w_out, w_in_t):
    batch = c.shape[0]
    wout_in, wout_out = _cast_specs(a_w_out, 0, ADALN_CAST_CHUNKS)
    win_rows = N_MAIN // ADALN_CAST_CHUNKS

    def cast_chunk(i):
        return jnp.minimum(i, ADALN_CAST_CHUNKS - 1)
    tiles_per_layer = N_MOD * D_MODEL // ADALN_COL_TILE
    bk = kv_ada_b.reshape(N_KV_TILES, 1, ADALN_COL_TILE)

    def ada_idx(i):
        return jnp.minimum(i, N_ADA_TILES - 1)

    def kv_idx(i):
        return jnp.maximum(i - N_ADA_TILES, 0)

    return pl.pallas_call(
        _adaln_kernel,
        grid=(N_ADA_TILES + N_KV_TILES,),
        in_specs=[
            _const_spec((batch, D_MODEL)),
            pl.BlockSpec((1, D_MODEL, ADALN_COL_TILE),
                         lambda i: (ada_idx(i) // tiles_per_layer, 0, ada_idx(i) % tiles_per_layer)),
            pl.BlockSpec((D_MODEL, ADALN_COL_TILE), lambda i: (0, kv_idx(i))),
            pl.BlockSpec((DEPTH, ADALN_COL_TILE), lambda i: (0, ada_idx(i) % tiles_per_layer)),
            pl.BlockSpec((1, 1, ADALN_COL_TILE), lambda i: (kv_idx(i), 0, 0)),
            wout_in,
            pl.BlockSpec((1, win_rows, D_MODEL), lambda i: (0, cast_chunk(i), 0)),
            pl.BlockSpec((1, 2 * M_HEADS, D_MODEL), lambda i: (0, N_MAIN // (2 * M_HEADS), 0)),
        ],
        out_specs=[pl.BlockSpec((batch, 1, ADALN_COL_TILE), lambda i: (0, 0, i)), wout_out,
                   pl.BlockSpec((win_rows, D_MODEL), lambda i: (cast_chunk(i), 0)),
                   _const_spec((GATE_PAD, D_MODEL))],
        out_shape=[jax.ShapeDtypeStruct((batch, 1, MOD_WIDTH), F32), _bf16_like(a_w_out),
                   jax.ShapeDtypeStruct((N_MAIN, D_MODEL), BF16), jax.ShapeDtypeStruct((GATE_PAD, D_MODEL), BF16)],
        compiler_params=_compiler_params(),
        name="adaln",
    )(c, ada_w, kv_ada_w, ada_b, bk, a_w_out, w_in_t, w_in_t)


NQ = M_HEADS * M_QK_DIM
N_MAIN = 2 * NQ + 2 * D_MODEL
GATE_PAD = LANES_V7X
GATE_QUANTS = 3
GATE_PART_ROWS = 32
GATE_EXPAND_COLS = GATE_QUANTS * M_HEADS * LANES_V7X


def _gate_expand_matrix():
    e = np.zeros((GATE_PART_ROWS, GATE_EXPAND_COLS), np.float32)
    for quant in range(GATE_QUANTS):
        for h in range(M_HEADS):
            row = quant * SUBLANES_V7X + M_HEADS + h
            grp = quant * M_HEADS + h
            e[row, grp * LANES_V7X:(grp + 1) * LANES_V7X] = 1.0
    return np.tile(e, (3, 1))


def _segment_scan(x, op, identity):
    pos = lax.broadcasted_iota(jnp.int32, x.shape, 1) % M_CHUNK
    shift = 1
    while shift < M_CHUNK:
        x = op(x, jnp.where(pos >= shift, pltpu.roll(x, shift, axis=1), identity))
        shift *= 2
    return x


def _gate_scan(g_tm):
    L = M_CHUNK
    nc = g_tm.shape[0] // L
    gates_t = jnp.concatenate(
        [g_tm[c * L:(c + 1) * L, :].T[0:2 * M_HEADS, :] for c in range(nc)], axis=1)
    capped = GATE_CAP * jnp.tanh(gates_t / GATE_CAP)
    log_f = jnp.minimum(capped, 0.0) - jnp.log1p(jnp.exp(-jnp.abs(capped)))
    bcum = _segment_scan(log_f * LOG2_E, jnp.add, 0.0)
    a = pltpu.roll(capped * LOG2_E, M_HEADS, axis=0) - bcum
    cmax = _segment_scan(a, jnp.maximum, -jnp.inf)
    stacked = jnp.concatenate([bcum, cmax, a, jnp.zeros_like(a)], axis=0)
    return a, jnp.concatenate(_split3_bf16(stacked), axis=0)


def _gate_expand(parts, gexp_ref):
    L = M_CHUNK
    return [_dot_tn(parts[:, c * L:(c + 1) * L], gexp_ref[...]) for c in range(parts.shape[1] // L)]


def _mlstm_tile(a_rows, cols, qk_s, v_s, h_s, c_s, n_s, m_s):
    L = M_CHUNK
    W = LANES_V7X
    nc = len(cols)
    heads = range(M_HEADS)
    causal = lax.broadcasted_iota(jnp.int32, (L, L), 1) <= lax.broadcasted_iota(jnp.int32, (L, L), 0)
    ones = jnp.ones((L, W), BF16)

    def rows(c):
        return slice(c * L, (c + 1) * L)

    def q_of(h, c):
        return qk_s[rows(c), h * M_QK_DIM:(h + 1) * M_QK_DIM]

    def k_of(h, c):
        return qk_s[rows(c), NQ + h * M_QK_DIM:NQ + (h + 1) * M_QK_DIM]

    def v1_of(h, c):
        return jnp.concatenate([v_s[rows(c), h * M_V_DIM:(h + 1) * M_V_DIM], ones], axis=1)

    def col(c, quant, h):
        g = quant * M_HEADS + h
        return cols[c][:, g * W:(g + 1) * W]

    scores = [[_dot_nt(q_of(h, c), k_of(h, c)) for c in range(nc)] for h in heads]

    m_in = [[None] * nc for _ in heads]
    mx = [[None] * nc for _ in heads]
    for h in heads:
        m_old = m_s[h:h + 1, :]
        for c in range(nc):
            m_in[h][c] = m_old
            mx[h][c] = jnp.maximum(m_old, col(c, 1, h)[L - 1:L, :])
            m_old = col(c, 0, h)[L - 1:L, :] + mx[h][c]
        m_s[h:h + 1, :] = m_old

    deltas = [[_dot_tn((k_of(h, c).astype(F32) * jnp.exp2(col(c, 2, h) - mx[h][c])).astype(BF16), v1_of(h, c))
               for c in range(nc)] for h in heads]

    for h in heads:
        c_t = c_s[h]
        n_bc = n_s[h]
        for c in range(nc):
            m_old = m_in[h][c]
            a_row = a_rows[M_HEADS + h:M_HEADS + h + 1, rows(c)]
            mt = jnp.maximum(col(c, 1, h), m_old)
            w_intra = jnp.exp2(jnp.where(causal, a_row - mt, -jnp.inf))
            w_inter = jnp.exp2(m_old - mt)
            e_neg = jnp.exp2(-col(c, 0, h) - mt)
            s_qk = scores[h][c] * w_intra
            lhs = jnp.concatenate([(q_of(h, c).astype(F32) * w_inter).astype(BF16), s_qk.astype(BF16)], axis=1)
            state = jnp.concatenate([c_t.astype(BF16), n_bc.astype(BF16)], axis=1)
            out = _dot(lhs, jnp.concatenate([state, v1_of(h, c)], axis=0))
            r_den = 1.0 / jnp.maximum(jnp.abs(out[:, M_V_DIM:M_V_DIM + W]), e_neg)
            for half in range(M_V_DIM // W):
                h_s[rows(c), h * M_V_DIM + half * W:h * M_V_DIM + (half + 1) * W] = (
                    out[:, half * W:(half + 1) * W] * r_den)
            decay = jnp.exp2(m_old - mx[h][c])
            c_t = jnp.concatenate([decay] * (M_V_DIM // W), axis=1) * c_t + deltas[h][c][:, 0:M_V_DIM]
            n_bc = decay * n_bc + deltas[h][c][:, M_V_DIM:M_V_DIM + W]
        c_s[h] = c_t
        n_s[h] = n_bc


def _mix0_tile(tiles_per_seq, x_ref, mod_ref, wg_ref, bg_ref, gexp_ref, nw_ref, lng_ref, lnb_ref,
               o_ref, win_s, wout_s, qk_s, v_s, og_s, h_s, c_s, n_s, m_s):
    @pl.when(pl.program_id(0) % tiles_per_seq == 0)
    def _():
        c_s[...] = jnp.zeros_like(c_s)
        n_s[...] = jnp.zeros_like(n_s)
        m_s[...] = jnp.zeros_like(m_s)

    x = x_ref[0]
    u = (x * (1.0 + _mod_slice(mod_ref, 1)) + _mod_slice(mod_ref, 0)).astype(BF16)
    gate_bias = jnp.concatenate([bg_ref[...], jnp.zeros((1, GATE_PAD - 2 * M_HEADS), F32)], axis=1)
    a_rows, parts = _gate_scan(_dot_nt(u, wg_ref[...]) + gate_bias)
    qk_s[:, 0:NQ] = _dot_nt(u, win_s[0:NQ, :]).astype(BF16)
    qk_s[:, NQ:2 * NQ] = (_dot_nt(u, win_s[NQ:2 * NQ, :]) * (1.0 / math.sqrt(M_QK_DIM))).astype(BF16)
    v_s[...] = _dot_nt(u, win_s[2 * NQ:2 * NQ + D_MODEL, :]).astype(BF16)
    og_s[...] = _dot_nt(u, win_s[2 * NQ + D_MODEL:N_MAIN, :])

    _mlstm_tile(a_rows, _gate_expand(parts, gexp_ref), qk_s, v_s, h_s, c_s, n_s, m_s)

    normed = []
    for h in range(M_HEADS):
        hh = h_s[:, h * M_V_DIM:(h + 1) * M_V_DIM]
        normed.append(hh * lax.rsqrt(jnp.mean(hh * hh, axis=-1, keepdims=True) + RMS_EPS))
    hn = jnp.concatenate(normed, axis=1)
    gated = hn * nw_ref[...] * jax.nn.sigmoid(og_s[...])
    y = _dot(gated.astype(BF16), wout_s[...])
    r = DEEPNORM_ALPHA * x + (1.0 + _mod_slice(mod_ref, 2)) * y
    o_ref[0] = _layer_norm(r, lng_ref[...], lnb_ref[...])


def _mix0_kernel(tiles_per_seq, x_ref, mod_ref, win_ref, wout_ref, wg_ref, bg_ref, gexp_ref, nw_ref,
                 lng_ref, lnb_ref, up_c, dn_c, kv_c, q_c,
                 o_ref, up_b, dn_b, kv_b, q_b, *scratch):
    up_b[...] = up_c[0].astype(BF16)
    dn_b[...] = dn_c[0].astype(BF16)
    kv_b[...] = kv_c[...].astype(BF16)
    wq = q_c[0]
    q_b[...] = jnp.concatenate(
        [wq[:, hd * A_HEAD_DIM:(hd + 1) * A_HEAD_DIM] for hd in HEAD_ORDER], axis=1).astype(BF16)
    _mix0_tile(tiles_per_seq, x_ref, mod_ref, wg_ref, bg_ref, gexp_ref, nw_ref, lng_ref, lnb_ref,
               o_ref, win_ref, wout_ref, *scratch)


def _mix0(x, mod, w_in, w_out, w_gate, b_gates, gate_expand, norm_w, ln_g, ln_b,
          mlp_w_up, mlp_w_down, w_kv, b_w_q):
    B, S, D = x.shape
    tm = ROW_TILE
    n_t = S // tm
    tiles = (B, n_t)
    casts = [_cast_specs(w, 0, B * n_t) for w in (mlp_w_up, mlp_w_down, w_kv, b_w_q)]
    return pl.pallas_call(
        functools.partial(_mix0_kernel, n_t),
        grid=(B * n_t,),
        in_specs=[
            _row_spec(tm, D, tiles),
            _mod_spec(tiles),
            _const_spec((N_MAIN, D)),
            _const_spec((D, D)),
            _const_spec((GATE_PAD, D)),
            _const_spec((1, 2 * M_HEADS)),
            _const_spec(gate_expand.shape),
            _const_spec((1, D)),
            _const_spec((1, D)),
            _const_spec((1, D)),
            *[c[0] for c in casts],
        ],
        out_specs=[_row_spec(tm, D, tiles), *[c[1] for c in casts]],
        out_shape=[jax.ShapeDtypeStruct((B, S, D), F32),
                   *[_bf16_like(w) for w in (mlp_w_up, mlp_w_down, w_kv, b_w_q)]],
        scratch_shapes=[
            pltpu.VMEM((tm, 2 * NQ), BF16),
            pltpu.VMEM((tm, D_MODEL), BF16),
            pltpu.VMEM((tm, D_MODEL), F32),
            pltpu.VMEM((tm, D_MODEL), F32),
            pltpu.VMEM((M_HEADS, M_QK_DIM, M_V_DIM), F32),
            pltpu.VMEM((M_HEADS, M_QK_DIM, LANES_V7X), F32),
            pltpu.VMEM((SUBLANES_V7X, LANES_V7X), F32),
        ],
        compiler_params=_compiler_params(),
        name="mix0",
    )(x, mod, w_in, w_out, w_gate, b_gates, gate_expand, norm_w, ln_g, ln_b,
      mlp_w_up, mlp_w_down, w_kv, b_w_q)


def _mlp_core(h, mod_ref, layer, wup_s, wdn_s, lng_ref, lnb_ref):
    base = layer * N_MOD
    u = (h * (1.0 + _mod_slice(mod_ref, base + 4)) + _mod_slice(mod_ref, base + 3)).astype(BF16)
    a = jnp.maximum(_dot(u, wup_s[...]), 0.0)
    y = _dot((a * a).astype(BF16), wdn_s[...])
    r = DEEPNORM_ALPHA * h + (1.0 + _mod_slice(mod_ref, base + 5)) * y
    return _layer_norm(r, lng_ref[...], lnb_ref[...])


def _rope_tables(pos_row, invf_ref, expand_ref):
    ang = pos_row.astype(F32) * invf_ref[...]
    trig = jnp.concatenate([jnp.cos(ang), jnp.sin(ang)], axis=0)
    parts = jnp.concatenate(_split3_bf16(trig), axis=0)
    tab = _dot_tn(parts, expand_ref[...])
    lane = lax.broadcasted_iota(jnp.int32, (1, LANES_V7X), 1)
    cos_t = tab[:, 0:LANES_V7X] + jnp.where(lane % A_HEAD_DIM >= ROPE_DIM, 1.0, 0.0)
    return cos_t, tab[:, LANES_V7X:2 * LANES_V7X], tab[:, 2 * LANES_V7X:3 * LANES_V7X]


def _rope(x, tables):
    cos_t, sin_up, sin_dn = tables
    out = []
    for g in range(x.shape[1] // LANES_V7X):
        xg = x[:, g * LANES_V7X:(g + 1) * LANES_V7X]
        from_lo = pltpu.roll(xg, ROPE_HALF, axis=1)
        from_hi = pltpu.roll(xg, LANES_V7X - ROPE_HALF, axis=1)
        out.append(xg * cos_t + from_lo * sin_up + from_hi * sin_dn)
    return jnp.concatenate(out, axis=1)


def _mlp0_kernel(n_tiles, tiles_per_seq, h_ref, mod_ref, modp_ref, posp_ref, wup_ref, wdn_ref, wkv_ref, wq_ref, lng_ref,
                 lnb_ref, invf_ref, expand_ref, up_c, dn_c, wo_c,
                 o_ref, q_ref, k_ref, v_ref, up_b, dn_b, wo_b, r_s):
    i = pl.program_id(0)

    def close_previous_tile():
        h1 = _layer_norm(r_s[...], lng_ref[...], lnb_ref[...])
        o_ref[0] = h1
        kv_shift = modp_ref[0, :, MOD_KV_BASE:MOD_KV_BASE + D_MODEL]
        kv_scale = modp_ref[0, :, MOD_KV_BASE + D_MODEL:MOD_KV_BASE + 2 * D_MODEL]
        kv = _dot((h1 * (1.0 + kv_scale) + kv_shift).astype(BF16), wkv_ref[...])
        uq = (h1 * (1.0 + _mod_slice(modp_ref, N_MOD + 1)) + _mod_slice(modp_ref, N_MOD)).astype(BF16)
        q = _dot(uq, wq_ref[...])
        batch = jnp.clip(i - 1, 0, n_tiles - 1) // tiles_per_seq
        tables = _rope_tables(posp_ref[pl.ds(batch, 1), :], invf_ref, expand_ref)
        q_ref[0] = (_rope(q, tables) * (LOG2_E / math.sqrt(A_HEAD_DIM))).astype(BF16)
        k_ref[0] = _rope(kv[:, 0:A_KV_DIM], tables).astype(BF16)
        v_ref[0] = kv[:, A_KV_DIM:2 * A_KV_DIM].astype(BF16)

    @pl.when(i == 0)
    def _():
        r_s[...] = jnp.zeros_like(r_s)

    @pl.when(i < n_tiles)
    def _():
        up_b[...] = up_c[0].astype(BF16)
        dn_b[...] = dn_c[0].astype(BF16)
        wo_b[...] = wo_c[0].astype(BF16)

        h = h_ref[0]
        u = (h * (1.0 + _mod_slice(mod_ref, 4)) + _mod_slice(mod_ref, 3)).astype(BF16)
        a = jnp.maximum(_dot(u, wup_ref[...]), 0.0)
        close_previous_tile()
        y = _dot((a * a).astype(BF16), wdn_ref[...])
        r_s[...] = DEEPNORM_ALPHA * h + (1.0 + _mod_slice(mod_ref, 5)) * y

    @pl.when(i == n_tiles)
    def _():
        close_previous_tile()


def _mlp1_kernel(h_ref, mod_ref, wup_ref, wdn_ref, lng_ref, lnb_ref, o_ref):
    o_ref[0] = _mlp_core(h_ref[0], mod_ref, 1, wup_ref, wdn_ref, lng_ref, lnb_ref)


def _mlp0(h, mod, pos, w_up, w_dn, w_kv, w_q, ln_g, ln_b, inv_freq, expand, mlp_w_up, mlp_w_down, b_w_o):
    B, S, D = h.shape
    tm = ROW_TILE
    n_t = S // tm
    tiles = (B, n_t)
    n_tiles = B * n_t
    q_dim = A_HEADS * A_HEAD_DIM

    def pos_index(i):
        return 0, _tile_index(i, tiles, lag=1)[1]

    up_in, up_out = _cast_specs(mlp_w_up, 1, n_tiles)
    dn_in, dn_out = _cast_specs(mlp_w_down, 1, n_tiles)
    _, wo_out = _cast_specs(b_w_o, 0, n_tiles)
    wo_rows = q_dim // n_tiles
    subs = A_HEAD_DIM // wo_rows

    def wo_index(i):
        c = jnp.minimum(i, n_tiles - 1)
        slot, sub = c // subs, c % subs
        g, half = slot // 2, slot % 2
        return 0, (8 * (g // 4) + 4 * half + g % 4) * subs + sub, 0

    return pl.pallas_call(
        functools.partial(_mlp0_kernel, n_tiles, n_t),
        grid=(n_tiles + 1,),
        in_specs=[
            _row_spec(tm, D, tiles),
            _mod_spec(tiles),
            _mod_spec(tiles, lag=1),
            pl.BlockSpec((B, tm), pos_index),
            _const_spec((D, D_FF)),
            _const_spec((D_FF, D)),
            _const_spec((D, 2 * A_KV_DIM)),
            _const_spec((D, q_dim)),
            _const_spec((1, D)),
            _const_spec((1, D)),
            _const_spec((ROPE_HALF, 1)),
            _const_spec(expand.shape),
            up_in,
            dn_in,
            pl.BlockSpec((1, wo_rows, D), wo_index),
        ],
        out_specs=[_row_spec(tm, D, tiles, lag=1), _row_spec(tm, q_dim, tiles, lag=1),
                   _row_spec(tm, A_KV_DIM, tiles, lag=1), _row_spec(tm, A_KV_DIM, tiles, lag=1),
                   up_out, dn_out, wo_out],
        out_shape=[
            jax.ShapeDtypeStruct((B, S, D), F32),
            jax.ShapeDtypeStruct((B, S, q_dim), BF16),
            jax.ShapeDtypeStruct((B, S, A_KV_DIM), BF16),
            jax.ShapeDtypeStruct((B, S, A_KV_DIM), BF16),
            _bf16_like(mlp_w_up), _bf16_like(mlp_w_down), _bf16_like(b_w_o),
        ],
        scratch_shapes=[pltpu.VMEM((tm, D), F32)],
        compiler_params=_compiler_params(),
        name="mlp0",
    )(h, mod, mod, pos, w_up, w_dn, w_kv, w_q, ln_g, ln_b, inv_freq, expand, mlp_w_up, mlp_w_down, b_w_o)


def _mlp1(h, mod, w_up, w_dn, ln_g, ln_b):
    B, S, D = h.shape
    tm = WIDE_ROW_TILE
    n_t = S // tm
    tiles = (B, n_t)
    return pl.pallas_call(
        _mlp1_kernel,
        grid=(B * n_t,),
        in_specs=[
            _row_spec(tm, D, tiles),
            _mod_spec(tiles),
            _const_spec((D, D_FF)),
            _const_spec((D_FF, D)),
            _const_spec((1, D)),
            _const_spec((1, D)),
        ],
        out_specs=_row_spec(tm, D, tiles),
        out_shape=jax.ShapeDtypeStruct((B, S, D), F32),
        compiler_params=_compiler_params(),
        name="mlp1",
    )(h, mod, w_up, w_dn, ln_g, ln_b)


HEAD_ORDER = tuple(8 * (G // 4) + 4 * half + (G % 4) for G in range(A_HEADS // 2) for half in range(2))
KV_PAIRS = A_KV_HEADS // 2


def _attn_pair_operands(k_band, v_band, p, lo_half, ones_bd):
    W = LANES_V7X
    zero = jnp.zeros((), BF16)
    kp = k_band[:, p * W:(p + 1) * W]
    vp = v_band[:, p * W:(p + 1) * W]
    k_bd = jnp.concatenate([jnp.where(lo_half, kp, zero), jnp.where(lo_half, zero, kp)], axis=0)
    v_bd = jnp.concatenate([jnp.where(lo_half, vp, zero), jnp.where(lo_half, zero, vp)], axis=0)
    return k_bd, jnp.concatenate([v_bd, ones_bd], axis=1)


def _attn_softmax_pv(s_all, v_ext, p, start_bias, sink_ref, prev_visible, lo_half_q):
    L = WINDOW
    W = LANES_V7X
    zero = jnp.zeros((), BF16)
    e_rows, sink_rows = [], []
    for j in range(A_GROUP):
        halves, sink_terms = [], []
        for half in range(2):
            c0 = half * 2 * L
            s = jnp.where(prev_visible, s_all[j * L:(j + 1) * L, c0:c0 + L], s_all[j * L:(j + 1) * L, c0 + L:c0 + 2 * L])
            if start_bias is not None:
                s = s + start_bias
            sink = sink_ref[0, HEAD_ORDER[2 * (4 * p + j) + half]] * LOG2_E
            mx = jnp.maximum(jnp.max(s, axis=1, keepdims=True), sink)
            e = jnp.exp2(s - mx).astype(BF16)
            halves += [jnp.where(prev_visible, e, zero), jnp.where(prev_visible, zero, e)]
            sink_terms.append(jnp.exp2(sink - mx))
        e_rows.append(jnp.concatenate(halves, axis=1))
        sink_rows.append(jnp.where(lo_half_q, sink_terms[0], sink_terms[1]))
    o_ext = _dot(jnp.concatenate(e_rows, axis=0), v_ext)
    o_all = o_ext[:, 0:W] / (o_ext[:, W:2 * W] + jnp.concatenate(sink_rows, axis=0))
    return [o_all[j * L:(j + 1) * L, :].astype(BF16) for j in range(A_GROUP)]


def _attn_tile(tiles_per_seq, sink_ref, q_ref, k_ref, kprev_ref, v_ref, vprev_ref, h_ref, mod_ref,
               wo_ref, lng_ref, lnb_ref, o_ref, att_s):
    L = WINDOW
    W = LANES_V7X
    tq = q_ref.shape[1]
    prev_visible = lax.broadcasted_iota(jnp.int32, (L, L), 1) > lax.broadcasted_iota(jnp.int32, (L, L), 0)
    seq_start = pl.program_id(0) % tiles_per_seq == 0
    start_bias = jnp.where(prev_visible & seq_start, -jnp.inf, 0.0)
    lo_half = lax.broadcasted_iota(jnp.int32, (2 * L, W), 1) < A_HEAD_DIM
    lo_half_q = lax.broadcasted_iota(jnp.int32, (L, W), 1) < A_HEAD_DIM
    ones_bd = jnp.concatenate([jnp.where(lo_half, 1.0, 0.0), jnp.where(lo_half, 0.0, 1.0)], axis=0).astype(BF16)

    for blk in range(tq // L):
        r0 = blk * L
        if blk == 0:
            k_prev, v_prev, blk_bias = kprev_ref[0], vprev_ref[0], start_bias
        else:
            k_prev, v_prev, blk_bias = k_ref[0, r0 - L:r0, :], v_ref[0, r0 - L:r0, :], None
        k_band = jnp.concatenate([k_prev, k_ref[0, r0:r0 + L, :]], axis=0)
        v_band = jnp.concatenate([v_prev, v_ref[0, r0:r0 + L, :]], axis=0)
        for p in range(KV_PAIRS):
            k_bd, v_ext = _attn_pair_operands(k_band, v_band, p, lo_half, ones_bd)
            lhs = jnp.concatenate(
                [q_ref[0, r0:r0 + L, (4 * p + j) * W:(4 * p + j + 1) * W] for j in range(A_GROUP)], axis=0)
            outs = _attn_softmax_pv(_dot_nt(lhs, k_bd), v_ext, p, blk_bias, sink_ref, prev_visible, lo_half_q)
            for j in range(A_GROUP):
                att_s[r0:r0 + L, (4 * p + j) * W:(4 * p + j + 1) * W] = outs[j]

    y = _dot(att_s[...], wo_ref[...])
    r = DEEPNORM_ALPHA * h_ref[0] + (1.0 + _mod_slice(mod_ref, N_MOD + 2)) * y
    o_ref[0] = _layer_norm(r, lng_ref[...], lnb_ref[...])


def _attn(sinks, q, k, v, h, mod, w_o, ln_g, ln_b):
    B, S, D = h.shape
    tq = WIDE_ROW_TILE
    n_t = S // tq
    tiles = (B, n_t)
    q_dim = A_HEADS * A_HEAD_DIM
    blocks_per_tile = tq // WINDOW

    def prev_index(i):
        b, t = _tile_index(i, tiles)
        return b, jnp.maximum(t * blocks_per_tile - 1, 0), 0

    return pl.pallas_call(
        functools.partial(_attn_tile, n_t),
        grid=(B * n_t,),
        in_specs=[
            pl.BlockSpec(memory_space=pltpu.SMEM),
            _row_spec(tq, q_dim, tiles),
            _row_spec(tq, A_KV_DIM, tiles),
            pl.BlockSpec((1, WINDOW, A_KV_DIM), prev_index),
            _row_spec(tq, A_KV_DIM, tiles),
            pl.BlockSpec((1, WINDOW, A_KV_DIM), prev_index),
            _row_spec(tq, D, tiles),
            _mod_spec(tiles),
            _const_spec((q_dim, D)),
            _const_spec((1, D)),
            _const_spec((1, D)),
        ],
        out_specs=_row_spec(tq, D, tiles),
        out_shape=jax.ShapeDtypeStruct((B, S, D), F32),
        scratch_shapes=[pltpu.VMEM((tq, q_dim), BF16)],
        compiler_params=_compiler_params(),
        name="attn",
    )(sinks, q, k, k, v, v, h, mod, w_o, ln_g, ln_b)


def _rope_expand_matrix():
    e = np.zeros((2 * ROPE_HALF, 3 * LANES_V7X), np.float32)
    for lane in range(LANES_V7X):
        d = lane % A_HEAD_DIM
        if d < ROPE_HALF:
            e[d, lane] = 1.0
            e[ROPE_HALF + d, 2 * LANES_V7X + lane] = -1.0
        elif d < ROPE_DIM:
            e[d - ROPE_HALF, lane] = 1.0
            e[d, LANES_V7X + lane] = 1.0
    return np.tile(e, (3, 1))


def kernel(x, c, positions, ada_w, ada_b, kv_ada_w, kv_ada_b, a_w_in, a_b_gates, a_norm_w, a_w_out,
           w_kv, b_w_q, b_sinks, b_w_o, mlp_w_up, mlp_w_down, ln_g, ln_b):
    B, S, D = x.shape
    assert D == D_MODEL and S % WIDE_ROW_TILE == 0 and WIDE_ROW_TILE % ROW_TILE == 0 and ROW_TILE % M_CHUNK == 0

    mod, w_out, w_in, w_gate = _adaln(c, ada_w, ada_b, kv_ada_w, kv_ada_b, a_w_out, jnp.swapaxes(a_w_in, 1, 2))

    h, w_up0, w_dn0, w_kv_b, w_q = _mix0(
        x, mod, w_in, w_out, w_gate, a_b_gates,
        jnp.asarray(_gate_expand_matrix(), BF16), a_norm_w[0].reshape(1, D),
        ln_g[0].reshape(1, D), ln_b[0].reshape(1, D),
        mlp_w_up, mlp_w_down, w_kv, b_w_q)

    inv_freq = (ROPE_THETA ** (-jnp.arange(ROPE_HALF, dtype=F32) / ROPE_HALF)).reshape(ROPE_HALF, 1)
    h, q, k, v, w_up1, w_dn1, w_o = _mlp0(
        h, mod, positions, w_up0, w_dn0, w_kv_b, w_q, ln_g[1].reshape(1, D),
        ln_b[1].reshape(1, D), inv_freq, jnp.asarray(_rope_expand_matrix(), BF16), mlp_w_up, mlp_w_down, b_w_o)

    h = _attn(b_sinks, q, k, v, h, mod, w_o, ln_g[2].reshape(1, D), ln_b[2].reshape(1, D))

    return _mlp1(h, mod, w_up1, w_dn1, ln_g[3].reshape(1, D), ln_b[3].reshape(1, D))
```

```python
import functools
import math

import numpy as np
import jax
import jax.numpy as jnp
from jax import lax
from jax.experimental import pallas as pl
from jax.experimental.pallas import tpu as pltpu

F32 = jnp.float32
BF16 = jnp.bfloat16

D_MODEL = 1024
DEPTH = 2
M_HEADS = 4
M_V_DIM = D_MODEL // M_HEADS
M_QK_DIM = M_V_DIM // 2
M_CHUNK = 128
GATE_CAP = 15.0
A_HEADS = 16
A_KV_HEADS = 4
A_GROUP = A_HEADS // A_KV_HEADS
A_HEAD_DIM = 64
A_KV_DIM = A_KV_HEADS * A_HEAD_DIM
WINDOW = 128
ROPE_DIM = A_HEAD_DIM // 4
ROPE_HALF = ROPE_DIM // 2
ROPE_THETA = 500000.0
D_FF = 4 * D_MODEL
DEEPNORM_ALPHA = (2 * DEPTH) ** 0.25
LN_EPS = 1e-5
RMS_EPS = 1e-6
LOG2_E = math.log2(math.e)
N_MOD = 6
MOD_KV_BASE = DEPTH * N_MOD * D_MODEL
MOD_WIDTH = MOD_KV_BASE + 2 * D_MODEL

LANES_V7X = 128
SUBLANES_V7X = 8
VMEM_LIMIT_BYTES_V7X = 56 * 1024 * 1024

ROW_TILE = 512
WIDE_ROW_TILE = 1024
ADALN_COL_TILE = 2048
ADALN_CAST_CHUNKS = 4

def _compiler_params():
    return pltpu.CompilerParams(dimension_semantics=("arbitrary",), vmem_limit_bytes=VMEM_LIMIT_BYTES_V7X)


def _const_spec(shape):
    return pl.BlockSpec(shape, lambda *_: (0,) * len(shape))


def _tile_index(i, tiles, lag=0):
    n_b, n_t = tiles
    j = jnp.clip(i - lag, 0, n_b * n_t - 1)
    return j // n_t, j % n_t


def _row_spec(tm, width, tiles, lag=0):
    def index(i):
        b, t = _tile_index(i, tiles, lag)
        return b, t, 0
    return pl.BlockSpec((1, tm, width), index)


def _mod_spec(tiles, lag=0):
    return pl.BlockSpec((1, 1, MOD_WIDTH), lambda i: (_tile_index(i, tiles, lag)[0], 0, 0))


def _cast_specs(w, layer, n_chunks):
    rows, cols = w.shape[-2:]
    r = rows // n_chunks

    def chunk(i):
        return jnp.minimum(i, n_chunks - 1)

    out_spec = pl.BlockSpec((r, cols), lambda i: (chunk(i), 0))
    if w.ndim == 2:
        return out_spec, out_spec
    return pl.BlockSpec((1, r, cols), lambda i: (layer, chunk(i), 0)), out_spec


def _bf16_like(w):
    return jax.ShapeDtypeStruct(w.shape[-2:], BF16)


def _layer_norm(r, g, b):
    mu = jnp.mean(r, axis=-1, keepdims=True)
    d = r - mu
    var = jnp.mean(d * d, axis=-1, keepdims=True)
    return d * lax.rsqrt(var + LN_EPS) * g + b


def _mod_slice(mod_ref, idx):
    return mod_ref[0, :, idx * D_MODEL:(idx + 1) * D_MODEL]


def _dot(a, b):
    return jnp.dot(a, b, preferred_element_type=F32)


def _dot_nt(a, b):
    return lax.dot_general(a, b, (((1,), (1,)), ((), ())), preferred_element_type=F32)


def _dot_tn(a, b):
    return lax.dot_general(a, b, (((0,), (0,)), ((), ())), preferred_element_type=F32)


def _split3_bf16(x):
    hi = x.astype(BF16)
    r1 = x - hi.astype(F32)
    mid = r1.astype(BF16)
    lo = (r1 - mid.astype(F32)).astype(BF16)
    return hi, mid, lo


N_ADA_TILES = DEPTH * N_MOD * D_MODEL // ADALN_COL_TILE
N_KV_TILES = 2 * D_MODEL // ADALN_COL_TILE


def _adaln_kernel(c_ref, wa_ref, wk_ref, ba_ref, bk_ref, wout_c, win_c, wg_c, o_ref, wout_b, win_b, wg_b):
    i = pl.program_id(0)
    c = c_ref[...]
    cs = (c * jax.nn.sigmoid(c)).astype(BF16)

    @pl.when(i < ADALN_CAST_CHUNKS)
    def _():
        wout_b[...] = wout_c[0].astype(BF16)
        win_b[...] = win_c[0].astype(BF16)

    @pl.when(i == 0)
    def _():
        pad = jnp.zeros((GATE_PAD - 2 * M_HEADS, D_MODEL), F32)
        wg_b[...] = jnp.concatenate([wg_c[0], pad], axis=0).astype(BF16)

    @pl.when(i < N_ADA_TILES)
    def _():
        layer = i // (N_ADA_TILES // DEPTH)
        o_ref[:, 0, :] = _dot(cs, wa_ref[0].astype(BF16)) + ba_ref[pl.ds(layer, 1), :]

    @pl.when(i >= N_ADA_TILES)
    def _():
        o_ref[:, 0, :] = _dot(cs, wk_ref[...].astype(BF16)) + bk_ref[0]


def _adaln(c, ada_w, ada_b, kv_ada_w, kv_ada_b, a_w_out, w_in_t):
    batch = c.shape[0]
    wout_in, wout_out = _cast_specs(a_w_out, 0, ADALN_CAST_CHUNKS)
    win_rows = N_MAIN // ADALN_CAST_CHUNKS

    def cast_chunk(i):
        return jnp.minimum(i, ADALN_CAST_CHUNKS - 1)
    tiles_per_layer = N_MOD * D_MODEL // ADALN_COL_TILE
    bk = kv_ada_b.reshape(N_KV_TILES, 1, ADALN_COL_TILE)

    def ada_idx(i):
        return jnp.minimum(i, N_ADA_TILES - 1)

    def kv_idx(i):
        return jnp.maximum(i - N_ADA_TILES, 0)

    return pl.pallas_call(
        _adaln_kernel,
        grid=(N_ADA_TILES + N_KV_TILES,),
        in_specs=[
            _const_spec((batch, D_MODEL)),
            pl.BlockSpec((1, D_MODEL, ADALN_COL_TILE),
                         lambda i: (ada_idx(i) // tiles_per_layer, 0, ada_idx(i) % tiles_per_layer)),
            pl.BlockSpec((D_MODEL, ADALN_COL_TILE), lambda i: (0, kv_idx(i))),
            pl.BlockSpec((DEPTH, ADALN_COL_TILE), lambda i: (0, ada_idx(i) % tiles_per_layer)),
            pl.BlockSpec((1, 1, ADALN_COL_TILE), lambda i: (kv_idx(i), 0, 0)),
            wout_in,
            pl.BlockSpec((1, win_rows, D_MODEL), lambda i: (0, cast_chunk(i), 0)),
            pl.BlockSpec((1, 2 * M_HEADS, D_MODEL), lambda i: (0, N_MAIN // (2 * M_HEADS), 0)),
        ],
        out_specs=[pl.BlockSpec((batch, 1, ADALN_COL_TILE), lambda i: (0, 0, i)), wout_out,
                   pl.BlockSpec((win_rows, D_MODEL), lambda i: (cast_chunk(i), 0)),
                   _const_spec((GATE_PAD, D_MODEL))],
        out_shape=[jax.ShapeDtypeStruct((batch, 1, MOD_WIDTH), F32), _bf16_like(a_w_out),
                   jax.ShapeDtypeStruct((N_MAIN, D_MODEL), BF16), jax.ShapeDtypeStruct((GATE_PAD, D_MODEL), BF16)],
        compiler_params=_compiler_params(),
        name="adaln",
    )(c, ada_w, kv_ada_w, ada_b, bk, a_w_out, w_in_t, w_in_t)


NQ = M_HEADS * M_QK_DIM
N_MAIN = 2 * NQ + 2 * D_MODEL
GATE_PAD = LANES_V7X
GATE_QUANTS = 3
GATE_PART_ROWS = 32
GATE_EXPAND_COLS = GATE_QUANTS * M_HEADS * LANES_V7X


def _gate_expand_matrix():
    e = np.zeros((GATE_PART_ROWS, GATE_EXPAND_COLS), np.float32)
    for quant in range(GATE_QUANTS):
        for h in range(M_HEADS):
            row = quant * SUBLANES_V7X + M_HEADS + h
            grp = quant * M_HEADS + h
            e[row, grp * LANES_V7X:(grp + 1) * LANES_V7X] = 1.0
    return np.tile(e, (3, 1))


def _segment_scan(x, op, identity):
    pos = lax.broadcasted_iota(jnp.int32, x.shape, 1) % M_CHUNK
    shift = 1
    while shift < M_CHUNK:
        x = op(x, jnp.where(pos >= shift, pltpu.roll(x, shift, axis=1), identity))
        shift *= 2
    return x


def _gate_scan(g_tm):
    L = M_CHUNK
    nc = g_tm.shape[0] // L
    gates_t = jnp.concatenate(
        [g_tm[c * L:(c + 1) * L, :].T[0:2 * M_HEADS, :] for c in range(nc)], axis=1)
    capped = GATE_CAP * jnp.tanh(gates_t / GATE_CAP)
    log_f = jnp.minimum(capped, 0.0) - jnp.log1p(jnp.exp(-jnp.abs(capped)))
    bcum = _segment_scan(log_f * LOG2_E, jnp.add, 0.0)
    a = pltpu.roll(capped * LOG2_E, M_HEADS, axis=0) - bcum
    cmax = _segment_scan(a, jnp.maximum, -jnp.inf)
    stacked = jnp.concatenate([bcum, cmax, a, jnp.zeros_like(a)], axis=0)
    return a, jnp.concatenate(_split3_bf16(stacked), axis=0)


def _gate_expand(parts, gexp_ref):
    L = M_CHUNK
    return [_dot_tn(parts[:, c * L:(c + 1) * L], gexp_ref[...]) for c in range(parts.shape[1] // L)]


def _mlstm_tile(a_rows, cols, qk_s, v_s, h_s, c_s, n_s, m_s):
    L = M_CHUNK
    W = LANES_V7X
    nc = len(cols)
    heads = range(M_HEADS)
    causal = lax.broadcasted_iota(jnp.int32, (L, L), 1) <= lax.broadcasted_iota(jnp.int32, (L, L), 0)
    ones = jnp.ones((L, W), BF16)

    def rows(c):
        return slice(c * L, (c + 1) * L)

    def q_of(h, c):
        return qk_s[rows(c), h * M_QK_DIM:(h + 1) * M_QK_DIM]

    def k_of(h, c):
        return qk_s[rows(c), NQ + h * M_QK_DIM:NQ + (h + 1) * M_QK_DIM]

    def v1_of(h, c):
        return jnp.concatenate([v_s[rows(c), h * M_V_DIM:(h + 1) * M_V_DIM], ones], axis=1)

    def col(c, quant, h):
        g = quant * M_HEADS + h
        return cols[c][:, g * W:(g + 1) * W]

    scores = [[_dot_nt(q_of(h, c), k_of(h, c)) for c in range(nc)] for h in heads]

    m_in = [[None] * nc for _ in heads]
    mx = [[None] * nc for _ in heads]
    for h in heads:
        m_old = m_s[h:h + 1, :]
        for c in range(nc):
            m_in[h][c] = m_old
            mx[h][c] = jnp.maximum(m_old, col(c, 1, h)[L - 1:L, :])
            m_old = col(c, 0, h)[L - 1:L, :] + mx[h][c]
        m_s[h:h + 1, :] = m_old

    deltas = [[_dot_tn((k_of(h, c).astype(F32) * jnp.exp2(col(c, 2, h) - mx[h][c])).astype(BF16), v1_of(h, c))
               for c in range(nc)] for h in heads]

    for h in heads:
        c_t = c_s[h]
        n_bc = n_s[h]
        for c in range(nc):
            m_old = m_in[h][c]
            a_row = a_rows[M_HEADS + h:M_HEADS + h + 1, rows(c)]
            mt = jnp.maximum(col(c, 1, h), m_old)
            w_intra = jnp.exp2(jnp.where(causal, a_row - mt, -jnp.inf))
            w_inter = jnp.exp2(m_old - mt)
            e_neg = jnp.exp2(-col(c, 0, h) - mt)
            s_qk = scores[h][c] * w_intra
            lhs = jnp.concatenate([(q_of(h, c).astype(F32) * w_inter).astype(BF16), s_qk.astype(BF16)], axis=1)
            state = jnp.concatenate([c_t.astype(BF16), n_bc.astype(BF16)], axis=1)
            out = _dot(lhs, jnp.concatenate([state, v1_of(h, c)], axis=0))
            r_den = 1.0 / jnp.maximum(jnp.abs(out[:, M_V_DIM:M_V_DIM + W]), e_neg)
            for half in range(M_V_DIM // W):
                h_s[rows(c), h * M_V_DIM + half * W:h * M_V_DIM + (half + 1) * W] = (
                    out[:, half * W:(half + 1) * W] * r_den)
            decay = jnp.exp2(m_old - mx[h][c])
            c_t = jnp.concatenate([decay] * (M_V_DIM // W), axis=1) * c_t + deltas[h][c][:, 0:M_V_DIM]
            n_bc = decay * n_bc + deltas[h][c][:, M_V_DIM:M_V_DIM + W]
        c_s[h] = c_t
        n_s[h] = n_bc


def _mix0_tile(tiles_per_seq, x_ref, mod_ref, wg_ref, bg_ref, gexp_ref, nw_ref, lng_ref, lnb_ref,
               o_ref, win_s, wout_s, qk_s, v_s, og_s, h_s, c_s, n_s, m_s):
    @pl.when(pl.program_id(0) % tiles_per_seq == 0)
    def _():
        c_s[...] = jnp.zeros_like(c_s)
        n_s[...] = jnp.zeros_like(n_s)
        m_s[...] = jnp.zeros_like(m_s)

    x = x_ref[0]
    u = (x * (1.0 + _mod_slice(mod_ref, 1)) + _mod_slice(mod_ref, 0)).astype(BF16)
    gate_bias = jnp.concatenate([bg_ref[...], jnp.zeros((1, GATE_PAD - 2 * M_HEADS), F32)], axis=1)
    a_rows, parts = _gate_scan(_dot_nt(u, wg_ref[...]) + gate_bias)
    qk_s[:, 0:NQ] = _dot_nt(u, win_s[0:NQ, :]).astype(BF16)
    qk_s[:, NQ:2 * NQ] = (_dot_nt(u, win_s[NQ:2 * NQ, :]) * (1.0 / math.sqrt(M_QK_DIM))).astype(BF16)
    v_s[...] = _dot_nt(u, win_s[2 * NQ:2 * NQ + D_MODEL, :]).astype(BF16)
    og_s[...] = _dot_nt(u, win_s[2 * NQ + D_MODEL:N_MAIN, :])

    _mlstm_tile(a_rows, _gate_expand(parts, gexp_ref), qk_s, v_s, h_s, c_s, n_s, m_s)

    normed = []
    for h in range(M_HEADS):
        hh = h_s[:, h * M_V_DIM:(h + 1) * M_V_DIM]
        normed.append(hh * lax.rsqrt(jnp.mean(hh * hh, axis=-1, keepdims=True) + RMS_EPS))
    hn = jnp.concatenate(normed, axis=1)
    gated = hn * nw_ref[...] * jax.nn.sigmoid(og_s[...])
    y = _dot(gated.astype(BF16), wout_s[...])
    r = DEEPNORM_ALPHA * x + (1.0 + _mod_slice(mod_ref, 2)) * y
    o_ref[0] = _layer_norm(r, lng_ref[...], lnb_ref[...])


def _mix0_kernel(tiles_per_seq, x_ref, mod_ref, win_ref, wout_ref, wg_ref, bg_ref, gexp_ref, nw_ref,
                 lng_ref, lnb_ref, up_c, dn_c, kv_c, q_c,
                 o_ref, up_b, dn_b, kv_b, q_b, *scratch):
    up_b[...] = up_c[0].astype(BF16)
    dn_b[...] = dn_c[0].astype(BF16)
    kv_b[...] = kv_c[...].astype(BF16)
    wq = q_c[0]
    q_b[...] = jnp.concatenate(
        [wq[:, hd * A_HEAD_DIM:(hd + 1) * A_HEAD_DIM] for hd in HEAD_ORDER], axis=1).astype(BF16)
    _mix0_tile(tiles_per_seq, x_ref, mod_ref, wg_ref, bg_ref, gexp_ref, nw_ref, lng_ref, lnb_ref,
               o_ref, win_ref, wout_ref, *scratch)


def _mix0(x, mod, w_in, w_out, w_gate, b_gates, gate_expand, norm_w, ln_g, ln_b,
          mlp_w_up, mlp_w_down, w_kv, b_w_q):
    B, S, D = x.shape
    tm = ROW_TILE
    n_t = S // tm
    tiles = (B, n_t)
    casts = [_cast_specs(w, 0, B * n_t) for w in (mlp_w_up, mlp_w_down, w_kv, b_w_q)]
    return pl.pallas_call(
        functools.partial(_mix0_kernel, n_t),
        grid=(B * n_t,),
        in_specs=[
            _row_spec(tm, D, tiles),
            _mod_spec(tiles),
            _const_spec((N_MAIN, D)),
            _const_spec((D, D)),
            _const_spec((GATE_PAD, D)),
            _const_spec((1, 2 * M_HEADS)),
            _const_spec(gate_expand.shape),
            _const_spec((1, D)),
            _const_spec((1, D)),
            _const_spec((1, D)),
            *[c[0] for c in casts],
        ],
        out_specs=[_row_spec(tm, D, tiles), *[c[1] for c in casts]],
        out_shape=[jax.ShapeDtypeStruct((B, S, D), F32),
                   *[_bf16_like(w) for w in (mlp_w_up, mlp_w_down, w_kv, b_w_q)]],
        scratch_shapes=[
            pltpu.VMEM((tm, 2 * NQ), BF16),
            pltpu.VMEM((tm, D_MODEL), BF16),
            pltpu.VMEM((tm, D_MODEL), F32),
            pltpu.VMEM((tm, D_MODEL), F32),
            pltpu.VMEM((M_HEADS, M_QK_DIM, M_V_DIM), F32),
            pltpu.VMEM((M_HEADS, M_QK_DIM, LANES_V7X), F32),
            pltpu.VMEM((SUBLANES_V7X, LANES_V7X), F32),
        ],
        compiler_params=_compiler_params(),
        name="mix0",
    )(x, mod, w_in, w_out, w_gate, b_gates, gate_expand, norm_w, ln_g, ln_b,
      mlp_w_up, mlp_w_down, w_kv, b_w_q)


def _mlp_core(h, mod_ref, layer, wup_s, wdn_s, lng_ref, lnb_ref):
    base = layer * N_MOD
    u = (h * (1.0 + _mod_slice(mod_ref, base + 4)) + _mod_slice(mod_ref, base + 3)).astype(BF16)
    a = jnp.maximum(_dot(u, wup_s[...]), 0.0)
    y = _dot((a * a).astype(BF16), wdn_s[...])
    r = DEEPNORM_ALPHA * h + (1.0 + _mod_slice(mod_ref, base + 5)) * y
    return _layer_norm(r, lng_ref[...], lnb_ref[...])


def _rope_tables(pos_row, invf_ref, expand_ref):
    ang = pos_row.astype(F32) * invf_ref[...]
    trig = jnp.concatenate([jnp.cos(ang), jnp.sin(ang)], axis=0)
    parts = jnp.concatenate(_split3_bf16(trig), axis=0)
    tab = _dot_tn(parts, expand_ref[...])
    lane = lax.broadcasted_iota(jnp.int32, (1, LANES_V7X), 1)
    cos_t = tab[:, 0:LANES_V7X] + jnp.where(lane % A_HEAD_DIM >= ROPE_DIM, 1.0, 0.0)
    return cos_t, tab[:, LANES_V7X:2 * LANES_V7X], tab[:, 2 * LANES_V7X:3 * LANES_V7X]


def _rope(x, tables):
    cos_t, sin_up, sin_dn = tables
    out = []
    for g in range(x.shape[1] // LANES_V7X):
        xg = x[:, g * LANES_V7X:(g + 1) * LANES_V7X]
        from_lo = pltpu.roll(xg, ROPE_HALF, axis=1)
        from_hi = pltpu.roll(xg, LANES_V7X - ROPE_HALF, axis=1)
        out.append(xg * cos_t + from_lo * sin_up + from_hi * sin_dn)
    return jnp.concatenate(out, axis=1)


def _mlp0_kernel(n_tiles, tiles_per_seq, h_ref, mod_ref, modp_ref, posp_ref, wup_ref, wdn_ref, wkv_ref, wq_ref, lng_ref,
                 lnb_ref, invf_ref, expand_ref, up_c, dn_c, wo_c,
                 o_ref, q_ref, k_ref, v_ref, up_b, dn_b, wo_b, r_s):
    i = pl.program_id(0)

    def close_previous_tile():
        h1 = _layer_norm(r_s[...], lng_ref[...], lnb_ref[...])
        o_ref[0] = h1
        kv_shift = modp_ref[0, :, MOD_KV_BASE:MOD_KV_BASE + D_MODEL]
        kv_scale = modp_ref[0, :, MOD_KV_BASE + D_MODEL:MOD_KV_BASE + 2 * D_MODEL]
        kv = _dot((h1 * (1.0 + kv_scale) + kv_shift).astype(BF16), wkv_ref[...])
        uq = (h1 * (1.0 + _mod_slice(modp_ref, N_MOD + 1)) + _mod_slice(modp_ref, N_MOD)).astype(BF16)
        q = _dot(uq, wq_ref[...])
        batch = jnp.clip(i - 1, 0, n_tiles - 1) // tiles_per_seq
        tables = _rope_tables(posp_ref[pl.ds(batch, 1), :], invf_ref, expand_ref)
        q_ref[0] = (_rope(q, tables) * (LOG2_E / math.sqrt(A_HEAD_DIM))).astype(BF16)
        k_ref[0] = _rope(kv[:, 0:A_KV_DIM], tables).astype(BF16)
        v_ref[0] = kv[:, A_KV_DIM:2 * A_KV_DIM].astype(BF16)

    @pl.when(i == 0)
    def _():
        r_s[...] = jnp.zeros_like(r_s)

    @pl.when(i < n_tiles)
    def _():
        up_b[...] = up_c[0].astype(BF16)
        dn_b[...] = dn_c[0].astype(BF16)
        wo_b[...] = wo_c[0].astype(BF16)

        h = h_ref[0]
        u = (h * (1.0 + _mod_slice(mod_ref, 4)) + _mod_slice(mod_ref, 3)).astype(BF16)
        a = jnp.maximum(_dot(u, wup_ref[...]), 0.0)
        close_previous_tile()
        y = _dot((a * a).astype(BF16), wdn_ref[...])
        r_s[...] = DEEPNORM_ALPHA * h + (1.0 + _mod_slice(mod_ref, 5)) * y

    @pl.when(i == n_tiles)
    def _():
        close_previous_tile()


def _mlp1_kernel(h_ref, mod_ref, wup_ref, wdn_hbm, lng_ref, lnb_ref, o_ref, wdn_s, wdn_sem):
    i = pl.program_id(0)

    def wdn_copy():
        return pltpu.make_async_copy(wdn_hbm, wdn_s, wdn_sem)

    @pl.when(i == 0)
    def _():
        wdn_copy().start()
        h = h_ref[0]
        u = (h * (1.0 + _mod_slice(mod_ref, N_MOD + 4)) + _mod_slice(mod_ref, N_MOD + 3)).astype(BF16)
        a = jnp.maximum(_dot(u, wup_ref[...]), 0.0)
        wdn_copy().wait()
        y = _dot((a * a).astype(BF16), wdn_s[...])
        r = DEEPNORM_ALPHA * h + (1.0 + _mod_slice(mod_ref, N_MOD + 5)) * y
        o_ref[0] = _layer_norm(r, lng_ref[...], lnb_ref[...])

    @pl.when(i > 0)
    def _():
        o_ref[0] = _mlp_core(h_ref[0], mod_ref, 1, wup_ref, wdn_s, lng_ref, lnb_ref)


def _mlp0(h, mod, pos, w_up, w_dn, w_kv, w_q, ln_g, ln_b, inv_freq, expand, mlp_w_up, mlp_w_down, b_w_o):
    B, S, D = h.shape
    tm = ROW_TILE
    n_t = S // tm
    tiles = (B, n_t)
    n_tiles = B * n_t
    q_dim = A_HEADS * A_HEAD_DIM

    def pos_index(i):
        return 0, _tile_index(i, tiles, lag=1)[1]

    up_in, up_out = _cast_specs(mlp_w_up, 1, n_tiles)
    dn_in, dn_out = _cast_specs(mlp_w_down, 1, n_tiles)
    _, wo_out = _cast_specs(b_w_o, 0, n_tiles)
    wo_rows = q_dim // n_tiles
    subs = A_HEAD_DIM // wo_rows

    def wo_index(i):
        c = jnp.minimum(i, n_tiles - 1)
        slot, sub = c // subs, c % subs
        g, half = slot // 2, slot % 2
        return 0, (8 * (g // 4) + 4 * half + g % 4) * subs + sub, 0

    return pl.pallas_call(
        functools.partial(_mlp0_kernel, n_tiles, n_t),
        grid=(n_tiles + 1,),
        in_specs=[
            _row_spec(tm, D, tiles),
            _mod_spec(tiles),
            _mod_spec(tiles, lag=1),
            pl.BlockSpec((B, tm), pos_index),
            _const_spec((D, D_FF)),
            _const_spec((D_FF, D)),
            _const_spec((D, 2 * A_KV_DIM)),
            _const_spec((D, q_dim)),
            _const_spec((1, D)),
            _const_spec((1, D)),
            _const_spec((ROPE_HALF, 1)),
            _const_spec(expand.shape),
            up_in,
            dn_in,
            pl.BlockSpec((1, wo_rows, D), wo_index),
        ],
        out_specs=[_row_spec(tm, D, tiles, lag=1), _row_spec(tm, q_dim, tiles, lag=1),
                   _row_spec(tm, A_KV_DIM, tiles, lag=1), _row_spec(tm, A_KV_DIM, tiles, lag=1),
                   up_out, dn_out, wo_out],
        out_shape=[
            jax.ShapeDtypeStruct((B, S, D), F32),
            jax.ShapeDtypeStruct((B, S, q_dim), BF16),
            jax.ShapeDtypeStruct((B, S, A_KV_DIM), BF16),
            jax.ShapeDtypeStruct((B, S, A_KV_DIM), BF16),
            _bf16_like(mlp_w_up), _bf16_like(mlp_w_down), _bf16_like(b_w_o),
        ],
        scratch_shapes=[pltpu.VMEM((tm, D), F32)],
        compiler_params=_compiler_params(),
        name="mlp0",
    )(h, mod, mod, pos, w_up, w_dn, w_kv, w_q, ln_g, ln_b, inv_freq, expand, mlp_w_up, mlp_w_down, b_w_o)


def _mlp1(h, mod, w_up, w_dn, ln_g, ln_b):
    B, S, D = h.shape
    tm = WIDE_ROW_TILE
    n_t = S // tm
    tiles = (B, n_t)
    return pl.pallas_call(
        _mlp1_kernel,
        grid=(B * n_t,),
        in_specs=[
            _row_spec(tm, D, tiles),
            _mod_spec(tiles),
            _const_spec((D, D_FF)),
            pl.BlockSpec(memory_space=pl.ANY),
            _const_spec((1, D)),
            _const_spec((1, D)),
        ],
        out_specs=_row_spec(tm, D, tiles),
        out_shape=jax.ShapeDtypeStruct((B, S, D), F32),
        scratch_shapes=[pltpu.VMEM((D_FF, D), BF16), pltpu.SemaphoreType.DMA(())],
        compiler_params=_compiler_params(),
        name="mlp1",
    )(h, mod, w_up, w_dn, ln_g, ln_b)


HEAD_ORDER = tuple(8 * (G // 4) + 4 * half + (G % 4) for G in range(A_HEADS // 2) for half in range(2))
KV_PAIRS = A_KV_HEADS // 2


def _attn_pair_operands(k_band, v_band, p, lo_half, ones_bd):
    W = LANES_V7X
    zero = jnp.zeros((), BF16)
    kp = k_band[:, p * W:(p + 1) * W]
    vp = v_band[:, p * W:(p + 1) * W]
    k_bd = jnp.concatenate([jnp.where(lo_half, kp, zero), jnp.where(lo_half, zero, kp)], axis=0)
    v_bd = jnp.concatenate([jnp.where(lo_half, vp, zero), jnp.where(lo_half, zero, vp)], axis=0)
    return k_bd, jnp.concatenate([v_bd, ones_bd], axis=1)


def _attn_softmax_pv(s_all, v_ext, p, start_bias, sink_ref, prev_visible, lo_half_q):
    L = WINDOW
    W = LANES_V7X
    zero = jnp.zeros((), BF16)
    e_rows, sink_rows = [], []
    for j in range(A_GROUP):
        halves, sink_terms = [], []
        for half in range(2):
            c0 = half * 2 * L
            s = jnp.where(prev_visible, s_all[j * L:(j + 1) * L, c0:c0 + L], s_all[j * L:(j + 1) * L, c0 + L:c0 + 2 * L])
            if start_bias is not None:
                s = s + start_bias
            sink = sink_ref[0, HEAD_ORDER[2 * (4 * p + j) + half]] * LOG2_E
            mx = jnp.maximum(jnp.max(s, axis=1, keepdims=True), sink)
            e = jnp.exp2(s - mx).astype(BF16)
            halves += [jnp.where(prev_visible, e, zero), jnp.where(prev_visible, zero, e)]
            sink_terms.append(jnp.exp2(sink - mx))
        e_rows.append(jnp.concatenate(halves, axis=1))
        sink_rows.append(jnp.where(lo_half_q, sink_terms[0], sink_terms[1]))
    o_ext = _dot(jnp.concatenate(e_rows, axis=0), v_ext)
    o_all = o_ext[:, 0:W] / (o_ext[:, W:2 * W] + jnp.concatenate(sink_rows, axis=0))
    return [o_all[j * L:(j + 1) * L, :].astype(BF16) for j in range(A_GROUP)]


def _attn_tile(tiles_per_seq, sink_ref, q_ref, k_ref, kprev_ref, v_ref, vprev_ref, h_ref, mod_ref,
               wo_ref, lng_ref, lnb_ref, o_ref, att_s):
    L = WINDOW
    W = LANES_V7X
    tq = q_ref.shape[1]
    prev_visible = lax.broadcasted_iota(jnp.int32, (L, L), 1) > lax.broadcasted_iota(jnp.int32, (L, L), 0)
    seq_start = pl.program_id(0) % tiles_per_seq == 0
    start_bias = jnp.where(prev_visible & seq_start, -jnp.inf, 0.0)
    lo_half = lax.broadcasted_iota(jnp.int32, (2 * L, W), 1) < A_HEAD_DIM
    lo_half_q = lax.broadcasted_iota(jnp.int32, (L, W), 1) < A_HEAD_DIM
    ones_bd = jnp.concatenate([jnp.where(lo_half, 1.0, 0.0), jnp.where(lo_half, 0.0, 1.0)], axis=0).astype(BF16)

    for blk in range(tq // L):
        r0 = blk * L
        if blk == 0:
            k_prev, v_prev, blk_bias = kprev_ref[0], vprev_ref[0], start_bias
        else:
            k_prev, v_prev, blk_bias = k_ref[0, r0 - L:r0, :], v_ref[0, r0 - L:r0, :], None
        k_band = jnp.concatenate([k_prev, k_ref[0, r0:r0 + L, :]], axis=0)
        v_band = jnp.concatenate([v_prev, v_ref[0, r0:r0 + L, :]], axis=0)
        for p in range(KV_PAIRS):
            k_bd, v_ext = _attn_pair_operands(k_band, v_band, p, lo_half, ones_bd)
            lhs = jnp.concatenate(
                [q_ref[0, r0:r0 + L, (4 * p + j) * W:(4 * p + j + 1) * W] for j in range(A_GROUP)], axis=0)
            outs = _attn_softmax_pv(_dot_nt(lhs, k_bd), v_ext, p, blk_bias, sink_ref, prev_visible, lo_half_q)
            for j in range(A_GROUP):
                att_s[r0:r0 + L, (4 * p + j) * W:(4 * p + j + 1) * W] = outs[j]

    y = _dot(att_s[...], wo_ref[...])
    r = DEEPNORM_ALPHA * h_ref[0] + (1.0 + _mod_slice(mod_ref, N_MOD + 2)) * y
    o_ref[0] = _layer_norm(r, lng_ref[...], lnb_ref[...])


def _attn(sinks, q, k, v, h, mod, w_o, ln_g, ln_b):
    B, S, D = h.shape
    tq = WIDE_ROW_TILE
    n_t = S // tq
    tiles = (B, n_t)
    q_dim = A_HEADS * A_HEAD_DIM
    blocks_per_tile = tq // WINDOW

    def prev_index(i):
        b, t = _tile_index(i, tiles)
        return b, jnp.maximum(t * blocks_per_tile - 1, 0), 0

    return pl.pallas_call(
        functools.partial(_attn_tile, n_t),
        grid=(B * n_t,),
        in_specs=[
            pl.BlockSpec(memory_space=pltpu.SMEM),
            _row_spec(tq, q_dim, tiles),
            _row_spec(tq, A_KV_DIM, tiles),
            pl.BlockSpec((1, WINDOW, A_KV_DIM), prev_index),
            _row_spec(tq, A_KV_DIM, tiles),
            pl.BlockSpec((1, WINDOW, A_KV_DIM), prev_index),
            _row_spec(tq, D, tiles),
            _mod_spec(tiles),
            _const_spec((q_dim, D)),
            _const_spec((1, D)),
            _const_spec((1, D)),
        ],
        out_specs=_row_spec(tq, D, tiles),
        out_shape=jax.ShapeDtypeStruct((B, S, D), F32),
        scratch_shapes=[pltpu.VMEM((tq, q_dim), BF16)],
        compiler_params=_compiler_params(),
        name="attn",
    )(sinks, q, k, k, v, v, h, mod, w_o, ln_g, ln_b)


def _rope_expand_matrix():
    e = np.zeros((2 * ROPE_HALF, 3 * LANES_V7X), np.float32)
    for lane in range(LANES_V7X):
        d = lane % A_HEAD_DIM
        if d < ROPE_HALF:
            e[d, lane] = 1.0
            e[ROPE_HALF + d, 2 * LANES_V7X + lane] = -1.0
        elif d < ROPE_DIM:
            e[d - ROPE_HALF, lane] = 1.0
            e[d, LANES_V7X + lane] = 1.0
    return np.tile(e, (3, 1))


def kernel(x, c, positions, ada_w, ada_b, kv_ada_w, kv_ada_b, a_w_in, a_b_gates, a_norm_w, a_w_out,
           w_kv, b_w_q, b_sinks, b_w_o, mlp_w_up, mlp_w_down, ln_g, ln_b):
    B, S, D = x.shape
    assert D == D_MODEL and S % WIDE_ROW_TILE == 0 and WIDE_ROW_TILE % ROW_TILE == 0 and ROW_TILE % M_CHUNK == 0

    mod, w_out, w_in, w_gate = _adaln(c, ada_w, ada_b, kv_ada_w, kv_ada_b, a_w_out, jnp.swapaxes(a_w_in, 1, 2))

    h, w_up0, w_dn0, w_kv_b, w_q = _mix0(
        x, mod, w_in, w_out, w_gate, a_b_gates,
        jnp.asarray(_gate_expand_matrix(), BF16), a_norm_w[0].reshape(1, D),
        ln_g[0].reshape(1, D), ln_b[0].reshape(1, D),
        mlp_w_up, mlp_w_down, w_kv, b_w_q)

    inv_freq = (ROPE_THETA ** (-jnp.arange(ROPE_HALF, dtype=F32) / ROPE_HALF)).reshape(ROPE_HALF, 1)
    h, q, k, v, w_up1, w_dn1, w_o = _mlp0(
        h, mod, positions, w_up0, w_dn0, w_kv_b, w_q, ln_g[1].reshape(1, D),
        ln_b[1].reshape(1, D), inv_freq, jnp.asarray(_rope_expand_matrix(), BF16), mlp_w_up, mlp_w_down, b_w_o)

    h = _attn(b_sinks, q, k, v, h, mod, w_o, ln_g[2].reshape(1, D), ln_b[2].reshape(1, D))

    return _mlp1(h, mod, w_up1, w_dn1, ln_g[3].reshape(1, D), ln_b[3].reshape(1, D))
```

```python
import functools
import math

import numpy as np
import jax
import jax.numpy as jnp
from jax import lax
from jax.experimental import pallas as pl
from jax.experimental.pallas import tpu as pltpu

F32 = jnp.float32
BF16 = jnp.bfloat16

D_MODEL = 1024
DEPTH = 2
M_HEADS = 4
M_V_DIM = D_MODEL // M_HEADS
M_QK_DIM = M_V_DIM // 2
M_CHUNK = 128
GATE_CAP = 15.0
A_HEADS = 16
A_KV_HEADS = 4
A_GROUP = A_HEADS // A_KV_HEADS
A_HEAD_DIM = 64
A_KV_DIM = A_KV_HEADS * A_HEAD_DIM
WINDOW = 128
ROPE_DIM = A_HEAD_DIM // 4
ROPE_HALF = ROPE_DIM // 2
ROPE_THETA = 500000.0
D_FF = 4 * D_MODEL
DEEPNORM_ALPHA = (2 * DEPTH) ** 0.25
LN_EPS = 1e-5
RMS_EPS = 1e-6
LOG2_E = math.log2(math.e)
N_MOD = 6
MOD_KV_BASE = DEPTH * N_MOD * D_MODEL
MOD_WIDTH = MOD_KV_BASE + 2 * D_MODEL

LANES_V7X = 128
SUBLANES_V7X = 8
VMEM_LIMIT_BYTES_V7X = 56 * 1024 * 1024

ROW_TILE = 512
WIDE_ROW_TILE = 1024
ADALN_COL_TILE = 2048
ADALN_CAST_CHUNKS = 4

def _compiler_params():
    return pltpu.CompilerParams(dimension_semantics=("arbitrary",), vmem_limit_bytes=VMEM_LIMIT_BYTES_V7X)


def _const_spec(shape):
    return pl.BlockSpec(shape, lambda *_: (0,) * len(shape))


def _tile_index(i, tiles, lag=0):
    n_b, n_t = tiles
    j = jnp.clip(i - lag, 0, n_b * n_t - 1)
    return j // n_t, j % n_t


def _row_spec(tm, width, tiles, lag=0):
    def index(i):
        b, t = _tile_index(i, tiles, lag)
        return b, t, 0
    return pl.BlockSpec((1, tm, width), index)


def _mod_spec(tiles, lag=0):
    return pl.BlockSpec((1, 1, MOD_WIDTH), lambda i: (_tile_index(i, tiles, lag)[0], 0, 0))


def _cast_specs(w, layer, n_chunks):
    rows, cols = w.shape[-2:]
    r = rows // n_chunks

    def chunk(i):
        return jnp.minimum(i, n_chunks - 1)

    out_spec = pl.BlockSpec((r, cols), lambda i: (chunk(i), 0))
    if w.ndim == 2:
        return out_spec, out_spec
    return pl.BlockSpec((1, r, cols), lambda i: (layer, chunk(i), 0)), out_spec


def _bf16_like(w):
    return jax.ShapeDtypeStruct(w.shape[-2:], BF16)


def _layer_norm(r, g, b):
    mu = jnp.mean(r, axis=-1, keepdims=True)
    d = r - mu
    var = jnp.mean(d * d, axis=-1, keepdims=True)
    return d * lax.rsqrt(var + LN_EPS) * g + b


def _mod_slice(mod_ref, idx):
    return mod_ref[0, :, idx * D_MODEL:(idx + 1) * D_MODEL]


def _dot(a, b):
    return jnp.dot(a, b, preferred_element_type=F32)


def _dot_nt(a, b):
    return lax.dot_general(a, b, (((1,), (1,)), ((), ())), preferred_element_type=F32)


def _dot_tn(a, b):
    return lax.dot_general(a, b, (((0,), (0,)), ((), ())), preferred_element_type=F32)


def _split3_bf16(x):
    hi = x.astype(BF16)
    r1 = x - hi.astype(F32)
    mid = r1.astype(BF16)
    lo = (r1 - mid.astype(F32)).astype(BF16)
    return hi, mid, lo


N_ADA_TILES = DEPTH * N_MOD * D_MODEL // ADALN_COL_TILE
N_KV_TILES = 2 * D_MODEL // ADALN_COL_TILE


def _adaln_kernel(c_ref, wa_ref, wk_ref, ba_ref, bk_ref, wout_c, win_c, wg_c, o_ref, wout_b, win_b, wg_b):
    i = pl.program_id(0)
    c = c_ref[...]
    cs = (c * jax.nn.sigmoid(c)).astype(BF16)

    @pl.when(i < ADALN_CAST_CHUNKS)
    def _():
        wout_b[...] = wout_c[0].astype(BF16)
        win_b[...] = win_c[0].astype(BF16)

    @pl.when(i == 0)
    def _():
        pad = jnp.zeros((GATE_PAD - 2 * M_HEADS, D_MODEL), F32)
        wg_b[...] = jnp.concatenate([wg_c[0], pad], axis=0).astype(BF16)

    @pl.when(i < N_ADA_TILES)
    def _():
        layer = i // (N_ADA_TILES // DEPTH)
        o_ref[:, 0, :] = _dot(cs, wa_ref[0].astype(BF16)) + ba_ref[pl.ds(layer, 1), :]

    @pl.when(i >= N_ADA_TILES)
    def _():
        o_ref[:, 0, :] = _dot(cs, wk_ref[...].astype(BF16)) + bk_ref[0]


def _adaln(c, ada_w, ada_b, kv_ada_w, kv_ada_b, a_w_out, w_in_t):
    batch = c.shape[0]
    wout_in, wout_out = _cast_specs(a_w_out, 0, ADALN_CAST_CHUNKS)
    win_rows = N_MAIN // ADALN_CAST_CHUNKS

    def cast_chunk(i):
        return jnp.minimum(i, ADALN_CAST_CHUNKS - 1)
    tiles_per_layer = N_MOD * D_MODEL // ADALN_COL_TILE
    bk = kv_ada_b.reshape(N_KV_TILES, 1, ADALN_COL_TILE)

    def ada_idx(i):
        return jnp.minimum(i, N_ADA_TILES - 1)

    def kv_idx(i):
        return jnp.maximum(i - N_ADA_TILES, 0)

    return pl.pallas_call(
        _adaln_kernel,
        grid=(N_ADA_TILES + N_KV_TILES,),
        in_specs=[
            _const_spec((batch, D_MODEL)),
            pl.BlockSpec((1, D_MODEL, ADALN_COL_TILE),
                         lambda i: (ada_idx(i) // tiles_per_layer, 0, ada_idx(i) % tiles_per_layer)),
            pl.BlockSpec((D_MODEL, ADALN_COL_TILE), lambda i: (0, kv_idx(i))),
            pl.BlockSpec((DEPTH, ADALN_COL_TILE), lambda i: (0, ada_idx(i) % tiles_per_layer)),
            pl.BlockSpec((1, 1, ADALN_COL_TILE), lambda i: (kv_idx(i), 0, 0)),
            wout_in,
            pl.BlockSpec((1, win_rows, D_MODEL), lambda i: (0, cast_chunk(i), 0)),
            pl.BlockSpec((1, 2 * M_HEADS, D_MODEL), lambda i: (0, N_MAIN // (2 * M_HEADS), 0)),
        ],
        out_specs=[pl.BlockSpec((batch, 1, ADALN_COL_TILE), lambda i: (0, 0, i)), wout_out,
                   pl.BlockSpec((win_rows, D_MODEL), lambda i: (cast_chunk(i), 0)),
                   _const_spec((GATE_PAD, D_MODEL))],
        out_shape=[jax.ShapeDtypeStruct((batch, 1, MOD_WIDTH), F32), _bf16_like(a_w_out),
                   jax.ShapeDtypeStruct((N_MAIN, D_MODEL), BF16), jax.ShapeDtypeStruct((GATE_PAD, D_MODEL), BF16)],
        compiler_params=_compiler_params(),
        name="adaln",
    )(c, ada_w, kv_ada_w, ada_b, bk, a_w_out, w_in_t, w_in_t)


NQ = M_HEADS * M_QK_DIM
N_MAIN = 2 * NQ + 2 * D_MODEL
GATE_PAD = LANES_V7X
GATE_QUANTS = 3
GATE_PART_ROWS = 32
GATE_EXPAND_COLS = GATE_QUANTS * M_HEADS * LANES_V7X


def _gate_expand_matrix():
    e = np.zeros((GATE_PART_ROWS, GATE_EXPAND_COLS), np.float32)
    for quant in range(GATE_QUANTS):
        for h in range(M_HEADS):
            row = quant * SUBLANES_V7X + M_HEADS + h
            grp = quant * M_HEADS + h
            e[row, grp * LANES_V7X:(grp + 1) * LANES_V7X] = 1.0
    return np.tile(e, (3, 1))


def _segment_scan(x, op, identity):
    pos = lax.broadcasted_iota(jnp.int32, x.shape, 1) % M_CHUNK
    shift = 1
    while shift < M_CHUNK:
        x = op(x, jnp.where(pos >= shift, pltpu.roll(x, shift, axis=1), identity))
        shift *= 2
    return x


def _gate_scan(g_tm):
    L = M_CHUNK
    nc = g_tm.shape[0] // L
    gates_t = jnp.concatenate(
        [g_tm[c * L:(c + 1) * L, :].T[0:2 * M_HEADS, :] for c in range(nc)], axis=1)
    capped = GATE_CAP * jnp.tanh(gates_t / GATE_CAP)
    log_f = jnp.minimum(capped, 0.0) - jnp.log1p(jnp.exp(-jnp.abs(capped)))
    bcum = _segment_scan(log_f * LOG2_E, jnp.add, 0.0)
    a = pltpu.roll(capped * LOG2_E, M_HEADS, axis=0) - bcum
    cmax = _segment_scan(a, jnp.maximum, -jnp.inf)
    stacked = jnp.concatenate([bcum, cmax, a, jnp.zeros_like(a)], axis=0)
    return a, jnp.concatenate(_split3_bf16(stacked), axis=0)


def _gate_expand(parts, gexp_ref):
    L = M_CHUNK
    return [_dot_tn(parts[:, c * L:(c + 1) * L], gexp_ref[...]) for c in range(parts.shape[1] // L)]


def _mlstm_tile(a_rows, cols, qk_s, v_s, h_s, c_s, n_s, m_s):
    L = M_CHUNK
    W = LANES_V7X
    nc = len(cols)
    heads = range(M_HEADS)
    causal = lax.broadcasted_iota(jnp.int32, (L, L), 1) <= lax.broadcasted_iota(jnp.int32, (L, L), 0)
    ones = jnp.ones((L, W), BF16)

    def rows(c):
        return slice(c * L, (c + 1) * L)

    def q_of(h, c):
        return qk_s[rows(c), h * M_QK_DIM:(h + 1) * M_QK_DIM]

    def k_of(h, c):
        return qk_s[rows(c), NQ + h * M_QK_DIM:NQ + (h + 1) * M_QK_DIM]

    def v1_of(h, c):
        return jnp.concatenate([v_s[rows(c), h * M_V_DIM:(h + 1) * M_V_DIM], ones], axis=1)

    def col(c, quant, h):
        g = quant * M_HEADS + h
        return cols[c][:, g * W:(g + 1) * W]

    scores = [[_dot_nt(q_of(h, c), k_of(h, c)) for c in range(nc)] for h in heads]

    m_in = [[None] * nc for _ in heads]
    mx = [[None] * nc for _ in heads]
    for h in heads:
        m_old = m_s[h:h + 1, :]
        for c in range(nc):
            m_in[h][c] = m_old
            mx[h][c] = jnp.maximum(m_old, col(c, 1, h)[L - 1:L, :])
            m_old = col(c, 0, h)[L - 1:L, :] + mx[h][c]
        m_s[h:h + 1, :] = m_old

    deltas = [[_dot_tn((k_of(h, c).astype(F32) * jnp.exp2(col(c, 2, h) - mx[h][c])).astype(BF16), v1_of(h, c))
               for c in range(nc)] for h in heads]

    for h in heads:
        c_t = c_s[h]
        n_bc = n_s[h]
        for c in range(nc):
            m_old = m_in[h][c]
            a_row = a_rows[M_HEADS + h:M_HEADS + h + 1, rows(c)]
            mt = jnp.maximum(col(c, 1, h), m_old)
            w_intra = jnp.exp2(jnp.where(causal, a_row - mt, -jnp.inf))
            w_inter = jnp.exp2(m_old - mt)
            e_neg = jnp.exp2(-col(c, 0, h) - mt)
            s_qk = scores[h][c] * w_intra
            lhs = jnp.concatenate([(q_of(h, c).astype(F32) * w_inter).astype(BF16), s_qk.astype(BF16)], axis=1)
            state = jnp.concatenate([c_t.astype(BF16), n_bc.astype(BF16)], axis=1)
            out = _dot(lhs, jnp.concatenate([state, v1_of(h, c)], axis=0))
            r_den = 1.0 / jnp.maximum(jnp.abs(out[:, M_V_DIM:M_V_DIM + W]), e_neg)
            for half in range(M_V_DIM // W):
                h_s[rows(c), h * M_V_DIM + half * W:h * M_V_DIM + (half + 1) * W] = (
                    out[:, half * W:(half + 1) * W] * r_den)
            decay = jnp.exp2(m_old - mx[h][c])
            c_t = jnp.concatenate([decay] * (M_V_DIM // W), axis=1) * c_t + deltas[h][c][:, 0:M_V_DIM]
            n_bc = decay * n_bc + deltas[h][c][:, M_V_DIM:M_V_DIM + W]
        c_s[h] = c_t
        n_s[h] = n_bc


def _mix0_tile(tiles_per_seq, x_ref, mod_ref, wg_ref, bg_ref, gexp_ref, nw_ref, lng_ref, lnb_ref,
               o_ref, win_s, wout_s, qk_s, v_s, og_s, h_s, c_s, n_s, m_s):
    @pl.when(pl.program_id(0) % tiles_per_seq == 0)
    def _():
        c_s[...] = jnp.zeros_like(c_s)
        n_s[...] = jnp.zeros_like(n_s)
        m_s[...] = jnp.zeros_like(m_s)

    x = x_ref[0]
    u = (x * (1.0 + _mod_slice(mod_ref, 1)) + _mod_slice(mod_ref, 0)).astype(BF16)
    gate_bias = jnp.concatenate([bg_ref[...], jnp.zeros((1, GATE_PAD - 2 * M_HEADS), F32)], axis=1)
    a_rows, parts = _gate_scan(_dot_nt(u, wg_ref[...]) + gate_bias)
    qk_s[:, 0:NQ] = _dot_nt(u, win_s[0:NQ, :]).astype(BF16)
    qk_s[:, NQ:2 * NQ] = (_dot_nt(u, win_s[NQ:2 * NQ, :]) * (1.0 / math.sqrt(M_QK_DIM))).astype(BF16)
    v_s[...] = _dot_nt(u, win_s[2 * NQ:2 * NQ + D_MODEL, :]).astype(BF16)
    og_s[...] = _dot_nt(u, win_s[2 * NQ + D_MODEL:N_MAIN, :])

    _mlstm_tile(a_rows, _gate_expand(parts, gexp_ref), qk_s, v_s, h_s, c_s, n_s, m_s)

    normed = []
    for h in range(M_HEADS):
        hh = h_s[:, h * M_V_DIM:(h + 1) * M_V_DIM]
        normed.append(hh * lax.rsqrt(jnp.mean(hh * hh, axis=-1, keepdims=True) + RMS_EPS))
    hn = jnp.concatenate(normed, axis=1)
    gated = hn * nw_ref[...] * jax.nn.sigmoid(og_s[...])
    y = _dot(gated.astype(BF16), wout_s[...])
    r = DEEPNORM_ALPHA * x + (1.0 + _mod_slice(mod_ref, 2)) * y
    o_ref[0] = _layer_norm(r, lng_ref[...], lnb_ref[...])


def _mix0_kernel(tiles_per_seq, x_ref, mod_ref, win_ref, wout_ref, wg_ref, bg_ref, gexp_ref, nw_ref,
                 lng_ref, lnb_ref, up_c, dn_c, kv_c, q_c,
                 o_ref, up_b, dn_b, kv_b, q_b, *scratch):
    up_b[...] = up_c[0].astype(BF16)
    dn_b[...] = dn_c[0].astype(BF16)
    kv_b[...] = kv_c[...].astype(BF16)
    wq = q_c[0]
    q_b[...] = jnp.concatenate(
        [wq[:, hd * A_HEAD_DIM:(hd + 1) * A_HEAD_DIM] for hd in HEAD_ORDER], axis=1).astype(BF16)
    _mix0_tile(tiles_per_seq, x_ref, mod_ref, wg_ref, bg_ref, gexp_ref, nw_ref, lng_ref, lnb_ref,
               o_ref, win_ref, wout_ref, *scratch)


def _mix0(x, mod, w_in, w_out, w_gate, b_gates, gate_expand, norm_w, ln_g, ln_b,
          mlp_w_up, mlp_w_down, w_kv, b_w_q):
    B, S, D = x.shape
    tm = ROW_TILE
    n_t = S // tm
    tiles = (B, n_t)
    casts = [_cast_specs(w, 0, B * n_t) for w in (mlp_w_up, mlp_w_down, w_kv, b_w_q)]
    return pl.pallas_call(
        functools.partial(_mix0_kernel, n_t),
        grid=(B * n_t,),
        in_specs=[
            _row_spec(tm, D, tiles),
            _mod_spec(tiles),
            _const_spec((N_MAIN, D)),
            _const_spec((D, D)),
            _const_spec((GATE_PAD, D)),
            _const_spec((1, 2 * M_HEADS)),
            _const_spec(gate_expand.shape),
            _const_spec((1, D)),
            _const_spec((1, D)),
            _const_spec((1, D)),
            *[c[0] for c in casts],
        ],
        out_specs=[_row_spec(tm, D, tiles), *[c[1] for c in casts]],
        out_shape=[jax.ShapeDtypeStruct((B, S, D), F32),
                   *[_bf16_like(w) for w in (mlp_w_up, mlp_w_down, w_kv, b_w_q)]],
        scratch_shapes=[
            pltpu.VMEM((tm, 2 * NQ), BF16),
            pltpu.VMEM((tm, D_MODEL), BF16),
            pltpu.VMEM((tm, D_MODEL), F32),
            pltpu.VMEM((tm, D_MODEL), F32),
            pltpu.VMEM((M_HEADS, M_QK_DIM, M_V_DIM), F32),
            pltpu.VMEM((M_HEADS, M_QK_DIM, LANES_V7X), F32),
            pltpu.VMEM((SUBLANES_V7X, LANES_V7X), F32),
        ],
        compiler_params=_compiler_params(),
        name="mix0",
    )(x, mod, w_in, w_out, w_gate, b_gates, gate_expand, norm_w, ln_g, ln_b,
      mlp_w_up, mlp_w_down, w_kv, b_w_q)


def _mlp_core(h, mod_ref, layer, wup_s, wdn_s, lng_ref, lnb_ref):
    base = layer * N_MOD
    u = (h * (1.0 + _mod_slice(mod_ref, base + 4)) + _mod_slice(mod_ref, base + 3)).astype(BF16)
    a = jnp.maximum(_dot(u, wup_s[...]), 0.0)
    y = _dot((a * a).astype(BF16), wdn_s[...])
    r = DEEPNORM_ALPHA * h + (1.0 + _mod_slice(mod_ref, base + 5)) * y
    return _layer_norm(r, lng_ref[...], lnb_ref[...])


def _rope_tables(pos_row, invf_ref, expand_ref):
    ang = pos_row.astype(F32) * invf_ref[...]
    trig = jnp.concatenate([jnp.cos(ang), jnp.sin(ang)], axis=0)
    parts = jnp.concatenate(_split3_bf16(trig), axis=0)
    tab = _dot_tn(parts, expand_ref[...])
    lane = lax.broadcasted_iota(jnp.int32, (1, LANES_V7X), 1)
    cos_t = tab[:, 0:LANES_V7X] + jnp.where(lane % A_HEAD_DIM >= ROPE_DIM, 1.0, 0.0)
    return cos_t, tab[:, LANES_V7X:2 * LANES_V7X], tab[:, 2 * LANES_V7X:3 * LANES_V7X]


def _rope(x, tables):
    cos_t, sin_up, sin_dn = tables
    out = []
    for g in range(x.shape[1] // LANES_V7X):
        xg = x[:, g * LANES_V7X:(g + 1) * LANES_V7X]
        from_lo = pltpu.roll(xg, ROPE_HALF, axis=1)
        from_hi = pltpu.roll(xg, LANES_V7X - ROPE_HALF, axis=1)
        out.append(xg * cos_t + from_lo * sin_up + from_hi * sin_dn)
    return jnp.concatenate(out, axis=1)


def _mlp0_kernel(n_tiles, tiles_per_seq, h_ref, mod_ref, modp_ref, posp_ref, wup_ref, wdn_hbm, wkv_ref, wq_ref, lng_ref,
                 lnb_ref, invf_ref, expand_ref, up_c, dn_c, wo_c,
                 o_ref, q_ref, k_ref, v_ref, up_b, dn_b, wo_b, r_s, wdn_s, wdn_sem):
    i = pl.program_id(0)

    def close_previous_tile():
        h1 = _layer_norm(r_s[...], lng_ref[...], lnb_ref[...])
        o_ref[0] = h1
        kv_shift = modp_ref[0, :, MOD_KV_BASE:MOD_KV_BASE + D_MODEL]
        kv_scale = modp_ref[0, :, MOD_KV_BASE + D_MODEL:MOD_KV_BASE + 2 * D_MODEL]
        kv = _dot((h1 * (1.0 + kv_scale) + kv_shift).astype(BF16), wkv_ref[...])
        uq = (h1 * (1.0 + _mod_slice(modp_ref, N_MOD + 1)) + _mod_slice(modp_ref, N_MOD)).astype(BF16)
        q = _dot(uq, wq_ref[...])
        batch = jnp.clip(i - 1, 0, n_tiles - 1) // tiles_per_seq
        tables = _rope_tables(posp_ref[pl.ds(batch, 1), :], invf_ref, expand_ref)
        q_ref[0] = (_rope(q, tables) * (LOG2_E / math.sqrt(A_HEAD_DIM))).astype(BF16)
        k_ref[0] = _rope(kv[:, 0:A_KV_DIM], tables).astype(BF16)
        v_ref[0] = kv[:, A_KV_DIM:2 * A_KV_DIM].astype(BF16)

    def wdn_copy():
        return pltpu.make_async_copy(wdn_hbm, wdn_s, wdn_sem)

    def mlp_tile(first_step):
        up_b[...] = up_c[0].astype(BF16)
        dn_b[...] = dn_c[0].astype(BF16)
        wo_b[...] = wo_c[0].astype(BF16)

        h = h_ref[0]
        u = (h * (1.0 + _mod_slice(mod_ref, 4)) + _mod_slice(mod_ref, 3)).astype(BF16)
        a = jnp.maximum(_dot(u, wup_ref[...]), 0.0)
        close_previous_tile()
        if first_step:
            wdn_copy().wait()
        y = _dot((a * a).astype(BF16), wdn_s[...])
        r_s[...] = DEEPNORM_ALPHA * h + (1.0 + _mod_slice(mod_ref, 5)) * y

    @pl.when(i == 0)
    def _():
        wdn_copy().start()
        r_s[...] = jnp.zeros_like(r_s)
        mlp_tile(True)

    @pl.when((i > 0) & (i < n_tiles))
    def _():
        mlp_tile(False)

    @pl.when(i == n_tiles)
    def _():
        close_previous_tile()


def _mlp1_kernel(h_ref, mod_ref, wup_ref, wdn_hbm, lng_ref, lnb_ref, o_ref, wdn_s, wdn_sem):
    i = pl.program_id(0)

    def wdn_copy():
        return pltpu.make_async_copy(wdn_hbm, wdn_s, wdn_sem)

    @pl.when(i == 0)
    def _():
        wdn_copy().start()
        h = h_ref[0]
        u = (h * (1.0 + _mod_slice(mod_ref, N_MOD + 4)) + _mod_slice(mod_ref, N_MOD + 3)).astype(BF16)
        a = jnp.maximum(_dot(u, wup_ref[...]), 0.0)
        wdn_copy().wait()
        y = _dot((a * a).astype(BF16), wdn_s[...])
        r = DEEPNORM_ALPHA * h + (1.0 + _mod_slice(mod_ref, N_MOD + 5)) * y
        o_ref[0] = _layer_norm(r, lng_ref[...], lnb_ref[...])

    @pl.when(i > 0)
    def _():
        o_ref[0] = _mlp_core(h_ref[0], mod_ref, 1, wup_ref, wdn_s, lng_ref, lnb_ref)


def _mlp0(h, mod, pos, w_up, w_dn, w_kv, w_q, ln_g, ln_b, inv_freq, expand, mlp_w_up, mlp_w_down, b_w_o):
    B, S, D = h.shape
    tm = ROW_TILE
    n_t = S // tm
    tiles = (B, n_t)
    n_tiles = B * n_t
    q_dim = A_HEADS * A_HEAD_DIM

    def pos_index(i):
        return 0, _tile_index(i, tiles, lag=1)[1]

    up_in, up_out = _cast_specs(mlp_w_up, 1, n_tiles)
    dn_in, dn_out = _cast_specs(mlp_w_down, 1, n_tiles)
    _, wo_out = _cast_specs(b_w_o, 0, n_tiles)
    wo_rows = q_dim // n_tiles
    subs = A_HEAD_DIM // wo_rows

    def wo_index(i):
        c = jnp.minimum(i, n_tiles - 1)
        slot, sub = c // subs, c % subs
        g, half = slot // 2, slot % 2
        return 0, (8 * (g // 4) + 4 * half + g % 4) * subs + sub, 0

    return pl.pallas_call(
        functools.partial(_mlp0_kernel, n_tiles, n_t),
        grid=(n_tiles + 1,),
        in_specs=[
            _row_spec(tm, D, tiles),
            _mod_spec(tiles),
            _mod_spec(tiles, lag=1),
            pl.BlockSpec((B, tm), pos_index),
            _const_spec((D, D_FF)),
            pl.BlockSpec(memory_space=pl.ANY),
            _const_spec((D, 2 * A_KV_DIM)),
            _const_spec((D, q_dim)),
            _const_spec((1, D)),
            _const_spec((1, D)),
            _const_spec((ROPE_HALF, 1)),
            _const_spec(expand.shape),
            up_in,
            dn_in,
            pl.BlockSpec((1, wo_rows, D), wo_index),
        ],
        out_specs=[_row_spec(tm, D, tiles, lag=1), _row_spec(tm, q_dim, tiles, lag=1),
                   _row_spec(tm, A_KV_DIM, tiles, lag=1), _row_spec(tm, A_KV_DIM, tiles, lag=1),
                   up_out, dn_out, wo_out],
        out_shape=[
            jax.ShapeDtypeStruct((B, S, D), F32),
            jax.ShapeDtypeStruct((B, S, q_dim), BF16),
            jax.ShapeDtypeStruct((B, S, A_KV_DIM), BF16),
            jax.ShapeDtypeStruct((B, S, A_KV_DIM), BF16),
            _bf16_like(mlp_w_up), _bf16_like(mlp_w_down), _bf16_like(b_w_o),
        ],
        scratch_shapes=[pltpu.VMEM((tm, D), F32),
                        pltpu.VMEM((D_FF, D), BF16), pltpu.SemaphoreType.DMA(())],
        compiler_params=_compiler_params(),
        name="mlp0",
    )(h, mod, mod, pos, w_up, w_dn, w_kv, w_q, ln_g, ln_b, inv_freq, expand, mlp_w_up, mlp_w_down, b_w_o)


def _mlp1(h, mod, w_up, w_dn, ln_g, ln_b):
    B, S, D = h.shape
    tm = WIDE_ROW_TILE
    n_t = S // tm
    tiles = (B, n_t)
    return pl.pallas_call(
        _mlp1_kernel,
        grid=(B * n_t,),
        in_specs=[
            _row_spec(tm, D, tiles),
            _mod_spec(tiles),
            _const_spec((D, D_FF)),
            pl.BlockSpec(memory_space=pl.ANY),
            _const_spec((1, D)),
            _const_spec((1, D)),
        ],
        out_specs=_row_spec(tm, D, tiles),
        out_shape=jax.ShapeDtypeStruct((B, S, D), F32),
        scratch_shapes=[pltpu.VMEM((D_FF, D), BF16), pltpu.SemaphoreType.DMA(())],
        compiler_params=_compiler_params(),
        name="mlp1",
    )(h, mod, w_up, w_dn, ln_g, ln_b)


HEAD_ORDER = tuple(8 * (G // 4) + 4 * half + (G % 4) for G in range(A_HEADS // 2) for half in range(2))
KV_PAIRS = A_KV_HEADS // 2


def _attn_pair_operands(k_band, v_band, p, lo_half, ones_bd):
    W = LANES_V7X
    zero = jnp.zeros((), BF16)
    kp = k_band[:, p * W:(p + 1) * W]
    vp = v_band[:, p * W:(p + 1) * W]
    k_bd = jnp.concatenate([jnp.where(lo_half, kp, zero), jnp.where(lo_half, zero, kp)], axis=0)
    v_bd = jnp.concatenate([jnp.where(lo_half, vp, zero), jnp.where(lo_half, zero, vp)], axis=0)
    return k_bd, jnp.concatenate([v_bd, ones_bd], axis=1)


def _attn_softmax_pv(s_all, v_ext, p, start_bias, sink_ref, prev_visible, lo_half_q):
    L = WINDOW
    W = LANES_V7X
    zero = jnp.zeros((), BF16)
    e_rows, sink_rows = [], []
    for j in range(A_GROUP):
        halves, sink_terms = [], []
        for half in range(2):
            c0 = half * 2 * L
            s = jnp.where(prev_visible, s_all[j * L:(j + 1) * L, c0:c0 + L], s_all[j * L:(j + 1) * L, c0 + L:c0 + 2 * L])
            if start_bias is not None:
                s = s + start_bias
            sink = sink_ref[0, HEAD_ORDER[2 * (4 * p + j) + half]] * LOG2_E
            mx = jnp.maximum(jnp.max(s, axis=1, keepdims=True), sink)
            e = jnp.exp2(s - mx).astype(BF16)
            halves += [jnp.where(prev_visible, e, zero), jnp.where(prev_visible, zero, e)]
            sink_terms.append(jnp.exp2(sink - mx))
        e_rows.append(jnp.concatenate(halves, axis=1))
        sink_rows.append(jnp.where(lo_half_q, sink_terms[0], sink_terms[1]))
    o_ext = _dot(jnp.concatenate(e_rows, axis=0), v_ext)
    o_all = o_ext[:, 0:W] / (o_ext[:, W:2 * W] + jnp.concatenate(sink_rows, axis=0))
    return [o_all[j * L:(j + 1) * L, :].astype(BF16) for j in range(A_GROUP)]


def _attn_tile(tiles_per_seq, sink_ref, q_ref, k_ref, kprev_ref, v_ref, vprev_ref, h_ref, mod_ref,
               wo_ref, lng_ref, lnb_ref, o_ref, att_s):
    L = WINDOW
    W = LANES_V7X
    tq = q_ref.shape[1]
    prev_visible = lax.broadcasted_iota(jnp.int32, (L, L), 1) > lax.broadcasted_iota(jnp.int32, (L, L), 0)
    seq_start = pl.program_id(0) % tiles_per_seq == 0
    start_bias = jnp.where(prev_visible & seq_start, -jnp.inf, 0.0)
    lo_half = lax.broadcasted_iota(jnp.int32, (2 * L, W), 1) < A_HEAD_DIM
    lo_half_q = lax.broadcasted_iota(jnp.int32, (L, W), 1) < A_HEAD_DIM
    ones_bd = jnp.concatenate([jnp.where(lo_half, 1.0, 0.0), jnp.where(lo_half, 0.0, 1.0)], axis=0).astype(BF16)

    for blk in range(tq // L):
        r0 = blk * L
        if blk == 0:
            k_prev, v_prev, blk_bias = kprev_ref[0], vprev_ref[0], start_bias
        else:
            k_prev, v_prev, blk_bias = k_ref[0, r0 - L:r0, :], v_ref[0, r0 - L:r0, :], None
        k_band = jnp.concatenate([k_prev, k_ref[0, r0:r0 + L, :]], axis=0)
        v_band = jnp.concatenate([v_prev, v_ref[0, r0:r0 + L, :]], axis=0)
        for p in range(KV_PAIRS):
            k_bd, v_ext = _attn_pair_operands(k_band, v_band, p, lo_half, ones_bd)
            lhs = jnp.concatenate(
                [q_ref[0, r0:r0 + L, (4 * p + j) * W:(4 * p + j + 1) * W] for j in range(A_GROUP)], axis=0)
            outs = _attn_softmax_pv(_dot_nt(lhs, k_bd), v_ext, p, blk_bias, sink_ref, prev_visible, lo_half_q)
            for j in range(A_GROUP):
                att_s[r0:r0 + L, (4 * p + j) * W:(4 * p + j + 1) * W] = outs[j]

    y = _dot(att_s[...], wo_ref[...])
    r = DEEPNORM_ALPHA * h_ref[0] + (1.0 + _mod_slice(mod_ref, N_MOD + 2)) * y
    o_ref[0] = _layer_norm(r, lng_ref[...], lnb_ref[...])


def _attn(sinks, q, k, v, h, mod, w_o, ln_g, ln_b):
    B, S, D = h.shape
    tq = WIDE_ROW_TILE
    n_t = S // tq
    tiles = (B, n_t)
    q_dim = A_HEADS * A_HEAD_DIM
    blocks_per_tile = tq // WINDOW

    def prev_index(i):
        b, t = _tile_index(i, tiles)
        return b, jnp.maximum(t * blocks_per_tile - 1, 0), 0

    return pl.pallas_call(
        functools.partial(_attn_tile, n_t),
        grid=(B * n_t,),
        in_specs=[
            pl.BlockSpec(memory_space=pltpu.SMEM),
            _row_spec(tq, q_dim, tiles),
            _row_spec(tq, A_KV_DIM, tiles),
            pl.BlockSpec((1, WINDOW, A_KV_DIM), prev_index),
            _row_spec(tq, A_KV_DIM, tiles),
            pl.BlockSpec((1, WINDOW, A_KV_DIM), prev_index),
            _row_spec(tq, D, tiles),
            _mod_spec(tiles),
            _const_spec((q_dim, D)),
            _const_spec((1, D)),
            _const_spec((1, D)),
        ],
        out_specs=_row_spec(tq, D, tiles),
        out_shape=jax.ShapeDtypeStruct((B, S, D), F32),
        scratch_shapes=[pltpu.VMEM((tq, q_dim), BF16)],
        compiler_params=_compiler_params(),
        name="attn",
    )(sinks, q, k, k, v, v, h, mod, w_o, ln_g, ln_b)


def _rope_expand_matrix():
    e = np.zeros((2 * ROPE_HALF, 3 * LANES_V7X), np.float32)
    for lane in range(LANES_V7X):
        d = lane % A_HEAD_DIM
        if d < ROPE_HALF:
            e[d, lane] = 1.0
            e[ROPE_HALF + d, 2 * LANES_V7X + lane] = -1.0
        elif d < ROPE_DIM:
            e[d - ROPE_HALF, lane] = 1.0
            e[d, LANES_V7X + lane] = 1.0
    return np.tile(e, (3, 1))


def kernel(x, c, positions, ada_w, ada_b, kv_ada_w, kv_ada_b, a_w_in, a_b_gates, a_norm_w, a_w_out,
           w_kv, b_w_q, b_sinks, b_w_o, mlp_w_up, mlp_w_down, ln_g, ln_b):
    B, S, D = x.shape
    assert D == D_MODEL and S % WIDE_ROW_TILE == 0 and WIDE_ROW_TILE % ROW_TILE == 0 and ROW_TILE % M_CHUNK == 0

    mod, w_out, w_in, w_gate = _adaln(c, ada_w, ada_b, kv_ada_w, kv_ada_b, a_w_out, jnp.swapaxes(a_w_in, 1, 2))

    h, w_up0, w_dn0, w_kv_b, w_q = _mix0(
        x, mod, w_in, w_out, w_gate, a_b_gates,
        jnp.asarray(_gate_expand_matrix(), BF16), a_norm_w[0].reshape(1, D),
        ln_g[0].reshape(1, D), ln_b[0].reshape(1, D),
        mlp_w_up, mlp_w_down, w_kv, b_w_q)

    inv_freq = (ROPE_THETA ** (-jnp.arange(ROPE_HALF, dtype=F32) / ROPE_HALF)).reshape(ROPE_HALF, 1)
    h, q, k, v, w_up1, w_dn1, w_o = _mlp0(
        h, mod, positions, w_up0, w_dn0, w_kv_b, w_q, ln_g[1].reshape(1, D),
        ln_b[1].reshape(1, D), inv_freq, jnp.asarray(_rope_expand_matrix(), BF16), mlp_w_up, mlp_w_down, b_w_o)

    h = _attn(b_sinks, q, k, v, h, mod, w_o, ln_g[2].reshape(1, D), ln_b[2].reshape(1, D))

    return _mlp1(h, mod, w_up1, w_dn1, ln_g[3].reshape(1, D), ln_b[3].reshape(1, D))
```
